```python
import jax, jax.numpy as jnp
from jax import lax
import numpy as np

D_MODEL = 2048
BATCH = 1
SEQ = 16384
DEPTH = 2
DEC_BATCH = 2
DEC_SEQ = 4096
PAST_LEN = 128

N_META = 16
GRID_W = 64
ATT_HEAD_DIM = 64
ATT_HEADS = D_MODEL // 128
D_ATT = ATT_HEADS * ATT_HEAD_DIM
WIN_R = 8
WIN_C = 16
FNET_GROUPS = 4
D_FNET = D_MODEL // 2
FNET_GROUP_DIM = D_FNET // FNET_GROUPS
SSM_HEAD_DIM = 64
SSM_HEADS = D_MODEL // 128
D_SSM = SSM_HEADS * SSM_HEAD_DIM
SSM_GROUPS = 4
SSM_STATE = 128
SSM_CONV = 3
SSM_CHUNK = 128
D_SSM_CONV = D_SSM + 2 * SSM_GROUPS * SSM_STATE
N_BRANCHES = 3
D_IN_PROJ = 3 * D_ATT + D_FNET + D_SSM_CONV + D_SSM + 2 * SSM_HEADS + N_BRANCHES * D_MODEL
N_EXPERTS = 16
EC_CAPACITY = 2
D_EXPERT = D_MODEL
EPS = 1e-6

kernel_name = "hybrid_natten_fnet_ssd_ec_encoder"


def rmsnorm(x, gain):
    xf = x.astype(jnp.float32)
    y = xf * lax.rsqrt(jnp.mean(xf * xf, axis=-1, keepdims=True) + EPS)
    return (y * gain.astype(jnp.float32)).astype(x.dtype)


def _column_window_tables():
    c = np.arange(GRID_W)
    cs = np.clip(c - WIN_C // 2, 0, GRID_W - WIN_C)
    col_idx = cs[:, None] + np.arange(WIN_C)[None, :]
    dc_idx = col_idx - c[:, None] + (WIN_C - 1)
    return col_idx, dc_idx


def neighbourhood_attention(q, k, v, rel_bias, meta_bias):
    b, t, h, dh = q.shape
    rows = (t - N_META) // GRID_W
    wr = min(WIN_R, rows)
    scale = dh ** -0.5
    qm, km, vm = q[:, :N_META], k[:, :N_META], v[:, :N_META]
    kg = k[:, N_META:].reshape(b, rows, GRID_W, h, dh)
    vg = v[:, N_META:].reshape(b, rows, GRID_W, h, dh)
    qg = q[:, N_META:].reshape(b, rows, GRID_W, h, dh)
    col_idx, dc_idx = _column_window_tables()
    mb = meta_bias.astype(jnp.float32)[:, None, :]

    def one_row(args):
        r, q_row = args
        rs = jnp.clip(r - wr // 2, 0, rows - wr)
        k_nb = lax.dynamic_slice_in_dim(kg, rs, wr, axis=1)[:, :, col_idx]
        v_nb = lax.dynamic_slice_in_dim(vg, rs, wr, axis=1)[:, :, col_idx]
        rel = lax.dynamic_slice_in_dim(rel_bias, rs - r + (WIN_R - 1), wr, axis=1)[:, :, dc_idx]
        s_nb = (jnp.einsum('bqhd,bwqjhd->bhqwj', q_row, k_nb).astype(jnp.float32) * scale
                + jnp.transpose(rel, (0, 2, 1, 3)).astype(jnp.float32))
        s_meta = jnp.einsum('bqhd,bmhd->bhqm', q_row, km).astype(jnp.float32) * scale + mb
        logits = jnp.concatenate([s_meta, s_nb.reshape(b, h, GRID_W, wr * WIN_C)], axis=-1)
        p = jax.nn.softmax(logits, axis=-1)
        p_nb = p[..., N_META:].reshape(b, h, GRID_W, wr, WIN_C)
        o = (jnp.einsum('bhqm,bmhd->bqhd', p[..., :N_META], vm)
             + jnp.einsum('bhqwj,bwqjhd->bqhd', p_nb, v_nb))
        return o.astype(q.dtype)

    out_rows = lax.map(one_row, (jnp.arange(rows), jnp.moveaxis(qg, 1, 0)))
    out_grid = jnp.moveaxis(out_rows, 0, 1).reshape(b, rows * GRID_W, h, dh)
    s_mm = jnp.einsum('bqhd,bmhd->bhqm', qm, km).astype(jnp.float32) * scale + mb
    out_meta = jnp.einsum('bhqm,bmhd->bqhd', jax.nn.softmax(s_mm, axis=-1), vm).astype(q.dtype)
    return jnp.concatenate([out_meta, out_grid], axis=1)


def fourier_mix(u):
    b, t, _ = u.shape
    uf = u.astype(jnp.float32).reshape(b, t, FNET_GROUPS, FNET_GROUP_DIM)
    y = jnp.fft.fftn(uf, axes=(1, 3), norm='ortho').real
    return y.reshape(b, t, D_FNET).astype(u.dtype)


def centred_dwconv(u, w, bias):
    y = lax.conv_general_dilated(u, w[:, None, :].astype(u.dtype), window_strides=(1,),
                                 padding=[(SSM_CONV // 2, SSM_CONV // 2)],
                                 dimension_numbers=('NWC', 'WIO', 'NWC'),
                                 feature_group_count=u.shape[-1])
    return y + bias.astype(u.dtype)


def ssd_chunked(x, dt, a, bm, cm):
    bsz, t, h, p = x.shape
    g, n = bm.shape[2], bm.shape[3]
    kh = h // g
    c = t // SSM_CHUNK
    x = x.reshape(bsz, c, SSM_CHUNK, g, kh, p)
    dt = dt.reshape(bsz, c, SSM_CHUNK, g, kh)
    bm = bm.reshape(bsz, c, SSM_CHUNK, g, n)
    cm = cm.reshape(bsz, c, SSM_CHUNK, g, n)
    xdt = x * dt[..., None]
    acum = jnp.moveaxis(jnp.cumsum(dt * a.reshape(g, kh), axis=2), 2, -1)
    causal = np.tril(np.ones((SSM_CHUNK, SSM_CHUNK), dtype=bool))
    seg = acum[..., :, None] - acum[..., None, :]
    decay = jnp.exp(jnp.where(causal, seg, -jnp.inf))
    cb = jnp.einsum('bclgn,bcsgn->bcgls', cm, bm)
    y_diag = jnp.einsum('bcgls,bcgkls,bcsgkp->bclgkp', cb, decay, xdt)
    end_decay = jnp.exp(acum[..., -1:] - acum)
    states = jnp.einsum('bclgn,bcgkl,bclgkp->bcgkpn', bm, end_decay, xdt)
    chunk_decay = jnp.exp(acum[..., -1])

    def step(carry, inp):
        st, dec = inp
        return carry * dec[..., None, None] + st, carry

    init = jnp.zeros((bsz, g, kh, p, n), states.dtype)
    _, prev = lax.scan(step, init, (jnp.moveaxis(states, 1, 0), jnp.moveaxis(chunk_decay, 1, 0)))
    prev = jnp.moveaxis(prev, 0, 1)
    y_off = jnp.einsum('bclgn,bcgkpn,bcgkl->bclgkp', cm, prev, jnp.exp(acum))
    return (y_diag + y_off).reshape(bsz, t, h, p)


def ssd_direction(xs, dt, a, bm, cm, reverse):
    t = xs.shape[1]
    pad = SSM_CHUNK - N_META
    if reverse:
        xs, dt, bm, cm = jnp.flip(xs, 1), jnp.flip(dt, 1), jnp.flip(bm, 1), jnp.flip(cm, 1)
    cfg = (0, pad) if reverse else (pad, 0)

    def padt(u):
        return jnp.pad(u, [(0, 0), cfg] + [(0, 0)] * (u.ndim - 2))

    y = ssd_chunked(padt(xs), padt(dt), a, padt(bm), padt(cm))
    return jnp.flip(y[:, :t], 1) if reverse else y[:, pad:]


def ssd_mixer(xbc, z, dt_raw, conv_w, conv_b, a_log, dt_bias, d_skip, norm_gain):
    b, t, _ = xbc.shape
    xbc = jax.nn.silu(centred_dwconv(xbc, conv_w, conv_b))
    xs, bm, cm = jnp.split(xbc, [D_SSM, D_SSM + SSM_GROUPS * SSM_STATE], axis=-1)
    xs = xs.reshape(b, t, SSM_HEADS, SSM_HEAD_DIM)
    bm = bm.reshape(b, t, SSM_GROUPS, SSM_STATE)
    cm = cm.reshape(b, t, SSM_GROUPS, SSM_STATE)
    dt = jax.nn.softplus((dt_raw.reshape(b, t, 2, SSM_HEADS) + dt_bias).astype(jnp.float32))
    a = -jnp.exp(a_log.astype(jnp.float32))
    y = (ssd_direction(xs, dt[:, :, 0], a[0], bm, cm, False)
         + ssd_direction(xs, dt[:, :, 1], a[1], bm, cm, True)
         + xs * d_skip[:, None])
    y = y.reshape(b, t, D_SSM) * jax.nn.silu(z)
    return rmsnorm(y, norm_gain)


def expert_choice_ffn(h, w_router, w_exp_gate, w_exp_up, w_exp_down):
    b, t, d = h.shape
    tokens = h.reshape(b * t, d)
    cap = (EC_CAPACITY * b * t) // N_EXPERTS
    aff = jax.nn.softmax((tokens @ w_router).astype(jnp.float32), axis=-1)
    gate, idx = lax.top_k(aff.T, cap)
    xe = tokens[idx]
    hid = jax.nn.silu(jnp.einsum('ecd,edf->ecf', xe, w_exp_gate)) * jnp.einsum('ecd,edf->ecf', xe, w_exp_up)
    ye = jnp.einsum('ecf,efd->ecd', hid, w_exp_down) * gate[..., None]
    out = jnp.zeros_like(tokens).at[idx.reshape(-1)].add(ye.reshape(-1, d).astype(tokens.dtype))
    return out.reshape(b, t, d)


def encoder_layer(s, norm1_gain, w_in, rel_bias, meta_bias, conv_w, conv_b, a_log, dt_bias, d_skip,
                  ssd_norm_gain, w_branch_a, w_branch_f, w_branch_s, w_out, norm2_gain, w_router,
                  w_exp_gate, w_exp_up, w_exp_down):
    b, t, _ = s.shape
    h = rmsnorm(s, norm1_gain)
    proj = h @ w_in
    sizes = (3 * D_ATT, D_FNET, D_SSM_CONV, D_SSM, 2 * SSM_HEADS)
    cuts = [int(i) for i in np.cumsum(sizes)]
    qkv, u_f, xbc, z, dt_raw, gate_in = jnp.split(proj, cuts, axis=-1)
    qkv = qkv.reshape(b, t, 3, ATT_HEADS, ATT_HEAD_DIM)
    y_att = neighbourhood_attention(qkv[:, :, 0], qkv[:, :, 1], qkv[:, :, 2], rel_bias, meta_bias)
    y_att = y_att.reshape(b, t, D_ATT)
    y_f = fourier_mix(u_f)
    y_s = ssd_mixer(xbc, z, dt_raw, conv_w, conv_b, a_log, dt_bias, d_skip, ssd_norm_gain).astype(s.dtype)
    gates = jax.nn.sigmoid(gate_in.astype(jnp.float32)).reshape(b, t, N_BRANCHES, D_MODEL)
    merged = (gates[:, :, 0] * (y_att @ w_branch_a) + gates[:, :, 1] * (y_f @ w_branch_f)
              + gates[:, :, 2] * (y_s @ w_branch_s))
    s = s + merged.astype(s.dtype) @ w_out
    s = s + expert_choice_ffn(rmsnorm(s, norm2_gain), w_router, w_exp_gate, w_exp_up, w_exp_down)
    return s


def encode(x, meta_tokens, final_gain, layer_weights):
    b = x.shape[0]
    meta = jnp.broadcast_to(meta_tokens.astype(x.dtype)[None], (b, N_META, D_MODEL))
    s = jnp.concatenate([meta, x], axis=1)
    for layer in range(DEPTH):
        s = encoder_layer(s, *[w[layer] for w in layer_weights])
    return rmsnorm(s, final_gain)[:, N_META:]


def setup_inputs(seed: int = 0) -> dict:
    key = jax.random.key(seed)
    ks = jax.random.split(key, 24)
    f32 = jnp.float32
    nrm = lambda k, shape, scale: jax.random.normal(k, shape, f32) * scale
    dt0 = jnp.exp(jax.random.uniform(ks[10], (DEPTH, 2, SSM_HEADS), f32, np.log(1e-3), np.log(1e-1)))
    return {
        'x_prompt': nrm(ks[0], (BATCH, SEQ, D_MODEL), 1.0),
        'x_sample': nrm(ks[1], (DEC_BATCH, DEC_SEQ, D_MODEL), 1.0),
        'meta_tokens': nrm(ks[2], (N_META, D_MODEL), 1.0),
        'norm1_gain': 1.0 + nrm(ks[3], (DEPTH, D_MODEL), 0.1),
        'w_in': nrm(ks[4], (DEPTH, D_MODEL, D_IN_PROJ), D_MODEL ** -0.5),
        'rel_bias': nrm(ks[5], (DEPTH, ATT_HEADS, 2 * WIN_R - 1, 2 * WIN_C - 1), 0.1),
        'meta_bias': nrm(ks[6], (DEPTH, ATT_HEADS, N_META), 0.1),
        'conv_w': nrm(ks[7], (DEPTH, SSM_CONV, D_SSM_CONV), SSM_CONV ** -0.5),
        'conv_b': nrm(ks[8], (DEPTH, D_SSM_CONV), 0.01),
        'a_log': jnp.log(jax.random.uniform(ks[9], (DEPTH, 2, SSM_HEADS), f32, 1.0, 16.0)),
        'dt_bias': dt0 + jnp.log(-jnp.expm1(-dt0)),
        'd_skip': 1.0 + nrm(ks[11], (DEPTH, SSM_HEADS), 0.1),
        'ssd_norm_gain': 1.0 + nrm(ks[12], (DEPTH, D_SSM), 0.1),
        'w_branch_a': nrm(ks[13], (DEPTH, D_ATT, D_MODEL), D_ATT ** -0.5),
        'w_branch_f': nrm(ks[14], (DEPTH, D_FNET, D_MODEL), D_FNET ** -0.5),
        'w_branch_s': nrm(ks[15], (DEPTH, D_SSM, D_MODEL), D_SSM ** -0.5),
        'w_out': nrm(ks[16], (DEPTH, D_MODEL, D_MODEL), D_MODEL ** -0.5),
        'norm2_gain': 1.0 + nrm(ks[17], (DEPTH, D_MODEL), 0.1),
        'w_router': nrm(ks[18], (DEPTH, D_MODEL, N_EXPERTS), D_MODEL ** -0.5),
        'w_exp_gate': nrm(ks[19], (DEPTH, N_EXPERTS, D_MODEL, D_EXPERT), D_MODEL ** -0.5),
        'w_exp_up': nrm(ks[20], (DEPTH, N_EXPERTS, D_MODEL, D_EXPERT), D_MODEL ** -0.5),
        'w_exp_down': nrm(ks[21], (DEPTH, N_EXPERTS, D_EXPERT, D_MODEL), D_EXPERT ** -0.5),
        'final_gain': 1.0 + nrm(ks[22], (D_MODEL,), 0.1),
    }


def reference(x_prompt, x_sample, meta_tokens, norm1_gain, w_in, rel_bias, meta_bias, conv_w, conv_b,
              a_log, dt_bias, d_skip, ssd_norm_gain, w_branch_a, w_branch_f, w_branch_s, w_out,
              norm2_gain, w_router, w_exp_gate, w_exp_up, w_exp_down, final_gain):
    layer_weights = (norm1_gain, w_in, rel_bias, meta_bias, conv_w, conv_b, a_log, dt_bias, d_skip,
                     ssd_norm_gain, w_branch_a, w_branch_f, w_branch_s, w_out, norm2_gain, w_router,
                     w_exp_gate, w_exp_up, w_exp_down)
    y_prompt = encode(x_prompt, meta_tokens, final_gain, layer_weights)
    y_sample = encode(x_sample, meta_tokens, final_gain, layer_weights)
    return (y_prompt, y_sample)
```

```python
import functools

import numpy as np
import jax
import jax.numpy as jnp
from jax import lax
from jax.experimental import pallas as pl
from jax.experimental.pallas import tpu as pltpu

f32 = jnp.float32
bf16 = jnp.bfloat16
i32 = jnp.int32
HI = lax.Precision.HIGHEST

D_MODEL = 2048
N_META = 16
GRID_W = 64
ATT_HEADS = 16
ATT_HEAD_DIM = 64
D_ATT = 1024
WIN_R = 8
WIN_C = 16
FNET_GROUPS = 4
D_FNET = 1024
FNET_GROUP_DIM = 256
SSM_HEADS = 16
D_SSM = 1024
SSM_GROUPS = 4
SSM_STATE = 128
SSM_CHUNK = 128
D_SSM_CONV = 2048
N_EXPERTS = 16
EC_CAPACITY = 2
EPS = 1e-6

COL_GATE = 0
COL_QKV = 6144
COL_F = 9216
COL_XBC = 10240
COL_Z = 12288
COL_DT = 13312
N_PROJ = 13440

NEG = -1e30
VMEM_LIMIT = 56 * 1024 * 1024

TOK_BLK = 256
ATT_RB = 8
ATT_KR = 16


def _cdiv(a, b):
    return -(-a // b)


def _params(*sem):
    return pltpu.CompilerParams(dimension_semantics=sem, vmem_limit_bytes=VMEM_LIMIT)


def _sigmoid(x):
    return 1.0 / (1.0 + jnp.exp(-x))


def _softplus(x):
    return jnp.maximum(x, 0.0) + jnp.log(1.0 + jnp.exp(-jnp.abs(x)))


def _norm_matmul_body(x_ref, g_ref, w_ref, o_ref, h_ref):
    @pl.when(pl.program_id(1) == 0)
    def _():
        x = x_ref[...]
        ms = jnp.mean(x * x, axis=-1, keepdims=True)
        h_ref[...] = (x * lax.rsqrt(ms + EPS) * g_ref[...]).astype(bf16)

    o_ref[...] = jnp.dot(h_ref[...], w_ref[...], preferred_element_type=f32)


def norm_matmul(x2d, gain, w, tm=512, tn=1920):
    m, d = x2d.shape
    n = w.shape[1]
    assert n % tn == 0
    return pl.pallas_call(
        _norm_matmul_body,
        grid=(_cdiv(m, tm), n // tn),
        in_specs=[pl.BlockSpec((tm, d), lambda i, j: (i, 0)),
                  pl.BlockSpec((1, d), lambda i, j: (0, 0)),
                  pl.BlockSpec((d, tn), lambda i, j: (0, j))],
        out_specs=pl.BlockSpec((tm, tn), lambda i, j: (i, j)),
        out_shape=jax.ShapeDtypeStruct((m, n), f32),
        scratch_shapes=[pltpu.VMEM((tm, d), bf16)],
        compiler_params=_params("parallel", "arbitrary"),
        name="norm_matmul",
    )(x2d, gain.reshape(1, d), w)


def attention_bias_tables(rel_bias, rows):
    rq = np.arange(ATT_RB)[:, None, None, None]
    c = np.arange(GRID_W)[None, :, None, None]
    kr = np.arange(ATT_KR)[None, None, :, None]
    kc = np.arange(GRID_W)[None, None, None, :]
    dri, dci, val = [], [], []
    for r0, k0 in ((0, 0), (ATT_RB, ATT_RB - WIN_R // 2), (rows - ATT_RB, rows - ATT_KR)):
        r = r0 + rq
        ka = k0 + kr
        rs = np.clip(r - WIN_R // 2, 0, rows - WIN_R)
        vr = (ka >= rs) & (ka < rs + WIN_R)
        dr = ka - r + (WIN_R - 1)
        cs = np.clip(c - WIN_C // 2, 0, GRID_W - WIN_C)
        vc = (kc >= cs) & (kc < cs + WIN_C)
        dc = kc - c + (WIN_C - 1)
        shape = (ATT_RB, GRID_W, ATT_KR, GRID_W)
        v = np.broadcast_to(vr & vc, shape).reshape(ATT_RB * GRID_W, ATT_KR * GRID_W)
        dri.append(np.broadcast_to(np.clip(dr, 0, 2 * WIN_R - 2), shape).reshape(v.shape))
        dci.append(np.broadcast_to(np.clip(dc, 0, 2 * WIN_C - 2), shape).reshape(v.shape))
        val.append(v)
    dri, dci, val = np.stack(dri), np.stack(dci), np.stack(val)
    bias = rel_bias.astype(f32)[:, dri, dci]
    bias = jnp.where(val[None], bias, NEG)
    return jnp.transpose(bias, (1, 0, 2, 3))


def _attn_body(q_ref, k0, k1, k2, k3, v0, v1, v2, v3, km_ref, vm_ref, bias_ref, mb_ref, o_ref, *, nblk):
    i = pl.program_id(1)
    lane = lax.broadcasted_iota(i32, (1, 128), 1)
    q = q_ref[...] * (ATT_HEAD_DIM ** -0.5)
    km = km_ref[...].astype(bf16)
    vm = vm_ref[...]

    @pl.when(i < nblk)
    def _():
        k = jnp.concatenate([k0[...], k1[...], k2[...], k3[...]], axis=0).astype(bf16)
        v = jnp.concatenate([v0[...], v1[...], v2[...], v3[...]], axis=0)
        acc = jnp.zeros(q.shape, f32)
        for h in range(2):
            hm = (lane // ATT_HEAD_DIM) == h
            qh = jnp.where(hm, q, 0.0).astype(bf16)
            s = lax.dot_general(qh, k, (((1,), (1,)), ((), ())), preferred_element_type=f32) + bias_ref[h]
            sm = lax.dot_general(qh, km, (((1,), (1,)), ((), ())), preferred_element_type=f32) + mb_ref[h:h + 1, :]
            mx = jnp.maximum(jnp.max(s, axis=1, keepdims=True), jnp.max(sm, axis=1, keepdims=True))
            p = jnp.exp(s - mx)
            pm = jnp.exp(sm - mx)
            den = jnp.sum(p, axis=1, keepdims=True) + jnp.sum(pm, axis=1, keepdims=True)
            vh = jnp.where(hm, v, 0.0).astype(bf16)
            vmh = jnp.where(hm, vm, 0.0).astype(bf16)
            o = (jnp.dot(p.astype(bf16), vh, preferred_element_type=f32)
                 + jnp.dot(pm.astype(bf16), vmh, preferred_element_type=f32))
            acc = acc + o / den
        o_ref[...] = acc

    @pl.when(i == nblk)
    def _():
        acc = jnp.zeros(q.shape, f32)
        for h in range(2):
            hm = (lane // ATT_HEAD_DIM) == h
            qh = jnp.where(hm, q, 0.0).astype(bf16)
            sm = lax.dot_general(qh, km, (((1,), (1,)), ((), ())), preferred_element_type=f32) + mb_ref[h:h + 1, :]
            mx = jnp.max(sm, axis=1, keepdims=True)
            pm = jnp.exp(sm - mx)
            den = jnp.sum(pm, axis=1, keepdims=True)
            vmh = jnp.where(hm, vm, 0.0).astype(bf16)
            acc = acc + jnp.dot(pm.astype(bf16), vmh, preferred_element_type=f32) / den
        o_ref[...] = acc


def neighbourhood_attention(proj3, bias_tab, meta_bias, rows):
    b, t, _ = proj3.shape
    g = rows * GRID_W
    assert rows % ATT_RB == 0 and rows >= ATT_KR + ATT_RB
    nblk = rows // ATT_RB
    tq = ATT_RB * GRID_W
    tk = tq // 2
    qc, kc, vc = COL_QKV // 128, (COL_QKV + D_ATT) // 128, (COL_QKV + 2 * D_ATT) // 128
    nkb = g // tk

    def kstart(i):
        return jnp.clip(2 * i - 1, 0, nkb - 4)

    def kspec(j, col):
        return pl.BlockSpec((None, tk, 128), lambda bi, i, p: (bi, kstart(i) + j, col + p))

    def variant(i):
        return jnp.where(i == 0, 0, jnp.where(i >= nblk - 1, 2, 1))

    in_specs = ([pl.BlockSpec((None, tq, 128), lambda bi, i, p: (bi, i, qc + p))]
                + [kspec(j, kc) for j in range(4)]
                + [kspec(j, vc) for j in range(4)]
                + [pl.BlockSpec((None, N_META, 128), lambda bi, i, p: (bi, g // N_META, kc + p)),
                   pl.BlockSpec((None, N_META, 128), lambda bi, i, p: (bi, g // N_META, vc + p)),
                   pl.BlockSpec((None, 2, tq, ATT_KR * GRID_W), lambda bi, i, p: (variant(i), p, 0, 0)),
                   pl.BlockSpec((None, 2, N_META), lambda bi, i, p: (p, 0, 0))])
    return pl.pallas_call(
        functools.partial(_attn_body, nblk=nblk),
        grid=(b, nblk + 1, ATT_HEADS // 2),
        in_specs=in_specs,
        out_specs=pl.BlockSpec((None, tq, 128), lambda bi, i, p: (bi, i, p)),
        out_shape=jax.ShapeDtypeStruct((b, t, D_ATT), f32),
        compiler_params=_params("parallel", "arbitrary", "arbitrary"),
        name="nbr_attention",
    )(proj3, *([proj3] * 10), bias_tab, meta_bias.astype(f32).reshape(ATT_HEADS // 2, 2, N_META))


def _fnet_factors(t):
    best = None
    for n1 in range(8, t, 8):
        if t % n1 == 0 and (best is None or n1 + t // n1 < best[0] + best[1]):
            best = (n1, t // n1)
    assert best is not None
    return best


def _fnet_tables(t):
    n1, n2 = _fnet_factors(t)
    n2p = _cdiv(n2, 8) * 8
    k2 = np.arange(n2, dtype=np.float64)
    ang2 = 2 * np.pi * np.outer(k2, k2) / n2
    fa = np.zeros((2 * n2p, n2), np.float64)
    fa[:n2] = np.cos(ang2)
    fa[n2p:n2p + n2] = -np.sin(ang2)
    t1 = np.arange(n1, dtype=np.float64)
    angt = 2 * np.pi * np.outer(t1, k2) / t
    tc = np.cos(angt)[:, :, None]
    ts = np.sin(angt)[:, :, None]
    ang1 = 2 * np.pi * np.outer(t1, t1) / n1
    c1, s1 = np.cos(ang1), np.sin(ang1)
    fc = np.block([[c1, s1], [-s1, c1]])
    ch = np.arange(FNET_GROUP_DIM, dtype=np.float64)
    angc = 2 * np.pi * np.outer(ch, ch) / FNET_GROUP_DIM
    scale = 1.0 / np.sqrt(t * FNET_GROUP_DIM)
    cc, sc = np.cos(angc) * scale, np.sin(angc) * scale
    to = lambda a: jnp.asarray(a, f32)
    return n1, n2, n2p, to(fa), to(tc), to(ts), to(fc), to(cc), to(sc)


def _fnet_a_body(u_ref, fa_ref, tc_ref, ts_ref, o_ref, *, n2, n2p):
    fa = fa_ref[...]
    for j in range(8):
        r = jnp.dot(fa, u_ref[:, j, :], preferred_element_type=f32, precision=HI)
        re, im = r[:n2], r[n2p:n2p + n2]
        tc, ts = tc_ref[j], ts_ref[j]
        o_ref[0, j] = re * tc + im * ts
        o_ref[1, j] = im * tc - re * ts


def _fnet_c_body(p_ref, fc_ref, o_ref):
    o_ref[...] = jnp.dot(fc_ref[...], p_ref[...], preferred_element_type=f32, precision=HI)


def _fnet_d_body(q_ref, cc_ref, sc_ref, o_ref):
    cc, sc = cc_ref[...], sc_ref[...]
    for g in range(FNET_GROUPS):
        sl = slice(g * FNET_GROUP_DIM, (g + 1) * FNET_GROUP_DIM)
        o_ref[:, sl] = (jnp.dot(q_ref[0, :, sl], cc, preferred_element_type=f32, precision=HI)
                        + jnp.dot(q_ref[1, :, sl], sc, preferred_element_type=f32, precision=HI))


def fourier_mix(u):
    b, t, d = u.shape
    n1, n2, n2p, fa, tc, ts, fc, cc, sc = _fnet_tables(t)
    cb = 512
    p = pl.pallas_call(
        functools.partial(_fnet_a_body, n2=n2, n2p=n2p),
        grid=(b, n1 // 8, d // cb),
        in_specs=[pl.BlockSpec((None, n2, 8, cb), lambda bi, i, c: (bi, 0, i, c)),
                  pl.BlockSpec((2 * n2p, n2), lambda bi, i, c: (0, 0)),
                  pl.BlockSpec((8, n2, 1), lambda bi, i, c: (i, 0, 0)),
                  pl.BlockSpec((8, n2, 1), lambda bi, i, c: (i, 0, 0))],
        out_specs=pl.BlockSpec((None, 2, 8, n2, cb), lambda bi, i, c: (bi, 0, i, 0, c)),
        out_shape=jax.ShapeDtypeStruct((b, 2, n1, n2, d), f32),
        compiler_params=_params("parallel", "parallel", "parallel"),
        name="fnet_stage_a",
    )(u.reshape(b, n2, n1, d), fa, tc, ts)
    cw = 2048
    ncol = n2 * d
    q = pl.pallas_call(
        _fnet_c_body,
        grid=(b, _cdiv(ncol, cw)),
        in_specs=[pl.BlockSpec((None, 2 * n1, cw), lambda bi, c: (bi, 0, c)),
                  pl.BlockSpec((2 * n1, 2 * n1), lambda bi, c: (0, 0))],
        out_specs=pl.BlockSpec((None, 2 * n1, cw), lambda bi, c: (bi, 0, c)),
        out_shape=jax.ShapeDtypeStruct((b, 2 * n1, ncol), f32),
        compiler_params=_params("parallel", "parallel"),
        name="fnet_stage_c",
    )(p.reshape(b, 2 * n1, ncol), fc)
    tm = 512
    return pl.pallas_call(
        _fnet_d_body,
        grid=(b, _cdiv(t, tm)),
        in_specs=[pl.BlockSpec((None, 2, tm, d), lambda bi, i: (bi, 0, i, 0)),
                  pl.BlockSpec((FNET_GROUP_DIM, FNET_GROUP_DIM), lambda bi, i: (0, 0)),
                  pl.BlockSpec((FNET_GROUP_DIM, FNET_GROUP_DIM), lambda bi, i: (0, 0))],
        out_specs=pl.BlockSpec((None, tm, d), lambda bi, i: (bi, i, 0)),
        out_shape=jax.ShapeDtypeStruct((b, t, d), f32),
        compiler_params=_params("parallel", "parallel"),
        name="fnet_stage_d",
    )(q.reshape(b, 2, t, d), cc, sc)


def _ssd_body(*refs, d, nchunks):
    if d == 0:
        (xbc_ref, prev_ref, next_ref, dt_ref, cw_ref, cb_ref, alog_ref, dtb_ref,
         z_ref, yr_ref, dsk_ref, ng_ref, o_ref, st_ref, y_ref) = refs
    else:
        (xbc_ref, prev_ref, next_ref, dt_ref, cw_ref, cb_ref, alog_ref, dtb_ref,
         o_ref, st_ref) = refs
    L = SSM_CHUNK
    j = pl.program_id(1)
    if d == 0:
        ci = jnp.where(j == 0, nchunks, j - 1)
    else:
        ci = jnp.where(j == nchunks, nchunks, nchunks - 1 - j)
    is_meta = ci == nchunks

    @pl.when(j == 0)
    def _():
        st_ref[...] = jnp.zeros(st_ref.shape, f32)

    row = lax.broadcasted_iota(i32, (L, 1), 0)
    nvalid = jnp.where(is_meta, N_META, L)
    valid = row < nvalid

    x = jnp.where(valid, xbc_ref[...], 0.0)
    prev = jnp.where(is_meta, 0.0, prev_ref[7:8, :])
    nxt = jnp.where(ci == nchunks - 1, 0.0, next_ref[0:1, :])
    xp = jnp.where(row == 0, prev, pltpu.roll(x, 1, axis=0))
    xn = jnp.where(row == nvalid - 1, nxt, pltpu.roll(x, L - 1, axis=0))
    w = cw_ref[...]
    pre = w[0:1] * xp + w[1:2] * x + w[2:3] * xn + cb_ref[...]
    xc = jnp.where(valid, pre * _sigmoid(pre), 0.0)
    xs = xc[:, :D_SSM]
    bm = xc[:, D_SSM:D_SSM + SSM_GROUPS * SSM_STATE]
    cm = xc[:, D_SSM + SSM_GROUPS * SSM_STATE:]

    dtf = jnp.where(valid, _softplus(dt_ref[...] + dtb_ref[...]), 0.0)
    da = dtf * (-jnp.exp(alog_ref[...]))
    li = lax.broadcasted_iota(i32, (L, L), 0)
    si = lax.broadcasted_iota(i32, (L, L), 1)
    causal = (si <= li) if d == 0 else (si >= li)
    ac = jnp.dot(causal.astype(f32), da, preferred_element_type=f32, precision=HI)
    act = ac.T
    dtt = dtf.T
    tot = jnp.sum(da, axis=0, keepdims=True)

    lane = lax.broadcasted_iota(i32, (1, 128), 1)
    cbs = {}
    for p in range(SSM_HEADS // 2):
        g = (2 * p) // (SSM_HEADS // SSM_GROUPS)
        bg = bm[:, g * SSM_STATE:(g + 1) * SSM_STATE]
        cg = cm[:, g * SSM_STATE:(g + 1) * SSM_STATE]
        if g not in cbs:
            cbs[g] = lax.dot_general(cg.astype(bf16), bg.astype(bf16), (((1,), (1,)), ((), ())),
                                     preferred_element_type=f32)
        cbg = cbs[g]
        xs_p = xs[:, p * 128:(p + 1) * 128]
        st = st_ref[p]
        y_p = jnp.zeros((L, 128), f32)
        new_st = jnp.zeros((SSM_STATE, 128), f32)
        dec_row = jnp.zeros((1, 128), f32)
        for hh in range(2):
            col = d * SSM_HEADS + 2 * p + hh
            ac_c, ac_r = ac[:, col:col + 1], act[col:col + 1, :]
            dt_c, dt_r = dtf[:, col:col + 1], dtt[col:col + 1, :]
            tot_h = tot[:, col:col + 1]
            hm = (lane // 64) == hh
            m = cbg * jnp.exp(jnp.where(causal, ac_c - ac_r, NEG)) * dt_r
            xm = jnp.where(hm, xs_p, 0.0).astype(bf16)
            stm = jnp.where(hm, st, 0.0).astype(bf16)
            cwt = cg * jnp.exp(ac_c)
            lhs = jnp.concatenate([m, cwt], axis=1).astype(bf16)
            rhs = jnp.concatenate([xm, stm], axis=0)
            y_p = y_p + jnp.dot(lhs, rhs, preferred_element_type=f32)
            bw = (bg * (jnp.exp(tot_h - ac_c) * dt_c)).astype(bf16)
            new_st = new_st + lax.dot_general(bw, xm, (((0,), (0,)), ((), ())), preferred_element_type=f32)
            dec_row = jnp.where(hm, jnp.exp(tot_h), dec_row)
        st_ref[p] = st * dec_row + new_st
        if d == 0:
            y_ref[:, p * 128:(p + 1) * 128] = y_p
        else:
            o_ref[:, p * 128:(p + 1) * 128] = y_p

    if d == 0:
        y = y_ref[...] + yr_ref[...] + xs * dsk_ref[...]
        z = z_ref[...]
        y = y * (z * _sigmoid(z))
        ms = jnp.mean(y * y, axis=-1, keepdims=True)
        o_ref[...] = y * lax.rsqrt(ms + EPS) * ng_ref[...]


def ssd_mixer(proj3, rows, conv_w, conv_b, a_log, dt_bias, d_skip, norm_gain):
    b, t, _ = proj3.shape
    g = rows * GRID_W
    L = SSM_CHUNK
    assert g % L == 0
    nch = g // L
    pad128 = lambda v: jnp.pad(v.astype(f32).reshape(1, -1), ((0, 0), (0, 128 - 2 * SSM_HEADS)))
    alog, dtb = pad128(a_log), pad128(dt_bias)
    cw = conv_w.astype(f32)
    cb = conv_b.astype(f32).reshape(1, -1)
    dsk = jnp.repeat(d_skip.astype(f32), D_SSM // SSM_HEADS).reshape(1, -1)
    ng = norm_gain.astype(f32).reshape(1, -1)
    xcol, zcol, dcol = COL_XBC // D_SSM_CONV, COL_Z // D_SSM, COL_DT // 128
    hb = L // 8

    def run(d, extra_in, extra_specs, scratch):
        if d == 0:
            cidx = lambda j: jnp.where(j == 0, nch, j - 1)
        else:
            cidx = lambda j: jnp.where(j == nch, nch, nch - 1 - j)

        def pidx(j):
            ci = cidx(j)
            return jnp.where(ci == nch, 0, jnp.where(ci == 0, hb * nch + 1, hb * ci - 1))

        def nidx(j):
            ci = cidx(j)
            return jnp.where(ci >= nch - 1, 0, hb * (ci + 1))

        const = lambda shape: pl.BlockSpec(shape, lambda bi, j: (0,) * len(shape))
        in_specs = [pl.BlockSpec((None, L, D_SSM_CONV), lambda bi, j: (bi, cidx(j), xcol)),
                    pl.BlockSpec((None, 8, D_SSM_CONV), lambda bi, j: (bi, pidx(j), xcol)),
                    pl.BlockSpec((None, 8, D_SSM_CONV), lambda bi, j: (bi, nidx(j), xcol)),
                    pl.BlockSpec((None, L, 128), lambda bi, j: (bi, cidx(j), dcol)),
                    const((3, D_SSM_CONV)), const((1, D_SSM_CONV)), const((1, 128)), const((1, 128))]
        in_specs += extra_specs(cidx, const)
        return pl.pallas_call(
            functools.partial(_ssd_body, d=d, nchunks=nch),
            grid=(b, nch + 1),
            in_specs=in_specs,
            out_specs=pl.BlockSpec((None, L, D_SSM), lambda bi, j: (bi, cidx(j), 0)),
            out_shape=jax.ShapeDtypeStruct((b, t, D_SSM), f32),
            scratch_shapes=[pltpu.VMEM((SSM_HEADS // 2, SSM_STATE, 128), f32)] + scratch,
            compiler_params=_params("parallel", "arbitrary"),
            name="ssd_fwd" if d == 0 else "ssd_rev",
        )(proj3, proj3, proj3, proj3, cw, cb, alog, dtb, *extra_in)

    y_rev = run(1, (), lambda cidx, const: [], [])
    return run(
        0, (proj3, y_rev, dsk, ng),
        lambda cidx, const: [pl.BlockSpec((None, L, D_SSM), lambda bi, j: (bi, cidx(j), zcol)),
                             pl.BlockSpec((None, L, D_SSM), lambda bi, j: (bi, cidx(j), 0)),
                             const((1, D_SSM)), const((1, D_SSM))],
        [pltpu.VMEM((L, D_SSM), f32)])


def _merge_body(ya, yf, ys, g0, g1, g2, s_ref, wa, wf, ws, wo, o_ref):
    def branch(y, g, w):
        return _sigmoid(g[...]) * jnp.dot(y[...].astype(bf16), w[...], preferred_element_type=f32)

    merged = branch(ya, g0, wa) + branch(yf, g1, wf) + branch(ys, g2, ws)
    o_ref[...] = s_ref[...] + jnp.dot(merged.astype(bf16), wo[...], preferred_element_type=f32)


def merge_branches(s2d, proj2d, ya, yf, ys, wa, wf, ws, wo, tm=256):
    m, d = s2d.shape
    row = lambda w: pl.BlockSpec((tm, w), lambda i: (i, 0))
    gate = lambda k: pl.BlockSpec((tm, d), lambda i: (i, COL_GATE // d + k))
    wspec = lambda r: pl.BlockSpec((r, d), lambda i: (0, 0), pipeline_mode=pl.Buffered(1))
    return pl.pallas_call(
        _merge_body,
        grid=(_cdiv(m, tm),),
        in_specs=[row(D_ATT), row(D_FNET), row(D_SSM), gate(0), gate(1), gate(2), row(d),
                  wspec(D_ATT), wspec(D_FNET), wspec(D_SSM), wspec(d)],
        out_specs=row(d),
        out_shape=jax.ShapeDtypeStruct((m, d), f32),
        compiler_params=_params("parallel"),
        name="merge_branches",
    )(ya, yf, ys, proj2d, proj2d, proj2d, s2d, wa, wf, ws, wo)


def _router_body(s_ref, g_ref, wrt_ref, wr_ref, tok_ref, afft_ref, aff_ref, *, m_total, tm):
    i = pl.program_id(0)
    x = s_ref[...]
    ms = jnp.mean(x * x, axis=-1, keepdims=True)
    tok = x * lax.rsqrt(ms + EPS) * g_ref[...]
    row = i * tm + lax.broadcasted_iota(i32, (tm, 1), 0)
    tok = jnp.where(row < m_total, tok, 0.0)
    tok_ref[...] = tok.astype(bf16)
    lt = lax.dot_general(wrt_ref[...], tok, (((1,), (1,)), ((), ())), preferred_element_type=f32, precision=HI)
    et = jnp.exp(lt - jnp.max(lt, axis=0, keepdims=True))
    afft_ref[...] = et / jnp.sum(et, axis=0, keepdims=True)
    lg = jnp.dot(tok, wr_ref[...], preferred_element_type=f32, precision=HI)
    ex = jnp.exp(lg - jnp.max(lg, axis=1, keepdims=True))
    aff_ref[...] = ex / jnp.sum(ex, axis=1, keepdims=True)


def router(s2d, gain, w_router, tm=512):
    m, d = s2d.shape
    mp = _cdiv(m, tm) * tm
    wr = w_router.astype(f32)
    return pl.pallas_call(
        functools.partial(_router_body, m_total=m, tm=tm),
        grid=(mp // tm,),
        in_specs=[pl.BlockSpec((tm, d), lambda i: (i, 0)),
                  pl.BlockSpec((1, d), lambda i: (0, 0)),
                  pl.BlockSpec((N_EXPERTS, d), lambda i: (0, 0)),
                  pl.BlockSpec((d, N_EXPERTS), lambda i: (0, 0))],
        out_specs=[pl.BlockSpec((tm, d), lambda i: (i, 0)),
                   pl.BlockSpec((N_EXPERTS, tm), lambda i: (0, i)),
                   pl.BlockSpec((tm, N_EXPERTS), lambda i: (i, 0))],
        out_shape=[jax.ShapeDtypeStruct((mp, d), bf16),
                   jax.ShapeDtypeStruct((N_EXPERTS, mp), f32),
                   jax.ShapeDtypeStruct((mp, N_EXPERTS), f32)],
        compiler_params=_params("parallel"),
        name="moe_router",
    )(s2d, gain.astype(f32).reshape(1, d), wr.T, wr)


def _exclusive_rank(x, nt):
    li = lax.broadcasted_iota(i32, (128, 128), 0)
    lj = lax.broadcasted_iota(i32, (128, 128), 1)
    lane_before = (li < lj).astype(bf16)
    ti = lax.broadcasted_iota(i32, (nt, nt), 0)
    tj = lax.broadcasted_iota(i32, (nt, nt), 1)
    row_before = (tj < ti).astype(bf16)
    xb = x.astype(bf16)
    within = jnp.dot(xb, lane_before, preferred_element_type=f32)
    before = jnp.sum(jnp.dot(row_before, xb, preferred_element_type=f32), axis=1, keepdims=True)
    return within + before


def _count(mask):
    return jnp.sum(jnp.sum(mask.astype(f32), axis=-1, keepdims=True), axis=-2, keepdims=True)


def _select_body(a_ref, sel_ref, *, cap, nt):
    bits = pltpu.bitcast(a_ref[...], i32)

    def step(i, pref):
        cand = pref | jnp.left_shift(jnp.int32(1), 30 - i)
        return jnp.where(_count(bits >= cand) >= cap, cand, pref)

    thr = lax.fori_loop(0, 31, step, jnp.zeros((N_EXPERTS, 1, 1), i32))
    for e in range(N_EXPERTS):
        be = bits[e]
        gt = be > thr[e]
        eq = be == thr[e]
        need = cap - _count(gt)
        take = jnp.logical_and(eq, _exclusive_rank(eq.astype(f32), nt) < need)
        sel_ref[e] = jnp.logical_or(gt, take).astype(f32)


def _rank_body(sel_ref, slot_ref, *, nt):
    for e in range(N_EXPERTS):
        sel = sel_ref[e]
        slot_ref[e] = jnp.where(sel > 0.5, _exclusive_rank(sel, nt), -1.0).astype(i32)


def select_tokens(aff3, cap):
    e, nt, _ = aff3.shape
    return pl.pallas_call(
        functools.partial(_select_body, cap=cap, nt=nt),
        out_shape=jax.ShapeDtypeStruct(aff3.shape, f32),
        compiler_params=pltpu.CompilerParams(vmem_limit_bytes=VMEM_LIMIT),
        name="moe_select",
    )(aff3)


def rank_tokens(sel3):
    e, nt, _ = sel3.shape
    return pl.pallas_call(
        functools.partial(_rank_body, nt=nt),
        out_shape=jax.ShapeDtypeStruct(sel3.shape, i32),
        compiler_params=pltpu.CompilerParams(vmem_limit_bytes=VMEM_LIMIT),
        name="moe_rank",
    )(sel3)


def _gather_body(i_ref, j_ref, f_ref, v_ref, tok_ref, slot_ref, o_ref, acc_ref, *, ns):
    k = pl.program_id(0) * ns + pl.program_id(1)

    @pl.when(f_ref[k] == 1)
    def _():
        acc_ref[...] = jnp.zeros(acc_ref.shape, f32)

    @pl.when(v_ref[k] == 1)
    def _():
        want = lax.broadcasted_iota(i32, (TOK_BLK, TOK_BLK), 0) + j_ref[k] * TOK_BLK
        onehot = jnp.where(slot_ref[...] == want, 1.0, 0.0).astype(bf16)
        acc_ref[...] += jnp.dot(onehot, tok_ref[...], preferred_element_type=f32)

    o_ref[...] = acc_ref[...].astype(bf16)


def gather_tokens(tok, slot_row3, sched, cap_pad, ns):
    d = tok.shape[1]
    ii, jj, ff, vv = sched
    return pl.pallas_call(
        functools.partial(_gather_body, ns=ns),
        grid_spec=pltpu.PrefetchScalarGridSpec(
            num_scalar_prefetch=4,
            grid=(N_EXPERTS, ns),
            in_specs=[pl.BlockSpec((TOK_BLK, d), lambda e, s, ii, jj, ff, vv: (ii[e * ns + s], 0)),
                      pl.BlockSpec((None, 1, TOK_BLK), lambda e, s, ii, jj, ff, vv: (e, 0, ii[e * ns + s]))],
            out_specs=pl.BlockSpec((None, TOK_BLK, d), lambda e, s, ii, jj, ff, vv: (e, jj[e * ns + s], 0)),
            scratch_shapes=[pltpu.VMEM((TOK_BLK, d), f32)]),
        out_shape=jax.ShapeDtypeStruct((N_EXPERTS, cap_pad, d), bf16),
        compiler_params=_params("parallel", "arbitrary"),
        name="moe_gather",
    )(ii, jj, ff, vv, tok, slot_row3)


def _ffn_body(x_ref, wg_ref, wu_ref, wd_ref, o_ref):
    @pl.when(pl.program_id(2) == 0)
    def _():
        o_ref[...] = jnp.zeros(o_ref.shape, f32)

    x = x_ref[...]
    gt = jnp.dot(x, wg_ref[...].astype(bf16), preferred_element_type=f32)
    up = jnp.dot(x, wu_ref[...].astype(bf16), preferred_element_type=f32)
    hid = (gt * _sigmoid(gt) * up).astype(bf16)
    o_ref[...] += jnp.dot(hid, wd_ref[...].astype(bf16), preferred_element_type=f32)


def expert_ffn(xe, w_gate, w_up, w_down, layer, tf=256):
    e, cap_pad, d = xe.shape
    dff = w_gate.shape[-1]
    nm = 1
    while cap_pad // nm > 1280 or cap_pad % nm or (cap_pad // nm) % 8:
        nm += 1
    tm = cap_pad // nm
    return pl.pallas_call(
        _ffn_body,
        grid=(e, nm, dff // tf),
        in_specs=[pl.BlockSpec((None, tm, d), lambda ei, mi, fi: (ei, mi, 0)),
                  pl.BlockSpec((None, None, d, tf), lambda ei, mi, fi: (layer, ei, 0, fi)),
                  pl.BlockSpec((None, None, d, tf), lambda ei, mi, fi: (layer, ei, 0, fi)),
                  pl.BlockSpec((None, None, tf, d), lambda ei, mi, fi: (layer, ei, fi, 0))],
        out_specs=pl.BlockSpec((None, tm, d), lambda ei, mi, fi: (ei, mi, 0)),
        out_shape=jax.ShapeDtypeStruct((e, cap_pad, d), f32),
        compiler_params=_params("parallel", "parallel", "arbitrary"),
        name="moe_expert_ffn",
    )(xe, w_gate, w_up, w_down)


def _combine_body(i_ref, e_ref, j_ref, f_ref, v_ref, s_ref, ye_ref, slot_ref, aff_ref, o_ref):
    k = pl.program_id(0)

    @pl.when(f_ref[k] == 1)
    def _():
        o_ref[...] = s_ref[...]

    @pl.when(v_ref[k] == 1)
    def _():
        lane = lax.broadcasted_iota(i32, (1, N_EXPERTS), 1)
        pick = lane == e_ref[k]
        slot_col = jnp.sum(jnp.where(pick, slot_ref[...], 0.0), axis=1, keepdims=True)
        gate_col = jnp.sum(jnp.where(pick, aff_ref[...], 0.0), axis=1, keepdims=True)
        want = (lax.broadcasted_iota(i32, (1, TOK_BLK), 1) + j_ref[k] * TOK_BLK).astype(f32)
        onehot = jnp.where(slot_col == want, 1.0, 0.0).astype(bf16)
        ye = ye_ref[...]
        hi = ye.astype(bf16)
        lo = (ye - hi.astype(f32)).astype(bf16)
        picked = (jnp.dot(onehot, hi, preferred_element_type=f32)
                  + jnp.dot(onehot, lo, preferred_element_type=f32))
        o_ref[...] += gate_col * picked


def combine_tokens(s2d, ye, slot_tok, aff, sched):
    m, d = s2d.shape
    ii, ee, jj, ff, vv = sched
    nsteps = ii.shape[0]
    return pl.pallas_call(
        _combine_body,
        grid_spec=pltpu.PrefetchScalarGridSpec(
            num_scalar_prefetch=5,
            grid=(nsteps,),
            in_specs=[pl.BlockSpec((TOK_BLK, d), lambda k, ii, ee, jj, ff, vv: (ii[k], 0)),
                      pl.BlockSpec((None, TOK_BLK, d), lambda k, ii, ee, jj, ff, vv: (ee[k], jj[k], 0)),
                      pl.BlockSpec((TOK_BLK, N_EXPERTS), lambda k, ii, ee, jj, ff, vv: (ii[k], 0)),
                      pl.BlockSpec((TOK_BLK, N_EXPERTS), lambda k, ii, ee, jj, ff, vv: (ii[k], 0))],
            out_specs=pl.BlockSpec((TOK_BLK, d), lambda k, ii, ee, jj, ff, vv: (ii[k], 0))),
        out_shape=jax.ShapeDtypeStruct((m, d), f32),
        compiler_params=_params("arbitrary"),
        name="moe_combine",
    )(ii, ee, jj, ff, vv, s2d, ye, slot_tok, aff)


def _routing_schedules(slot, n_tok_blk, n_slot_blk):
    e = slot.shape[0]
    cnt = jnp.sum((slot[:, :n_tok_blk * TOK_BLK] >= 0).reshape(e, n_tok_blk, TOK_BLK), axis=-1).astype(i32)
    cum_in = jnp.cumsum(cnt, axis=1)
    cum_ex = cum_in - cnt
    jlo = cum_ex // TOK_BLK
    jhi = (cum_in - 1) // TOK_BLK
    npairs = jnp.where(cnt > 0, jhi - jlo + 1, 0)
    off_in = jnp.cumsum(npairs, axis=1)
    off_ex = off_in - npairs
    total = off_in[:, -1:]
    ns = n_tok_blk + n_slot_blk
    step = jnp.arange(ns, dtype=i32)[None, :]
    valid = step < total
    sc = jnp.minimum(step, total - 1)
    tok_blk = jnp.sum(off_in[:, None, :] <= sc[:, :, None], axis=-1).astype(i32)
    tok_blk = jnp.minimum(tok_blk, n_tok_blk - 1)
    slot_blk = (jnp.take_along_axis(jlo, tok_blk, axis=1)
                + sc - jnp.take_along_axis(off_ex, tok_blk, axis=1)).astype(i32)
    first = jnp.concatenate([jnp.ones((e, 1), bool), slot_blk[:, 1:] != slot_blk[:, :-1]], axis=1)
    gather = tuple(a.reshape(-1).astype(i32) for a in (tok_blk, slot_blk, first, valid))

    eid = jnp.broadcast_to(jnp.arange(e, dtype=i32)[:, None], (e, ns))
    big = jnp.int32(2 ** 30)
    key_pairs = jnp.where(valid, (tok_blk * (e * (n_slot_blk + 1) + 1) + 1 + eid * (n_slot_blk + 1) + slot_blk), big)
    init_tok = jnp.arange(n_tok_blk, dtype=i32)
    key = jnp.concatenate([init_tok * (e * (n_slot_blk + 1) + 1), key_pairs.reshape(-1)])
    c_tok = jnp.concatenate([init_tok, tok_blk.reshape(-1)])
    c_exp = jnp.concatenate([jnp.zeros_like(init_tok), eid.reshape(-1)])
    c_slot = jnp.concatenate([jnp.zeros_like(init_tok), slot_blk.reshape(-1)])
    c_valid = jnp.concatenate([jnp.zeros_like(init_tok), valid.reshape(-1).astype(i32)])
    order = jnp.argsort(key)
    n_real = n_tok_blk + jnp.sum(valid)
    pos = jnp.minimum(jnp.arange(key.shape[0]), n_real - 1)
    order = order[pos]
    c_tok, c_exp, c_slot = c_tok[order], c_exp[order], c_slot[order]
    c_valid = jnp.where(jnp.arange(key.shape[0]) < n_real, c_valid[order], 0)
    c_first = jnp.concatenate([jnp.ones((1,), i32), (c_tok[1:] != c_tok[:-1]).astype(i32)])
    combine = tuple(a.astype(i32) for a in (c_tok, c_exp, c_slot, c_first, c_valid))
    return gather, combine, ns


def expert_choice_ffn(s3, gain, w_router, w_gate, w_up, w_down, layer):
    b, t, d = s3.shape
    m = b * t
    cap = (EC_CAPACITY * m) // N_EXPERTS
    s2d = s3.reshape(m, d)
    tok, afft, aff = router(s2d, gain, w_router)

    n_pad = _cdiv(m, 1024) * 1024
    nt = n_pad // 128
    a = jnp.roll(afft[:, :m].reshape(N_EXPERTS, b, t), N_META, axis=2).reshape(N_EXPERTS, m)
    a = jnp.pad(a, ((0, 0), (0, n_pad - m)), constant_values=-1.0)
    sel = select_tokens(a.reshape(N_EXPERTS, nt, 128), cap).reshape(N_EXPERTS, n_pad)
    sel = jnp.roll(sel[:, :m].reshape(N_EXPERTS, b, t), -N_META, axis=2).reshape(N_EXPERTS, m)
    sel = jnp.pad(sel, ((0, 0), (0, n_pad - m)))
    slot = rank_tokens(sel.reshape(N_EXPERTS, nt, 128)).reshape(N_EXPERTS, n_pad)

    n_tok_blk = _cdiv(m, TOK_BLK)
    n_slot_blk = _cdiv(cap, TOK_BLK)
    cap_pad = n_slot_blk * TOK_BLK
    g_sched, c_sched, ns = _routing_schedules(slot, n_tok_blk, n_slot_blk)
    xe = gather_tokens(tok, slot.reshape(N_EXPERTS, 1, n_pad), g_sched, cap_pad, ns)
    ye = expert_ffn(xe, w_gate, w_up, w_down, layer)
    out = combine_tokens(s2d, ye, slot.T.astype(f32), aff, c_sched)
    return out.reshape(b, t, d)


def _final_norm_body(x_ref, g_ref, o_ref):
    x = x_ref[...]
    ms = jnp.mean(x * x, axis=-1, keepdims=True)
    o_ref[...] = x * lax.rsqrt(ms + EPS) * g_ref[...]


def final_norm(s3, gain, g, tm=512):
    b, t, d = s3.shape
    assert g % tm == 0
    return pl.pallas_call(
        _final_norm_body,
        grid=(b, g // tm),
        in_specs=[pl.BlockSpec((None, tm, d), lambda bi, i: (bi, i, 0)),
                  pl.BlockSpec((1, d), lambda bi, i: (0, 0))],
        out_specs=pl.BlockSpec((None, tm, d), lambda bi, i: (bi, i, 0)),
        out_shape=jax.ShapeDtypeStruct((b, g, d), f32),
        compiler_params=_params("parallel", "parallel"),
        name="final_norm",
    )(s3, gain.astype(f32).reshape(1, d))


def _reorder_w_in(w):
    cuts = np.cumsum([3 * D_ATT, D_FNET, D_SSM_CONV, D_SSM, 2 * SSM_HEADS])
    qkv, u_f, xbc, z, dt, gate = jnp.split(w, [int(c) for c in cuts], axis=1)
    dt = jnp.pad(dt, ((0, 0), (0, 128 - 2 * SSM_HEADS)))
    return jnp.concatenate([gate, qkv, u_f, xbc, z, dt], axis=1).astype(bf16)


def encoder_layer(s3, rows, layer, lw):
    b, t, d = s3.shape
    m = b * t
    proj = norm_matmul(s3.reshape(m, d), lw["norm1_gain"], lw["w_in"])
    proj3 = proj.reshape(b, t, N_PROJ)
    y_att = neighbourhood_attention(proj3, lw["bias_tab"], lw["meta_bias"], rows)
    u_orig = jnp.roll(proj3[:, :, COL_F:COL_F + D_FNET], N_META, axis=1)
    y_f = jnp.roll(fourier_mix(u_orig), -N_META, axis=1)
    y_s = ssd_mixer(proj3, rows, lw["conv_w"], lw["conv_b"], lw["a_log"], lw["dt_bias"], lw["d_skip"],
                    lw["ssd_norm_gain"])
    s2d = merge_branches(s3.reshape(m, d), proj, y_att.reshape(m, -1), y_f.reshape(m, -1), y_s.reshape(m, -1),
                         lw["w_branch_a"], lw["w_branch_f"], lw["w_branch_s"], lw["w_out"])
    return expert_choice_ffn(s2d.reshape(b, t, d), lw["norm2_gain"], lw["w_router"],
                             lw["w_exp_gate"], lw["w_exp_up"], lw["w_exp_down"], layer)


def encode(x, meta_tokens, final_gain, layers):
    b, g, d = x.shape
    rows = g // GRID_W
    meta = jnp.broadcast_to(meta_tokens.astype(x.dtype)[None], (b, N_META, d))
    s = jnp.concatenate([x, meta], axis=1)
    for layer, lw in enumerate(layers):
        s = encoder_layer(s, rows, layer, lw)
    return final_norm(s, final_gain, g)


def kernel(x_prompt, x_sample, meta_tokens, norm1_gain, w_in, rel_bias, meta_bias, conv_w, conv_b, a_log, dt_bias,
           d_skip, ssd_norm_gain, w_branch_a, w_branch_f, w_branch_s, w_out, norm2_gain, w_router, w_exp_gate,
           w_exp_up, w_exp_down, final_gain):
    depth = w_in.shape[0]
    rows_set = {x_prompt.shape[1] // GRID_W, x_sample.shape[1] // GRID_W}
    layers = []
    for l in range(depth):
        rb = rel_bias[l]
        layers.append({
            "norm1_gain": norm1_gain[l].astype(f32), "w_in": _reorder_w_in(w_in[l]),
            "rel_bias": rb, "meta_bias": meta_bias[l],
            "conv_w": conv_w[l], "conv_b": conv_b[l], "a_log": a_log[l], "dt_bias": dt_bias[l],
            "d_skip": d_skip[l], "ssd_norm_gain": ssd_norm_gain[l],
            "w_branch_a": w_branch_a[l].astype(bf16), "w_branch_f": w_branch_f[l].astype(bf16),
            "w_branch_s": w_branch_s[l].astype(bf16), "w_out": w_out[l].astype(bf16),
            "norm2_gain": norm2_gain[l], "w_router": w_router[l],
            "w_exp_gate": w_exp_gate, "w_exp_up": w_exp_up, "w_exp_down": w_exp_down,
            "bias_tab": attention_bias_tables(rb, max(rows_set)),
        })
    y_prompt = encode(x_prompt, meta_tokens, final_gain, layers)
    y_sample = encode(x_sample, meta_tokens, final_gain, layers)
    return (y_prompt, y_sample)
```

```python
import functools

import numpy as np
import jax
import jax.numpy as jnp
from jax import lax
from jax.experimental import pallas as pl
from jax.experimental.pallas import tpu as pltpu

f32 = jnp.float32
bf16 = jnp.bfloat16
i32 = jnp.int32
HI = lax.Precision.HIGHEST

D_MODEL = 2048
N_META = 16
GRID_W = 64
ATT_HEADS = 16
ATT_HEAD_DIM = 64
D_ATT = 1024
WIN_R = 8
WIN_C = 16
FNET_GROUPS = 4
D_FNET = 1024
FNET_GROUP_DIM = 256
SSM_HEADS = 16
D_SSM = 1024
SSM_GROUPS = 4
SSM_STATE = 128
SSM_CHUNK = 128
D_SSM_CONV = 2048
N_EXPERTS = 16
EC_CAPACITY = 2
EPS = 1e-6

COL_GATE = 0
COL_QKV = 6144
COL_F = 9216
COL_XBC = 10240
COL_Z = 12288
COL_DT = 13312
N_PROJ = 13824

NEG = -1e30
VMEM_LIMIT = 56 * 1024 * 1024

SLOT_BLK = 256
GATHER_TOK = 1024
COMBINE_TOK = 512
ROUTE_PAD = 1024
FLAG_FIRST, FLAG_VALID, FLAG_LAST = 1, 2, 4
ATT_RB = 8
ATT_KR = 16


def _cdiv(a, b):
    return -(-a // b)


def _params(*sem):
    return pltpu.CompilerParams(dimension_semantics=sem, vmem_limit_bytes=VMEM_LIMIT)


def _sigmoid(x):
    return 1.0 / (1.0 + jnp.exp(-x))


def _softplus(x):
    return jnp.maximum(x, 0.0) + jnp.log(1.0 + jnp.exp(-jnp.abs(x)))


def _norm_matmul_body(x_ref, g_ref, w_ref, o_ref, h_ref):
    @pl.when(pl.program_id(1) == 0)
    def _():
        x = x_ref[...]
        ms = jnp.mean(x * x, axis=-1, keepdims=True)
        h_ref[...] = (x * lax.rsqrt(ms + EPS) * g_ref[...]).astype(bf16)

    o_ref[...] = jnp.dot(h_ref[...], w_ref[...], preferred_element_type=f32)


def norm_matmul(x2d, gain, w, tm=1024, tn=1536):
    m, d = x2d.shape
    n = w.shape[1]
    assert n % tn == 0
    return pl.pallas_call(
        _norm_matmul_body,
        grid=(_cdiv(m, tm), n // tn),
        in_specs=[pl.BlockSpec((tm, d), lambda i, j: (i, 0)),
                  pl.BlockSpec((1, d), lambda i, j: (0, 0)),
                  pl.BlockSpec((d, tn), lambda i, j: (0, j))],
        out_specs=pl.BlockSpec((tm, tn), lambda i, j: (i, j)),
        out_shape=jax.ShapeDtypeStruct((m, n), f32),
        scratch_shapes=[pltpu.VMEM((tm, d), bf16)],
        compiler_params=_params("parallel", "arbitrary"),
        name="norm_matmul",
    )(x2d, gain.reshape(1, d), w)


def attention_bias_tables(rel_bias, rows):
    rel = rel_bias.astype(f32)
    nh = rel.shape[0]
    cols = []
    for c in range(GRID_W):
        cs = min(max(c - WIN_C // 2, 0), GRID_W - WIN_C)
        j0 = cs - c + (WIN_C - 1)
        cols.append(jnp.pad(rel[:, :, j0:j0 + WIN_C], ((0, 0), (0, 0), (cs, GRID_W - WIN_C - cs)),
                            constant_values=NEG))
    colbias = jnp.transpose(jnp.stack(cols, axis=2), (0, 2, 1, 3))
    tabs = []
    for r0, k0 in ((0, 0), (ATT_RB, ATT_RB - WIN_R // 2), (rows - ATT_RB, rows - ATT_KR)):
        per_row = []
        for rq in range(ATT_RB):
            r = r0 + rq
            rs = min(max(r - WIN_R // 2, 0), rows - WIN_R)
            d0 = rs - r + (WIN_R - 1)
            kr0 = rs - k0
            per_row.append(jnp.pad(colbias[:, :, d0:d0 + WIN_R, :],
                                   ((0, 0), (0, 0), (kr0, ATT_KR - WIN_R - kr0), (0, 0)), constant_values=NEG))
        tabs.append(jnp.stack(per_row, axis=1).reshape(nh, ATT_RB * GRID_W, ATT_KR * GRID_W))
    return jnp.stack(tabs)


def _attn_body(q_ref, k0, k1, k2, k3, v0, v1, v2, v3, km_ref, vm_ref, bias_ref, mb_ref, o_ref, *, nblk):
    i = pl.program_id(1)
    lane = lax.broadcasted_iota(i32, (1, 128), 1)
    q = q_ref[...] * (ATT_HEAD_DIM ** -0.5)
    km = km_ref[...].astype(bf16)
    vm = vm_ref[...]

    @pl.when(i < nblk)
    def _():
        k = jnp.concatenate([k0[...], k1[...], k2[...], k3[...]], axis=0).astype(bf16)
        v = jnp.concatenate([v0[...], v1[...], v2[...], v3[...]], axis=0)
        acc = jnp.zeros(q.shape, f32)
        for h in range(2):
            hm = (lane // ATT_HEAD_DIM) == h
            qh = jnp.where(hm, q, 0.0).astype(bf16)
            s = lax.dot_general(qh, k, (((1,), (1,)), ((), ())), preferred_element_type=f32) + bias_ref[h]
            sm = lax.dot_general(qh, km, (((1,), (1,)), ((), ())), preferred_element_type=f32) + mb_ref[h:h + 1, :]
            mx = jnp.maximum(jnp.max(s, axis=1, keepdims=True), jnp.max(sm, axis=1, keepdims=True))
            p = jnp.exp(s - mx)
            pm = jnp.exp(sm - mx)
            den = jnp.sum(p, axis=1, keepdims=True) + jnp.sum(pm, axis=1, keepdims=True)
            vh = jnp.where(hm, v, 0.0).astype(bf16)
            vmh = jnp.where(hm, vm, 0.0).astype(bf16)
            o = (jnp.dot(p.astype(bf16), vh, preferred_element_type=f32)
                 + jnp.dot(pm.astype(bf16), vmh, preferred_element_type=f32))
            acc = acc + o / den
        o_ref[...] = acc

    @pl.when(i == nblk)
    def _():
        acc = jnp.zeros(q.shape, f32)
        for h in range(2):
            hm = (lane // ATT_HEAD_DIM) == h
            qh = jnp.where(hm, q, 0.0).astype(bf16)
            sm = lax.dot_general(qh, km, (((1,), (1,)), ((), ())), preferred_element_type=f32) + mb_ref[h:h + 1, :]
            mx = jnp.max(sm, axis=1, keepdims=True)
            pm = jnp.exp(sm - mx)
            den = jnp.sum(pm, axis=1, keepdims=True)
            vmh = jnp.where(hm, vm, 0.0).astype(bf16)
            acc = acc + jnp.dot(pm.astype(bf16), vmh, preferred_element_type=f32) / den
        o_ref[...] = acc


def neighbourhood_attention(proj3, bias_tab, meta_bias, rows):
    b, t, _ = proj3.shape
    g = rows * GRID_W
    assert rows % ATT_RB == 0 and rows >= ATT_KR + ATT_RB
    nblk = rows // ATT_RB
    tq = ATT_RB * GRID_W
    tk = tq // 2
    qc, kc, vc = COL_QKV // 128, (COL_QKV + D_ATT) // 128, (COL_QKV + 2 * D_ATT) // 128
    nkb = g // tk

    def kstart(i):
        return jnp.clip(2 * i - 1, 0, nkb - 4)

    def kspec(j, col):
        return pl.BlockSpec((None, tk, 128), lambda bi, i, p: (bi, kstart(i) + j, col + p))

    def variant(i):
        return jnp.where(i == 0, 0, jnp.where(i >= nblk - 1, 2, 1))

    in_specs = ([pl.BlockSpec((None, tq, 128), lambda bi, i, p: (bi, i, qc + p))]
                + [kspec(j, kc) for j in range(4)]
                + [kspec(j, vc) for j in range(4)]
                + [pl.BlockSpec((None, N_META, 128), lambda bi, i, p: (bi, g // N_META, kc + p)),
                   pl.BlockSpec((None, N_META, 128), lambda bi, i, p: (bi, g // N_META, vc + p)),
                   pl.BlockSpec((None, 2, tq, ATT_KR * GRID_W), lambda bi, i, p: (variant(i), p, 0, 0)),
                   pl.BlockSpec((None, 2, N_META), lambda bi, i, p: (p, 0, 0))])
    return pl.pallas_call(
        functools.partial(_attn_body, nblk=nblk),
        grid=(b, nblk + 1, ATT_HEADS // 2),
        in_specs=in_specs,
        out_specs=pl.BlockSpec((None, tq, 128), lambda bi, i, p: (bi, i, p)),
        out_shape=jax.ShapeDtypeStruct((b, t, D_ATT), f32),
        compiler_params=_params("parallel", "arbitrary", "arbitrary"),
        name="nbr_attention",
    )(proj3, *([proj3] * 10), bias_tab, meta_bias.astype(f32).reshape(ATT_HEADS // 2, 2, N_META))


def _fnet_factors(t):
    best = None
    for n1 in range(8, t, 8):
        if t % n1 == 0 and (best is None or n1 + t // n1 < best[0] + best[1]):
            best = (n1, t // n1)
    assert best is not None
    return best


def _fnet_tables(t):
    n1, n2 = _fnet_factors(t)
    n2p = _cdiv(n2, 8) * 8
    k2 = np.arange(n2, dtype=np.float64)
    ang2 = 2 * np.pi * np.outer(k2, k2) / n2
    fa = np.zeros((2 * n2p, n2), np.float64)
    fa[:n2] = np.cos(ang2)
    fa[n2p:n2p + n2] = -np.sin(ang2)
    t1 = np.arange(n1, dtype=np.float64)
    angt = 2 * np.pi * np.outer(t1, k2) / t
    tc = np.cos(angt)[:, :, None]
    ts = np.sin(angt)[:, :, None]
    ang1 = 2 * np.pi * np.outer(t1, t1) / n1
    c1, s1 = np.cos(ang1), np.sin(ang1)
    fc = np.block([[c1, s1], [-s1, c1]])
    ch = np.arange(FNET_GROUP_DIM, dtype=np.float64)
    angc = 2 * np.pi * np.outer(ch, ch) / FNET_GROUP_DIM
    scale = 1.0 / np.sqrt(t * FNET_GROUP_DIM)
    cc, sc = np.cos(angc) * scale, np.sin(angc) * scale
    to = lambda a: jnp.asarray(a, f32)
    return n1, n2, n2p, to(fa), to(tc), to(ts), to(fc), to(cc), to(sc)


def _fnet_a_body(u_ref, fa_ref, tc_ref, ts_ref, o_ref, *, n2, n2p):
    fa = fa_ref[...]
    for j in range(8):
        r = jnp.dot(fa, u_ref[:, j, :], preferred_element_type=f32, precision=HI)
        re, im = r[:n2], r[n2p:n2p + n2]
        tc, ts = tc_ref[j], ts_ref[j]
        o_ref[0, j] = re * tc + im * ts
        o_ref[1, j] = im * tc - re * ts


def _fnet_c_body(p_ref, fc_ref, o_ref):
    o_ref[...] = jnp.dot(fc_ref[...], p_ref[...], preferred_element_type=f32, precision=HI)


def _fnet_d_body(q_ref, cc_ref, sc_ref, o_ref):
    cc, sc = cc_ref[...], sc_ref[...]
    for g in range(FNET_GROUPS):
        sl = slice(g * FNET_GROUP_DIM, (g + 1) * FNET_GROUP_DIM)
        o_ref[:, sl] = (jnp.dot(q_ref[0, :, sl], cc, preferred_element_type=f32, precision=HI)
                        + jnp.dot(q_ref[1, :, sl], sc, preferred_element_type=f32, precision=HI))


def fourier_mix(u):
    b, t, d = u.shape
    n1, n2, n2p, fa, tc, ts, fc, cc, sc = _fnet_tables(t)
    cb = 512
    p = pl.pallas_call(
        functools.partial(_fnet_a_body, n2=n2, n2p=n2p),
        grid=(b, n1 // 8, d // cb),
        in_specs=[pl.BlockSpec((None, n2, 8, cb), lambda bi, i, c: (bi, 0, i, c)),
                  pl.BlockSpec((2 * n2p, n2), lambda bi, i, c: (0, 0)),
                  pl.BlockSpec((8, n2, 1), lambda bi, i, c: (i, 0, 0)),
                  pl.BlockSpec((8, n2, 1), lambda bi, i, c: (i, 0, 0))],
        out_specs=pl.BlockSpec((None, 2, 8, n2, cb), lambda bi, i, c: (bi, 0, i, 0, c)),
        out_shape=jax.ShapeDtypeStruct((b, 2, n1, n2, d), f32),
        compiler_params=_params("parallel", "parallel", "parallel"),
        name="fnet_stage_a",
    )(u.reshape(b, n2, n1, d), fa, tc, ts)
    cw = 2048
    ncol = n2 * d
    q = pl.pallas_call(
        _fnet_c_body,
        grid=(b, _cdiv(ncol, cw)),
        in_specs=[pl.BlockSpec((None, 2 * n1, cw), lambda bi, c: (bi, 0, c)),
                  pl.BlockSpec((2 * n1, 2 * n1), lambda bi, c: (0, 0))],
        out_specs=pl.BlockSpec((None, 2 * n1, cw), lambda bi, c: (bi, 0, c)),
        out_shape=jax.ShapeDtypeStruct((b, 2 * n1, ncol), f32),
        compiler_params=_params("parallel", "parallel"),
        name="fnet_stage_c",
    )(p.reshape(b, 2 * n1, ncol), fc)
    tm = 512
    return pl.pallas_call(
        _fnet_d_body,
        grid=(b, _cdiv(t, tm)),
        in_specs=[pl.BlockSpec((None, 2, tm, d), lambda bi, i: (bi, 0, i, 0)),
                  pl.BlockSpec((FNET_GROUP_DIM, FNET_GROUP_DIM), lambda bi, i: (0, 0)),
                  pl.BlockSpec((FNET_GROUP_DIM, FNET_GROUP_DIM), lambda bi, i: (0, 0))],
        out_specs=pl.BlockSpec((None, tm, d), lambda bi, i: (bi, i, 0)),
        out_shape=jax.ShapeDtypeStruct((b, t, d), f32),
        compiler_params=_params("parallel", "parallel"),
        name="fnet_stage_d",
    )(q.reshape(b, 2, t, d), cc, sc)


def _ssd_body(*refs, d, nchunks):
    if d == 0:
        (xbc_ref, prev_ref, next_ref, dt_ref, cw_ref, cb_ref, alog_ref, dtb_ref,
         z_ref, yr_ref, dsk_ref, ng_ref, o_ref, st_ref, y_ref) = refs
    else:
        (xbc_ref, prev_ref, next_ref, dt_ref, cw_ref, cb_ref, alog_ref, dtb_ref,
         o_ref, st_ref) = refs
    L = SSM_CHUNK
    j = pl.program_id(1)
    if d == 0:
        ci = jnp.where(j == 0, nchunks, j - 1)
    else:
        ci = jnp.where(j == nchunks, nchunks, nchunks - 1 - j)
    is_meta = ci == nchunks

    @pl.when(j == 0)
    def _():
        st_ref[...] = jnp.zeros(st_ref.shape, f32)

    row = lax.broadcasted_iota(i32, (L, 1), 0)
    nvalid = jnp.where(is_meta, N_META, L)
    valid = row < nvalid

    x = jnp.where(valid, xbc_ref[...], 0.0)
    prev = jnp.where(is_meta, 0.0, prev_ref[7:8, :])
    nxt = jnp.where(ci == nchunks - 1, 0.0, next_ref[0:1, :])
    xp = jnp.where(row == 0, prev, pltpu.roll(x, 1, axis=0))
    xn = jnp.where(row == nvalid - 1, nxt, pltpu.roll(x, L - 1, axis=0))
    w = cw_ref[...]
    pre = w[0:1] * xp + w[1:2] * x + w[2:3] * xn + cb_ref[...]
    xc = jnp.where(valid, pre * _sigmoid(pre), 0.0)
    xs = xc[:, :D_SSM]
    bm = xc[:, D_SSM:D_SSM + SSM_GROUPS * SSM_STATE]
    cm = xc[:, D_SSM + SSM_GROUPS * SSM_STATE:]

    dtf = jnp.where(valid, _softplus(dt_ref[...] + dtb_ref[...]), 0.0)
    da = dtf * (-jnp.exp(alog_ref[...]))
    li = lax.broadcasted_iota(i32, (L, L), 0)
    si = lax.broadcasted_iota(i32, (L, L), 1)
    causal = (si <= li) if d == 0 else (si >= li)
    ac = jnp.dot(causal.astype(f32), da, preferred_element_type=f32, precision=HI)
    act = ac.T
    dtt = dtf.T
    tot = jnp.sum(da, axis=0, keepdims=True)

    lane = lax.broadcasted_iota(i32, (1, 128), 1)
    cbs = {}
    for p in range(SSM_HEADS // 2):
        g = (2 * p) // (SSM_HEADS // SSM_GROUPS)
        bg = bm[:, g * SSM_STATE:(g + 1) * SSM_STATE]
        cg = cm[:, g * SSM_STATE:(g + 1) * SSM_STATE]
        if g not in cbs:
            cbs[g] = lax.dot_general(cg.astype(bf16), bg.astype(bf16), (((1,), (1,)), ((), ())),
                                     preferred_element_type=f32)
        cbg = cbs[g]
        xs_p = xs[:, p * 128:(p + 1) * 128]
        st = st_ref[p]
        y_p = jnp.zeros((L, 128), f32)
        new_st = jnp.zeros((SSM_STATE, 128), f32)
        dec_row = jnp.zeros((1, 128), f32)
        for hh in range(2):
            col = d * SSM_HEADS + 2 * p + hh
            ac_c, ac_r = ac[:, col:col + 1], act[col:col + 1, :]
            dt_c, dt_r = dtf[:, col:col + 1], dtt[col:col + 1, :]
            tot_h = tot[:, col:col + 1]
            hm = (lane // 64) == hh
            m = cbg * jnp.exp(jnp.where(causal, ac_c - ac_r, NEG)) * dt_r
            xm = jnp.where(hm, xs_p, 0.0).astype(bf16)
            stm = jnp.where(hm, st, 0.0).astype(bf16)
            cwt = cg * jnp.exp(ac_c)
            lhs = jnp.concatenate([m, cwt], axis=1).astype(bf16)
            rhs = jnp.concatenate([xm, stm], axis=0)
            y_p = y_p + jnp.dot(lhs, rhs, preferred_element_type=f32)
            bw = (bg * (jnp.exp(tot_h - ac_c) * dt_c)).astype(bf16)
            new_st = new_st + lax.dot_general(bw, xm, (((0,), (0,)), ((), ())), preferred_element_type=f32)
            dec_row = jnp.where(hm, jnp.exp(tot_h), dec_row)
        st_ref[p] = st * dec_row + new_st
        if d == 0:
            y_ref[:, p * 128:(p + 1) * 128] = y_p
        else:
            o_ref[:, p * 128:(p + 1) * 128] = y_p

    if d == 0:
        y = y_ref[...] + yr_ref[...] + xs * dsk_ref[...]
        z = z_ref[...]
        y = y * (z * _sigmoid(z))
        ms = jnp.mean(y * y, axis=-1, keepdims=True)
        o_ref[...] = y * lax.rsqrt(ms + EPS) * ng_ref[...]


def ssd_mixer(proj3, rows, conv_w, conv_b, a_log, dt_bias, d_skip, norm_gain):
    b, t, _ = proj3.shape
    g = rows * GRID_W
    L = SSM_CHUNK
    assert g % L == 0
    nch = g // L
    pad128 = lambda v: jnp.pad(v.astype(f32).reshape(1, -1), ((0, 0), (0, 128 - 2 * SSM_HEADS)))
    alog, dtb = pad128(a_log), pad128(dt_bias)
    cw = conv_w.astype(f32)
    cb = conv_b.astype(f32).reshape(1, -1)
    dsk = jnp.repeat(d_skip.astype(f32), D_SSM // SSM_HEADS).reshape(1, -1)
    ng = norm_gain.astype(f32).reshape(1, -1)
    xcol, zcol, dcol = COL_XBC // D_SSM_CONV, COL_Z // D_SSM, COL_DT // 128
    hb = L // 8

    def run(d, extra_in, extra_specs, scratch):
        if d == 0:
            cidx = lambda j: jnp.where(j == 0, nch, j - 1)
        else:
            cidx = lambda j: jnp.where(j == nch, nch, nch - 1 - j)

        def pidx(j):
            ci = cidx(j)
            return jnp.where(ci == nch, 0, jnp.where(ci == 0, hb * nch + 1, hb * ci - 1))

        def nidx(j):
            ci = cidx(j)
            return jnp.where(ci >= nch - 1, 0, hb * (ci + 1))

        const = lambda shape: pl.BlockSpec(shape, lambda bi, j: (0,) * len(shape))
        in_specs = [pl.BlockSpec((None, L, D_SSM_CONV), lambda bi, j: (bi, cidx(j), xcol)),
                    pl.BlockSpec((None, 8, D_SSM_CONV), lambda bi, j: (bi, pidx(j), xcol)),
                    pl.BlockSpec((None, 8, D_SSM_CONV), lambda bi, j: (bi, nidx(j), xcol)),
                    pl.BlockSpec((None, L, 128), lambda bi, j: (bi, cidx(j), dcol)),
                    const((3, D_SSM_CONV)), const((1, D_SSM_CONV)), const((1, 128)), const((1, 128))]
        in_specs += extra_specs(cidx, const)
        return pl.pallas_call(
            functools.partial(_ssd_body, d=d, nchunks=nch),
            grid=(b, nch + 1),
            in_specs=in_specs,
            out_specs=pl.BlockSpec((None, L, D_SSM), lambda bi, j: (bi, cidx(j), 0)),
            out_shape=jax.ShapeDtypeStruct((b, t, D_SSM), f32),
            scratch_shapes=[pltpu.VMEM((SSM_HEADS // 2, SSM_STATE, 128), f32)] + scratch,
            compiler_params=_params("parallel", "arbitrary"),
            name="ssd_fwd" if d == 0 else "ssd_rev",
        )(proj3, proj3, proj3, proj3, cw, cb, alog, dtb, *extra_in)

    y_rev = run(1, (), lambda cidx, const: [], [])
    return run(
        0, (proj3, y_rev, dsk, ng),
        lambda cidx, const: [pl.BlockSpec((None, L, D_SSM), lambda bi, j: (bi, cidx(j), zcol)),
                             pl.BlockSpec((None, L, D_SSM), lambda bi, j: (bi, cidx(j), 0)),
                             const((1, D_SSM)), const((1, D_SSM))],
        [pltpu.VMEM((L, D_SSM), f32)])


def _merge_body(ya, yf, ys, g0, g1, g2, s_ref, wa, wf, ws, wo, o_ref):
    def branch(y, g, w):
        return _sigmoid(g[...]) * jnp.dot(y[...].astype(bf16), w[...], preferred_element_type=f32)

    merged = branch(ya, g0, wa) + branch(yf, g1, wf) + branch(ys, g2, ws)
    o_ref[...] = s_ref[...] + jnp.dot(merged.astype(bf16), wo[...], preferred_element_type=f32)


def merge_branches(s2d, proj2d, ya, yf, ys, wa, wf, ws, wo, tm=256):
    m, d = s2d.shape
    row = lambda w: pl.BlockSpec((tm, w), lambda i: (i, 0))
    gate = lambda k: pl.BlockSpec((tm, d), lambda i: (i, COL_GATE // d + k))
    wspec = lambda r: pl.BlockSpec((r, d), lambda i: (0, 0), pipeline_mode=pl.Buffered(1))
    return pl.pallas_call(
        _merge_body,
        grid=(_cdiv(m, tm),),
        in_specs=[row(D_ATT), row(D_FNET), row(D_SSM), gate(0), gate(1), gate(2), row(d),
                  wspec(D_ATT), wspec(D_FNET), wspec(D_SSM), wspec(d)],
        out_specs=row(d),
        out_shape=jax.ShapeDtypeStruct((m, d), f32),
        compiler_params=_params("parallel"),
        name="merge_branches",
    )(ya, yf, ys, proj2d, proj2d, proj2d, s2d, wa, wf, ws, wo)


def _router_body(s_ref, g_ref, wrt_ref, wr_ref, tok_ref, afft_ref, aff_ref, *, m_total, tm):
    i = pl.program_id(0)
    x = s_ref[...]
    ms = jnp.mean(x * x, axis=-1, keepdims=True)
    tok = x * lax.rsqrt(ms + EPS) * g_ref[...]
    row = i * tm + lax.broadcasted_iota(i32, (tm, 1), 0)
    tok = jnp.where(row < m_total, tok, 0.0)
    tok_ref[...] = tok.astype(bf16)
    lt = lax.dot_general(wrt_ref[...], tok, (((1,), (1,)), ((), ())), preferred_element_type=f32, precision=HI)
    et = jnp.exp(lt - jnp.max(lt, axis=0, keepdims=True))
    afft_ref[...] = et / jnp.sum(et, axis=0, keepdims=True)
    lg = jnp.dot(tok, wr_ref[...], preferred_element_type=f32, precision=HI)
    ex = jnp.exp(lg - jnp.max(lg, axis=1, keepdims=True))
    aff_ref[...] = ex / jnp.sum(ex, axis=1, keepdims=True)


def router(s2d, gain, w_router, tm=512):
    m, d = s2d.shape
    mp = _cdiv(m, ROUTE_PAD) * ROUTE_PAD
    last_blk = _cdiv(m, tm) - 1
    wr = w_router.astype(f32)
    return pl.pallas_call(
        functools.partial(_router_body, m_total=m, tm=tm),
        grid=(mp // tm,),
        in_specs=[pl.BlockSpec((tm, d), lambda i: (jnp.minimum(i, last_blk), 0)),
                  pl.BlockSpec((1, d), lambda i: (0, 0)),
                  pl.BlockSpec((N_EXPERTS, d), lambda i: (0, 0)),
                  pl.BlockSpec((d, N_EXPERTS), lambda i: (0, 0))],
        out_specs=[pl.BlockSpec((tm, d), lambda i: (i, 0)),
                   pl.BlockSpec((N_EXPERTS, tm), lambda i: (0, i)),
                   pl.BlockSpec((tm, N_EXPERTS), lambda i: (i, 0))],
        out_shape=[jax.ShapeDtypeStruct((mp, d), bf16),
                   jax.ShapeDtypeStruct((N_EXPERTS, mp), f32),
                   jax.ShapeDtypeStruct((mp, N_EXPERTS), f32)],
        compiler_params=_params("parallel"),
        name="moe_router",
    )(s2d, gain.astype(f32).reshape(1, d), wr.T, wr)


def _exclusive_rank(x, nt):
    li = lax.broadcasted_iota(i32, (128, 128), 0)
    lj = lax.broadcasted_iota(i32, (128, 128), 1)
    lane_before = (li < lj).astype(bf16)
    ti = lax.broadcasted_iota(i32, (nt, nt), 0)
    tj = lax.broadcasted_iota(i32, (nt, nt), 1)
    row_before = (tj < ti).astype(bf16)
    xb = x.astype(bf16)
    within = jnp.dot(xb, lane_before, preferred_element_type=f32)
    before = jnp.sum(jnp.dot(row_before, xb, preferred_element_type=f32), axis=1, keepdims=True)
    return within + before


def _count(mask):
    return jnp.sum(jnp.sum(mask.astype(f32), axis=-1, keepdims=True), axis=-2, keepdims=True)


def _select_body(a_ref, sel_ref, *, cap, nt):
    bits = pltpu.bitcast(a_ref[...], i32)

    def step(i, pref):
        cand = pref | jnp.left_shift(jnp.int32(1), 30 - i)
        return jnp.where(_count(bits >= cand) >= cap, cand, pref)

    thr = lax.fori_loop(0, 31, step, jnp.zeros((N_EXPERTS, 1, 1), i32))
    for e in range(N_EXPERTS):
        be = bits[e]
        gt = be > thr[e]
        eq = be == thr[e]
        need = cap - _count(gt)
        take = jnp.logical_and(eq, _exclusive_rank(eq.astype(f32), nt) < need)
        sel_ref[e] = jnp.logical_or(gt, take).astype(f32)


def _rank_body(sel_ref, slot_ref, *, nt):
    for e in range(N_EXPERTS):
        sel = sel_ref[e]
        slot_ref[e] = jnp.where(sel > 0.5, _exclusive_rank(sel, nt), -1.0).astype(i32)


def select_tokens(aff3, cap):
    e, nt, _ = aff3.shape
    return pl.pallas_call(
        functools.partial(_select_body, cap=cap, nt=nt),
        out_shape=jax.ShapeDtypeStruct(aff3.shape, f32),
        compiler_params=pltpu.CompilerParams(vmem_limit_bytes=VMEM_LIMIT),
        name="moe_select",
    )(aff3)


def rank_tokens(sel3):
    e, nt, _ = sel3.shape
    return pl.pallas_call(
        functools.partial(_rank_body, nt=nt),
        out_shape=jax.ShapeDtypeStruct(sel3.shape, i32),
        compiler_params=pltpu.CompilerParams(vmem_limit_bytes=VMEM_LIMIT),
        name="moe_rank",
    )(sel3)


def _gather_body(i_ref, j_ref, f_ref, tok_ref, slot_ref, o_ref, acc_ref, *, ns):
    k = pl.program_id(0) * ns + pl.program_id(1)
    flags = f_ref[k]

    @pl.when((flags & FLAG_VALID) != 0)
    def _():
        want = lax.broadcasted_iota(i32, (SLOT_BLK, GATHER_TOK), 0) + j_ref[k] * SLOT_BLK
        onehot = jnp.where(slot_ref[...] == want, 1.0, 0.0).astype(bf16)
        rows = jnp.dot(onehot, tok_ref[...], preferred_element_type=f32)

        @pl.when((flags & FLAG_FIRST) != 0)
        def _():
            acc_ref[...] = rows

        @pl.when((flags & FLAG_FIRST) == 0)
        def _():
            acc_ref[...] += rows

    @pl.when((flags & FLAG_LAST) != 0)
    def _():
        o_ref[...] = acc_ref[...].astype(bf16)


def gather_tokens(tok, slot_row3, sched, cap_pad, ns):
    d = tok.shape[1]
    ii, jj, ff = sched
    return pl.pallas_call(
        functools.partial(_gather_body, ns=ns),
        grid_spec=pltpu.PrefetchScalarGridSpec(
            num_scalar_prefetch=3,
            grid=(N_EXPERTS, ns),
            in_specs=[pl.BlockSpec((GATHER_TOK, d), lambda e, s, ii, jj, ff: (ii[e * ns + s], 0)),
                      pl.BlockSpec((None, 1, GATHER_TOK), lambda e, s, ii, jj, ff: (e, 0, ii[e * ns + s]))],
            out_specs=pl.BlockSpec((None, SLOT_BLK, d), lambda e, s, ii, jj, ff: (e, jj[e * ns + s], 0)),
            scratch_shapes=[pltpu.VMEM((SLOT_BLK, d), f32)]),
        out_shape=jax.ShapeDtypeStruct((N_EXPERTS, cap_pad, d), bf16),
        compiler_params=_params("parallel", "arbitrary"),
        name="moe_gather",
    )(ii, jj, ff, tok, slot_row3)


def _ffn_body(x_ref, wg_ref, wu_ref, wd_ref, o_ref):
    @pl.when(pl.program_id(2) == 0)
    def _():
        o_ref[...] = jnp.zeros(o_ref.shape, f32)

    x = x_ref[...]
    gt = jnp.dot(x, wg_ref[...].astype(bf16), preferred_element_type=f32)
    up = jnp.dot(x, wu_ref[...].astype(bf16), preferred_element_type=f32)
    hid = (gt * _sigmoid(gt) * up).astype(bf16)
    o_ref[...] += jnp.dot(hid, wd_ref[...].astype(bf16), preferred_element_type=f32)


def expert_ffn(xe, w_gate, w_up, w_down, layer, tf=512, tm_max=768):
    e, cap_pad, d = xe.shape
    dff = w_gate.shape[-1]
    nm = 1
    while cap_pad // nm > tm_max or cap_pad % nm or (cap_pad // nm) % 8:
        nm += 1
    tm = cap_pad // nm
    return pl.pallas_call(
        _ffn_body,
        grid=(e, nm, dff // tf),
        in_specs=[pl.BlockSpec((None, tm, d), lambda ei, mi, fi: (ei, mi, 0)),
                  pl.BlockSpec((None, None, d, tf), lambda ei, mi, fi: (layer, ei, 0, fi)),
                  pl.BlockSpec((None, None, d, tf), lambda ei, mi, fi: (layer, ei, 0, fi)),
                  pl.BlockSpec((None, None, tf, d), lambda ei, mi, fi: (layer, ei, fi, 0))],
        out_specs=pl.BlockSpec((None, tm, d), lambda ei, mi, fi: (ei, mi, 0)),
        out_shape=jax.ShapeDtypeStruct((e, cap_pad, d), f32),
        compiler_params=_params("parallel", "parallel", "arbitrary"),
        name="moe_expert_ffn",
    )(xe, w_gate, w_up, w_down)


def _combine_body(i_ref, e_ref, j_ref, f_ref, s_ref, ye_ref, slot_ref, aff_ref, o_ref):
    k = pl.program_id(0)
    flags = f_ref[k]

    @pl.when((flags & FLAG_FIRST) != 0)
    def _():
        o_ref[...] = s_ref[...]

    @pl.when((flags & FLAG_VALID) != 0)
    def _():
        lane = lax.broadcasted_iota(i32, (1, N_EXPERTS), 1)
        pick = lane == e_ref[k]
        slot_col = jnp.sum(jnp.where(pick, slot_ref[...], 0.0), axis=1, keepdims=True)
        gate_col = jnp.sum(jnp.where(pick, aff_ref[...], 0.0), axis=1, keepdims=True)
        want = (lax.broadcasted_iota(i32, (1, SLOT_BLK), 1) + j_ref[k] * SLOT_BLK).astype(f32)
        onehot = jnp.where(slot_col == want, 1.0, 0.0).astype(bf16)
        ye = ye_ref[...]
        hi = ye.astype(bf16)
        lo = (ye - hi.astype(f32)).astype(bf16)
        picked = (jnp.dot(onehot, hi, preferred_element_type=f32)
                  + jnp.dot(onehot, lo, preferred_element_type=f32))
        o_ref[...] += gate_col * picked


def combine_tokens(s2d, ye, slot_tok, aff, sched):
    m, d = s2d.shape
    ii, ee, jj, ff = sched
    nsteps = ii.shape[0]
    return pl.pallas_call(
        _combine_body,
        grid_spec=pltpu.PrefetchScalarGridSpec(
            num_scalar_prefetch=4,
            grid=(nsteps,),
            in_specs=[pl.BlockSpec((COMBINE_TOK, d), lambda k, ii, ee, jj, ff: (ii[k], 0)),
                      pl.BlockSpec((None, SLOT_BLK, d), lambda k, ii, ee, jj, ff: (ee[k], jj[k], 0)),
                      pl.BlockSpec((COMBINE_TOK, N_EXPERTS), lambda k, ii, ee, jj, ff: (ii[k], 0)),
                      pl.BlockSpec((COMBINE_TOK, N_EXPERTS), lambda k, ii, ee, jj, ff: (ii[k], 0))],
            out_specs=pl.BlockSpec((COMBINE_TOK, d), lambda k, ii, ee, jj, ff: (ii[k], 0))),
        out_shape=jax.ShapeDtypeStruct((m, d), f32),
        compiler_params=_params("arbitrary"),
        name="moe_combine",
    )(ii, ee, jj, ff, s2d, ye, slot_tok, aff)


def _pair_lists(slot, m, tok_size, n_slot_blk):
    e = slot.shape[0]
    n_tok_blk = _cdiv(m, tok_size)
    cnt = jnp.sum((slot[:, :n_tok_blk * tok_size] >= 0).reshape(e, n_tok_blk, tok_size), axis=-1).astype(i32)
    cum_in = jnp.cumsum(cnt, axis=1)
    cum_ex = cum_in - cnt
    jlo = cum_ex // SLOT_BLK
    jhi = (cum_in - 1) // SLOT_BLK
    npairs = jnp.where(cnt > 0, jhi - jlo + 1, 0)
    off_in = jnp.cumsum(npairs, axis=1)
    off_ex = off_in - npairs
    total = off_in[:, -1:]
    ns = n_tok_blk + n_slot_blk
    step = jnp.arange(ns, dtype=i32)[None, :]
    valid = step < total
    sc = jnp.minimum(step, total - 1)
    tok_blk = jnp.sum(off_in[:, None, :] <= sc[:, :, None], axis=-1).astype(i32)
    tok_blk = jnp.minimum(tok_blk, n_tok_blk - 1)
    slot_blk = (jnp.take_along_axis(jlo, tok_blk, axis=1)
                + sc - jnp.take_along_axis(off_ex, tok_blk, axis=1)).astype(i32)
    return tok_blk, slot_blk, valid


def _gather_schedule(slot, m, n_slot_blk):
    tok_blk, slot_blk, valid = _pair_lists(slot, m, GATHER_TOK, n_slot_blk)
    e, ns = tok_blk.shape
    change = slot_blk[:, 1:] != slot_blk[:, :-1]
    first = jnp.concatenate([jnp.ones((e, 1), bool), change], axis=1)
    last = jnp.concatenate([change | ~valid[:, 1:], jnp.ones((e, 1), bool)], axis=1) & valid
    flags = first * FLAG_FIRST + valid * FLAG_VALID + last * FLAG_LAST
    return tuple(a.reshape(-1).astype(i32) for a in (tok_blk, slot_blk, flags)), ns


def _combine_schedule(slot, m, n_slot_blk):
    tok_blk, slot_blk, valid = _pair_lists(slot, m, COMBINE_TOK, n_slot_blk)
    e, ns = tok_blk.shape
    n_tok_blk = _cdiv(m, COMBINE_TOK)
    stride = e * (n_slot_blk + 1) + 1
    eid = jnp.broadcast_to(jnp.arange(e, dtype=i32)[:, None], (e, ns))
    key_pairs = jnp.where(valid, tok_blk * stride + 1 + eid * (n_slot_blk + 1) + slot_blk, jnp.int32(2 ** 30))
    init_tok = jnp.arange(n_tok_blk, dtype=i32)
    key = jnp.concatenate([init_tok * stride, key_pairs.reshape(-1)])
    c_tok = jnp.concatenate([init_tok, tok_blk.reshape(-1)])
    c_exp = jnp.concatenate([jnp.zeros_like(init_tok), eid.reshape(-1)])
    c_slot = jnp.concatenate([jnp.zeros_like(init_tok), slot_blk.reshape(-1)])
    c_valid = jnp.concatenate([jnp.zeros_like(init_tok), valid.reshape(-1).astype(i32)])
    n_all = key.shape[0]
    n_real = n_tok_blk + jnp.sum(valid)
    order = jnp.argsort(key)[jnp.minimum(jnp.arange(n_all), n_real - 1)]
    c_tok, c_exp, c_slot = c_tok[order], c_exp[order], c_slot[order]
    c_valid = jnp.where(jnp.arange(n_all) < n_real, c_valid[order], 0)
    c_first = jnp.concatenate([jnp.ones((1,), i32), (c_tok[1:] != c_tok[:-1]).astype(i32)])
    flags = c_first * FLAG_FIRST + c_valid * FLAG_VALID
    return tuple(a.astype(i32) for a in (c_tok, c_exp, c_slot, flags))


def expert_choice_ffn(s3, gain, w_router, w_gate, w_up, w_down, layer):
    b, t, d = s3.shape
    m = b * t
    cap = (EC_CAPACITY * m) // N_EXPERTS
    s2d = s3.reshape(m, d)
    tok, afft, aff = router(s2d, gain, w_router)

    n_pad = _cdiv(m, ROUTE_PAD) * ROUTE_PAD
    nt = n_pad // 128
    a = jnp.roll(afft[:, :m].reshape(N_EXPERTS, b, t), N_META, axis=2).reshape(N_EXPERTS, m)
    a = jnp.pad(a, ((0, 0), (0, n_pad - m)), constant_values=-1.0)
    sel = select_tokens(a.reshape(N_EXPERTS, nt, 128), cap).reshape(N_EXPERTS, n_pad)
    sel = jnp.roll(sel[:, :m].reshape(N_EXPERTS, b, t), -N_META, axis=2).reshape(N_EXPERTS, m)
    sel = jnp.pad(sel, ((0, 0), (0, n_pad - m)))
    slot = rank_tokens(sel.reshape(N_EXPERTS, nt, 128)).reshape(N_EXPERTS, n_pad)

    n_slot_blk = _cdiv(cap, SLOT_BLK)
    cap_pad = n_slot_blk * SLOT_BLK
    g_sched, ns = _gather_schedule(slot, m, n_slot_blk)
    xe = gather_tokens(tok, slot.reshape(N_EXPERTS, 1, n_pad), g_sched, cap_pad, ns)
    ye = expert_ffn(xe, w_gate, w_up, w_down, layer)
    out = combine_tokens(s2d, ye, slot.T.astype(f32), aff, _combine_schedule(slot, m, n_slot_blk))
    return out.reshape(b, t, d)


def _final_norm_body(x_ref, g_ref, o_ref):
    x = x_ref[...]
    ms = jnp.mean(x * x, axis=-1, keepdims=True)
    o_ref[...] = x * lax.rsqrt(ms + EPS) * g_ref[...]


def final_norm(s3, gain, g, tm=512):
    b, t, d = s3.shape
    assert g % tm == 0
    return pl.pallas_call(
        _final_norm_body,
        grid=(b, g // tm),
        in_specs=[pl.BlockSpec((None, tm, d), lambda bi, i: (bi, i, 0)),
                  pl.BlockSpec((1, d), lambda bi, i: (0, 0))],
        out_specs=pl.BlockSpec((None, tm, d), lambda bi, i: (bi, i, 0)),
        out_shape=jax.ShapeDtypeStruct((b, g, d), f32),
        compiler_params=_params("parallel", "parallel"),
        name="final_norm",
    )(s3, gain.astype(f32).reshape(1, d))


def _reorder_w_in(w):
    cuts = np.cumsum([3 * D_ATT, D_FNET, D_SSM_CONV, D_SSM, 2 * SSM_HEADS])
    qkv, u_f, xbc, z, dt, gate = jnp.split(w, [int(c) for c in cuts], axis=1)
    dt = jnp.pad(dt, ((0, 0), (0, N_PROJ - COL_DT - 2 * SSM_HEADS)))
    return jnp.concatenate([gate, qkv, u_f, xbc, z, dt], axis=1).astype(bf16)


def encoder_layer(s3, rows, layer, lw):
    b, t, d = s3.shape
    m = b * t
    proj = norm_matmul(s3.reshape(m, d), lw["norm1_gain"], lw["w_in"])
    proj3 = proj.reshape(b, t, N_PROJ)
    y_att = neighbourhood_attention(proj3, lw["bias_tab"], lw["meta_bias"], rows)
    u_orig = jnp.roll(proj3[:, :, COL_F:COL_F + D_FNET], N_META, axis=1)
    y_f = jnp.roll(fourier_mix(u_orig), -N_META, axis=1)
    y_s = ssd_mixer(proj3, rows, lw["conv_w"], lw["conv_b"], lw["a_log"], lw["dt_bias"], lw["d_skip"],
                    lw["ssd_norm_gain"])
    s2d = merge_branches(s3.reshape(m, d), proj, y_att.reshape(m, -1), y_f.reshape(m, -1), y_s.reshape(m, -1),
                         lw["w_branch_a"], lw["w_branch_f"], lw["w_branch_s"], lw["w_out"])
    return expert_choice_ffn(s2d.reshape(b, t, d), lw["norm2_gain"], lw["w_router"],
                             lw["w_exp_gate"], lw["w_exp_up"], lw["w_exp_down"], layer)


def encode(x, meta_tokens, final_gain, layers):
    b, g, d = x.shape
    rows = g // GRID_W
    meta = jnp.broadcast_to(meta_tokens.astype(x.dtype)[None], (b, N_META, d))
    s = jnp.concatenate([x, meta], axis=1)
    for layer, lw in enumerate(layers):
        s = encoder_layer(s, rows, layer, lw)
    return final_norm(s, final_gain, g)


def kernel(x_prompt, x_sample, meta_tokens, norm1_gain, w_in, rel_bias, meta_bias, conv_w, conv_b, a_log, dt_bias,
           d_skip, ssd_norm_gain, w_branch_a, w_branch_f, w_branch_s, w_out, norm2_gain, w_router, w_exp_gate,
           w_exp_up, w_exp_down, final_gain):
    depth = w_in.shape[0]
    rows_set = {x_prompt.shape[1] // GRID_W, x_sample.shape[1] // GRID_W}
    layers = []
    for l in range(depth):
        rb = rel_bias[l]
        layers.append({
            "norm1_gain": norm1_gain[l].astype(f32), "w_in": _reorder_w_in(w_in[l]),
            "rel_bias": rb, "meta_bias": meta_bias[l],
            "conv_w": conv_w[l], "conv_b": conv_b[l], "a_log": a_log[l], "dt_bias": dt_bias[l],
            "d_skip": d_skip[l], "ssd_norm_gain": ssd_norm_gain[l],
            "w_branch_a": w_branch_a[l].astype(bf16), "w_branch_f": w_branch_f[l].astype(bf16),
            "w_branch_s": w_branch_s[l].astype(bf16), "w_out": w_out[l].astype(bf16),
            "norm2_gain": norm2_gain[l], "w_router": w_router[l],
            "w_exp_gate": w_exp_gate, "w_exp_up": w_exp_up, "w_exp_down": w_exp_down,
            "bias_tab": attention_bias_tables(rb, max(rows_set)),
        })
    y_prompt = encode(x_prompt, meta_tokens, final_gain, layers)
    y_sample = encode(x_sample, meta_tokens, final_gain, layers)
    return (y_prompt, y_sample)
```

```python
import functools

import numpy as np
import jax
import jax.numpy as jnp
from jax import lax
from jax.experimental import pallas as pl
from jax.experimental.pallas import tpu as pltpu

f32 = jnp.float32
bf16 = jnp.bfloat16
i32 = jnp.int32
HI = lax.Precision.HIGHEST

D_MODEL = 2048
N_META = 16
GRID_W = 64
ATT_HEADS = 16
ATT_HEAD_DIM = 64
D_ATT = 1024
WIN_R = 8
WIN_C = 16
FNET_GROUPS = 4
D_FNET = 1024
FNET_GROUP_DIM = 256
SSM_HEADS = 16
D_SSM = 1024
SSM_GROUPS = 4
SSM_STATE = 128
SSM_CHUNK = 128
D_SSM_CONV = 2048
N_EXPERTS = 16
EC_CAPACITY = 2
EPS = 1e-6

COL_GATE = 0
COL_QKV = 6144
COL_F = 9216
COL_XBC = 10240
COL_Z = 12288
COL_DT = 13312
N_PROJ = 13824

NEG = -1e30
VMEM_LIMIT = 56 * 1024 * 1024

SLOT_BLK = 256
GATHER_TOK = 512
COMBINE_TOK = 512
ROUTE_PAD = 1024
FLAG_FIRST, FLAG_VALID, FLAG_LAST = 1, 2, 4
ATT_RB = 8
ATT_KR = 16


def _cdiv(a, b):
    return -(-a // b)


def _params(*sem):
    return pltpu.CompilerParams(dimension_semantics=sem, vmem_limit_bytes=VMEM_LIMIT)


def _sigmoid(x):
    return 1.0 / (1.0 + jnp.exp(-x))


def _softplus(x):
    return jnp.maximum(x, 0.0) + jnp.log(1.0 + jnp.exp(-jnp.abs(x)))


def _norm_matmul_body(x_ref, g_ref, w_ref, o_ref, h_ref):
    @pl.when(pl.program_id(1) == 0)
    def _():
        x = x_ref[...]
        ms = jnp.mean(x * x, axis=-1, keepdims=True)
        h_ref[...] = (x * lax.rsqrt(ms + EPS) * g_ref[...]).astype(bf16)

    o_ref[...] = jnp.dot(h_ref[...], w_ref[...], preferred_element_type=f32)


def norm_matmul(x2d, gain, w, tm=1024, tn=1536):
    m, d = x2d.shape
    n = w.shape[1]
    assert n % tn == 0
    return pl.pallas_call(
        _norm_matmul_body,
        grid=(_cdiv(m, tm), n // tn),
        in_specs=[pl.BlockSpec((tm, d), lambda i, j: (i, 0)),
                  pl.BlockSpec((1, d), lambda i, j: (0, 0)),
                  pl.BlockSpec((d, tn), lambda i, j: (0, j))],
        out_specs=pl.BlockSpec((tm, tn), lambda i, j: (i, j)),
        out_shape=jax.ShapeDtypeStruct((m, n), f32),
        scratch_shapes=[pltpu.VMEM((tm, d), bf16)],
        compiler_params=_params("parallel", "arbitrary"),
        name="norm_matmul",
    )(x2d, gain.reshape(1, d), w)


def attention_bias_tables(rel_bias, rows):
    rel = rel_bias.astype(f32)
    nh = rel.shape[0]
    cols = []
    for c in range(GRID_W):
        cs = min(max(c - WIN_C // 2, 0), GRID_W - WIN_C)
        j0 = cs - c + (WIN_C - 1)
        cols.append(jnp.pad(rel[:, :, j0:j0 + WIN_C], ((0, 0), (0, 0), (cs, GRID_W - WIN_C - cs)),
                            constant_values=NEG))
    colbias = jnp.transpose(jnp.stack(cols, axis=2), (0, 2, 1, 3))
    lo = ATT_KR - WIN_R
    nd = 2 * WIN_R - 1 + 2 * lo
    flat = jnp.pad(colbias, ((0, 0), (0, 0), (lo, lo), (0, 0)), constant_values=NEG).reshape(nh, GRID_W, nd * GRID_W)
    lane_row = np.arange(ATT_KR * GRID_W) // GRID_W
    tabs = []
    for r0, k0 in ((0, 0), (ATT_RB, ATT_RB - WIN_R // 2), (rows - ATT_RB, rows - ATT_KR)):
        per_row = []
        for rq in range(ATT_RB):
            r = r0 + rq
            rs = min(max(r - WIN_R // 2, 0), rows - WIN_R)
            start = (k0 - r + (WIN_R - 1) + lo) * GRID_W
            seen = (lane_row >= rs - k0) & (lane_row < rs - k0 + WIN_R)
            per_row.append(jnp.where(seen, flat[:, :, start:start + ATT_KR * GRID_W], NEG))
        tabs.append(jnp.stack(per_row, axis=1).reshape(nh, ATT_RB * GRID_W, ATT_KR * GRID_W))
    return jnp.stack(tabs)


def _attn_body(q_ref, k0, k1, k2, k3, v0, v1, v2, v3, km_ref, vm_ref, bias_ref, mb_ref, o_ref, *, nblk):
    i = pl.program_id(1)
    lane = lax.broadcasted_iota(i32, (1, 128), 1)
    q = q_ref[...] * (ATT_HEAD_DIM ** -0.5)
    km = km_ref[...].astype(bf16)
    vm = vm_ref[...]

    @pl.when(i < nblk)
    def _():
        k = jnp.concatenate([k0[...], k1[...], k2[...], k3[...]], axis=0).astype(bf16)
        v = jnp.concatenate([v0[...], v1[...], v2[...], v3[...]], axis=0)
        acc = jnp.zeros(q.shape, f32)
        for h in range(2):
            hm = (lane // ATT_HEAD_DIM) == h
            qh = jnp.where(hm, q, 0.0).astype(bf16)
            s = lax.dot_general(qh, k, (((1,), (1,)), ((), ())), preferred_element_type=f32) + bias_ref[h]
            sm = lax.dot_general(qh, km, (((1,), (1,)), ((), ())), preferred_element_type=f32) + mb_ref[h:h + 1, :]
            mx = jnp.maximum(jnp.max(s, axis=1, keepdims=True), jnp.max(sm, axis=1, keepdims=True))
            p = jnp.exp(s - mx)
            pm = jnp.exp(sm - mx)
            den = jnp.sum(p, axis=1, keepdims=True) + jnp.sum(pm, axis=1, keepdims=True)
            vh = jnp.where(hm, v, 0.0).astype(bf16)
            vmh = jnp.where(hm, vm, 0.0).astype(bf16)
            o = (jnp.dot(p.astype(bf16), vh, preferred_element_type=f32)
                 + jnp.dot(pm.astype(bf16), vmh, preferred_element_type=f32))
            acc = acc + o / den
        o_ref[...] = acc

    @pl.when(i == nblk)
    def _():
        acc = jnp.zeros(q.shape, f32)
        for h in range(2):
            hm = (lane // ATT_HEAD_DIM) == h
            qh = jnp.where(hm, q, 0.0).astype(bf16)
            sm = lax.dot_general(qh, km, (((1,), (1,)), ((), ())), preferred_element_type=f32) + mb_ref[h:h + 1, :]
            mx = jnp.max(sm, axis=1, keepdims=True)
            pm = jnp.exp(sm - mx)
            den = jnp.sum(pm, axis=1, keepdims=True)
            vmh = jnp.where(hm, vm, 0.0).astype(bf16)
            acc = acc + jnp.dot(pm.astype(bf16), vmh, preferred_element_type=f32) / den
        o_ref[...] = acc


def neighbourhood_attention(proj3, bias_tab, meta_bias, rows):
    b, t, _ = proj3.shape
    g = rows * GRID_W
    assert rows % ATT_RB == 0 and rows >= ATT_KR + ATT_RB
    nblk = rows // ATT_RB
    tq = ATT_RB * GRID_W
    tk = tq // 2
    qc, kc, vc = COL_QKV // 128, (COL_QKV + D_ATT) // 128, (COL_QKV + 2 * D_ATT) // 128
    nkb = g // tk

    def kstart(i):
        return jnp.clip(2 * i - 1, 0, nkb - 4)

    def kspec(j, col):
        return pl.BlockSpec((None, tk, 128), lambda bi, i, p: (bi, kstart(i) + j, col + p))

    def variant(i):
        return jnp.where(i == 0, 0, jnp.where(i >= nblk - 1, 2, 1))

    in_specs = ([pl.BlockSpec((None, tq, 128), lambda bi, i, p: (bi, i, qc + p))]
                + [kspec(j, kc) for j in range(4)]
                + [kspec(j, vc) for j in range(4)]
                + [pl.BlockSpec((None, N_META, 128), lambda bi, i, p: (bi, g // N_META, kc + p)),
                   pl.BlockSpec((None, N_META, 128), lambda bi, i, p: (bi, g // N_META, vc + p)),
                   pl.BlockSpec((None, 2, tq, ATT_KR * GRID_W), lambda bi, i, p: (variant(i), p, 0, 0)),
                   pl.BlockSpec((None, 2, N_META), lambda bi, i, p: (p, 0, 0))])
    return pl.pallas_call(
        functools.partial(_attn_body, nblk=nblk),
        grid=(b, nblk + 1, ATT_HEADS // 2),
        in_specs=in_specs,
        out_specs=pl.BlockSpec((None, tq, 128), lambda bi, i, p: (bi, i, p)),
        out_shape=jax.ShapeDtypeStruct((b, t, D_ATT), f32),
        compiler_params=_params("parallel", "arbitrary", "arbitrary"),
        name="nbr_attention",
    )(proj3, *([proj3] * 10), bias_tab, meta_bias.astype(f32).reshape(ATT_HEADS // 2, 2, N_META))


def _fnet_factors(t):
    best = None
    for n1 in range(8, t, 8):
        if t % n1 == 0 and (best is None or n1 + t // n1 < best[0] + best[1]):
            best = (n1, t // n1)
    assert best is not None
    return best


def _cos_sin(num, den):
    ang = 2 * np.pi * (num % den).astype(np.float64) / den
    return np.cos(ang), np.sin(ang)


def _fnet_tables(t, shift):
    n1, n2 = _fnet_factors(t)
    n2p = _cdiv(n2, 8) * 8
    t1, t2 = np.arange(n1, dtype=np.int64), np.arange(n2, dtype=np.int64)
    k1, k2 = t1, t2
    c2, s2 = _cos_sin(np.outer(k2 + shift, t2), n2)
    fa = np.zeros((2 * n2p, n2), np.float64)
    fa[:n2], fa[n2p:n2p + n2] = c2, -s2
    tc, ts = _cos_sin(np.outer(t1 + shift, k2 + shift), t)
    c1, s1 = _cos_sin(np.outer(k1, t1 + shift), n1)
    fc = np.block([[c1, s1], [-s1, c1]])
    ch = np.arange(FNET_GROUP_DIM, dtype=np.int64)
    cc, sc = _cos_sin(np.outer(ch, ch), FNET_GROUP_DIM)
    scale = 1.0 / np.sqrt(t * FNET_GROUP_DIM)
    to = lambda a: jnp.asarray(a, f32)
    return (n1, n2, n2p, to(fa), to(tc[:, :, None]), to(ts[:, :, None]), to(fc), to(cc * scale), to(sc * scale))


def _dot_f32(a, b):
    return jnp.dot(a, b, preferred_element_type=f32, precision=HI)


def _fnet_a_body(u_ref, fa_ref, tc_ref, ts_ref, o_ref, *, n2, n2p):
    fa = fa_ref[...]
    for j in range(8):
        r = _dot_f32(fa, u_ref[:, j, :])
        re, im = r[:n2], r[n2p:n2p + n2]
        tc, ts = tc_ref[j], ts_ref[j]
        o_ref[0, j] = re * tc + im * ts
        o_ref[1, j] = im * tc - re * ts


def _fnet_c_body(p_ref, fc_ref, o_ref):
    o_ref[...] = _dot_f32(fc_ref[...], p_ref[...])


def _fnet_d_body(q_ref, cc_ref, sc_ref, o_ref):
    cc, sc = cc_ref[...], sc_ref[...]
    for g in range(FNET_GROUPS):
        sl = slice(g * FNET_GROUP_DIM, (g + 1) * FNET_GROUP_DIM)
        o_ref[:, sl] = _dot_f32(q_ref[0, :, sl], cc) + _dot_f32(q_ref[1, :, sl], sc)


def fourier_mix(x3, col0, shift):
    b, t, ncols = x3.shape
    d = D_FNET
    n1, n2, n2p, fa, tc, ts, fc, cc, sc = _fnet_tables(t, shift)
    cb = 512
    assert col0 % cb == 0
    const2 = lambda shape: pl.BlockSpec(shape, lambda *_: (0, 0))
    p = pl.pallas_call(
        functools.partial(_fnet_a_body, n2=n2, n2p=n2p),
        grid=(b, n1 // 8, d // cb),
        in_specs=[pl.BlockSpec((None, n2, 8, cb), lambda bi, i, c: (bi, 0, i, col0 // cb + c)),
                  const2((2 * n2p, n2)),
                  pl.BlockSpec((8, n2, 1), lambda bi, i, c: (i, 0, 0)),
                  pl.BlockSpec((8, n2, 1), lambda bi, i, c: (i, 0, 0))],
        out_specs=pl.BlockSpec((None, 2, 8, n2, cb), lambda bi, i, c: (bi, 0, i, 0, c)),
        out_shape=jax.ShapeDtypeStruct((b, 2, n1, n2, d), f32),
        compiler_params=_params("parallel", "parallel", "parallel"),
        name="fnet_stage_a",
    )(x3.reshape(b, n2, n1, ncols), fa, tc, ts)
    cw = 2048
    ncol = n2 * d
    q = pl.pallas_call(
        _fnet_c_body,
        grid=(b, _cdiv(ncol, cw)),
        in_specs=[pl.BlockSpec((None, 2 * n1, cw), lambda bi, c: (bi, 0, c)),
                  const2((2 * n1, 2 * n1))],
        out_specs=pl.BlockSpec((None, 2 * n1, cw), lambda bi, c: (bi, 0, c)),
        out_shape=jax.ShapeDtypeStruct((b, 2 * n1, ncol), f32),
        compiler_params=_params("parallel", "parallel"),
        name="fnet_stage_c",
    )(p.reshape(b, 2 * n1, ncol), fc)
    tm = 512
    gd = (FNET_GROUP_DIM, FNET_GROUP_DIM)
    return pl.pallas_call(
        _fnet_d_body,
        grid=(b, _cdiv(t, tm)),
        in_specs=[pl.BlockSpec((None, 2, tm, d), lambda bi, i: (bi, 0, i, 0)),
                  const2(gd), const2(gd)],
        out_specs=pl.BlockSpec((None, tm, d), lambda bi, i: (bi, i, 0)),
        out_shape=jax.ShapeDtypeStruct((b, t, d), f32),
        compiler_params=_params("parallel", "parallel"),
        name="fnet_stage_d",
    )(q.reshape(b, 2, t, d), cc, sc)


def _ssd_body(*refs, d, nchunks):
    if d == 0:
        (xbc_ref, prev_ref, next_ref, dt_ref, cw_ref, cb_ref, alog_ref, dtb_ref,
         z_ref, yr_ref, dsk_ref, ng_ref, o_ref, st_ref, y_ref) = refs
    else:
        (xbc_ref, prev_ref, next_ref, dt_ref, cw_ref, cb_ref, alog_ref, dtb_ref,
         o_ref, st_ref) = refs
    L = SSM_CHUNK
    j = pl.program_id(1)
    if d == 0:
        ci = jnp.where(j == 0, nchunks, j - 1)
    else:
        ci = jnp.where(j == nchunks, nchunks, nchunks - 1 - j)
    is_meta = ci == nchunks

    @pl.when(j == 0)
    def _():
        st_ref[...] = jnp.zeros(st_ref.shape, f32)

    row = lax.broadcasted_iota(i32, (L, 1), 0)
    nvalid = jnp.where(is_meta, N_META, L)
    valid = row < nvalid

    x = jnp.where(valid, xbc_ref[...], 0.0)
    prev = jnp.where(is_meta, 0.0, prev_ref[7:8, :])
    nxt = jnp.where(ci == nchunks - 1, 0.0, next_ref[0:1, :])
    xp = jnp.where(row == 0, prev, pltpu.roll(x, 1, axis=0))
    xn = jnp.where(row == nvalid - 1, nxt, pltpu.roll(x, L - 1, axis=0))
    w = cw_ref[...]
    pre = w[0:1] * xp + w[1:2] * x + w[2:3] * xn + cb_ref[...]
    xc = jnp.where(valid, pre * _sigmoid(pre), 0.0)
    xs = xc[:, :D_SSM]
    bm = xc[:, D_SSM:D_SSM + SSM_GROUPS * SSM_STATE]
    cm = xc[:, D_SSM + SSM_GROUPS * SSM_STATE:]

    dtf = jnp.where(valid, _softplus(dt_ref[...] + dtb_ref[...]), 0.0)
    da = dtf * (-jnp.exp(alog_ref[...]))
    li = lax.broadcasted_iota(i32, (L, L), 0)
    si = lax.broadcasted_iota(i32, (L, L), 1)
    causal = (si <= li) if d == 0 else (si >= li)
    ac = jnp.dot(causal.astype(f32), da, preferred_element_type=f32, precision=HI)
    act = ac.T
    dtt = dtf.T
    tot = jnp.sum(da, axis=0, keepdims=True)

    lane = lax.broadcasted_iota(i32, (1, 128), 1)
    cbs = {}
    for p in range(SSM_HEADS // 2):
        g = (2 * p) // (SSM_HEADS // SSM_GROUPS)
        bg = bm[:, g * SSM_STATE:(g + 1) * SSM_STATE]
        cg = cm[:, g * SSM_STATE:(g + 1) * SSM_STATE]
        if g not in cbs:
            cbs[g] = lax.dot_general(cg.astype(bf16), bg.astype(bf16), (((1,), (1,)), ((), ())),
                                     preferred_element_type=f32)
        cbg = cbs[g]
        xs_p = xs[:, p * 128:(p + 1) * 128]
        st = st_ref[p]
        y_p = jnp.zeros((L, 128), f32)
        new_st = jnp.zeros((SSM_STATE, 128), f32)
        dec_row = jnp.zeros((1, 128), f32)
        for hh in range(2):
            col = d * SSM_HEADS + 2 * p + hh
            ac_c, ac_r = ac[:, col:col + 1], act[col:col + 1, :]
            dt_c, dt_r = dtf[:, col:col + 1], dtt[col:col + 1, :]
            tot_h = tot[:, col:col + 1]
            hm = (lane // 64) == hh
            m = cbg * jnp.exp(jnp.where(causal, ac_c - ac_r, NEG)) * dt_r
            xm = jnp.where(hm, xs_p, 0.0).astype(bf16)
            stm = jnp.where(hm, st, 0.0).astype(bf16)
            cwt = cg * jnp.exp(ac_c)
            lhs = jnp.concatenate([m, cwt], axis=1).astype(bf16)
            rhs = jnp.concatenate([xm, stm], axis=0)
            y_p = y_p + jnp.dot(lhs, rhs, preferred_element_type=f32)
            bw = (bg * (jnp.exp(tot_h - ac_c) * dt_c)).astype(bf16)
            new_st = new_st + lax.dot_general(bw, xm, (((0,), (0,)), ((), ())), preferred_element_type=f32)
            dec_row = jnp.where(hm, jnp.exp(tot_h), dec_row)
        st_ref[p] = st * dec_row + new_st
        if d == 0:
            y_ref[:, p * 128:(p + 1) * 128] = y_p
        else:
            o_ref[:, p * 128:(p + 1) * 128] = y_p

    if d == 0:
        y = y_ref[...] + yr_ref[...] + xs * dsk_ref[...]
        z = z_ref[...]
        y = y * (z * _sigmoid(z))
        ms = jnp.mean(y * y, axis=-1, keepdims=True)
        o_ref[...] = y * lax.rsqrt(ms + EPS) * ng_ref[...]


def ssd_mixer(proj3, rows, conv_w, conv_b, a_log, dt_bias, d_skip, norm_gain):
    b, t, _ = proj3.shape
    g = rows * GRID_W
    L = SSM_CHUNK
    assert g % L == 0
    nch = g // L
    pad128 = lambda v: jnp.pad(v.astype(f32).reshape(1, -1), ((0, 0), (0, 128 - 2 * SSM_HEADS)))
    alog, dtb = pad128(a_log), pad128(dt_bias)
    cw = conv_w.astype(f32)
    cb = conv_b.astype(f32).reshape(1, -1)
    dsk = jnp.repeat(d_skip.astype(f32), D_SSM // SSM_HEADS).reshape(1, -1)
    ng = norm_gain.astype(f32).reshape(1, -1)
    xcol, zcol, dcol = COL_XBC // D_SSM_CONV, COL_Z // D_SSM, COL_DT // 128
    hb = L // 8

    def run(d, extra_in, extra_specs, scratch):
        if d == 0:
            cidx = lambda j: jnp.where(j == 0, nch, j - 1)
        else:
            cidx = lambda j: jnp.where(j == nch, nch, nch - 1 - j)

        def pidx(j):
            ci = cidx(j)
            return jnp.where(ci == nch, 0, jnp.where(ci == 0, hb * nch + 1, hb * ci - 1))

        def nidx(j):
            ci = cidx(j)
            return jnp.where(ci >= nch - 1, 0, hb * (ci + 1))

        const = lambda shape: pl.BlockSpec(shape, lambda bi, j: (0,) * len(shape))
        in_specs = [pl.BlockSpec((None, L, D_SSM_CONV), lambda bi, j: (bi, cidx(j), xcol)),
                    pl.BlockSpec((None, 8, D_SSM_CONV), lambda bi, j: (bi, pidx(j), xcol)),
                    pl.BlockSpec((None, 8, D_SSM_CONV), lambda bi, j: (bi, nidx(j), xcol)),
                    pl.BlockSpec((None, L, 128), lambda bi, j: (bi, cidx(j), dcol)),
                    const((3, D_SSM_CONV)), const((1, D_SSM_CONV)), const((1, 128)), const((1, 128))]
        in_specs += extra_specs(cidx, const)
        return pl.pallas_call(
            functools.partial(_ssd_body, d=d, nchunks=nch),
            grid=(b, nch + 1),
            in_specs=in_specs,
            out_specs=pl.BlockSpec((None, L, D_SSM), lambda bi, j: (bi, cidx(j), 0)),
            out_shape=jax.ShapeDtypeStruct((b, t, D_SSM), f32),
            scratch_shapes=[pltpu.VMEM((SSM_HEADS // 2, SSM_STATE, 128), f32)] + scratch,
            compiler_params=_params("parallel", "arbitrary"),
            name="ssd_fwd" if d == 0 else "ssd_rev",
        )(proj3, proj3, proj3, proj3, cw, cb, alog, dtb, *extra_in)

    y_rev = run(1, (), lambda cidx, const: [], [])
    return run(
        0, (proj3, y_rev, dsk, ng),
        lambda cidx, const: [pl.BlockSpec((None, L, D_SSM), lambda bi, j: (bi, cidx(j), zcol)),
                             pl.BlockSpec((None, L, D_SSM), lambda bi, j: (bi, cidx(j), 0)),
                             const((1, D_SSM)), const((1, D_SSM))],
        [pltpu.VMEM((L, D_SSM), f32)])


def _merge_body(ya, yf, ys, g0, g1, g2, s_ref, wa, wf, ws, wo, o_ref):
    def branch(y, g, w):
        return _sigmoid(g[...]) * jnp.dot(y[...].astype(bf16), w[...], preferred_element_type=f32)

    merged = branch(ya, g0, wa) + branch(yf, g1, wf) + branch(ys, g2, ws)
    o_ref[...] = s_ref[...] + jnp.dot(merged.astype(bf16), wo[...], preferred_element_type=f32)


def merge_branches(s2d, proj2d, ya, yf, ys, wa, wf, ws, wo, tm=256):
    m, d = s2d.shape
    row = lambda w: pl.BlockSpec((tm, w), lambda i: (i, 0))
    gate = lambda k: pl.BlockSpec((tm, d), lambda i: (i, COL_GATE // d + k))
    wspec = lambda r: pl.BlockSpec((r, d), lambda i: (0, 0), pipeline_mode=pl.Buffered(1))
    return pl.pallas_call(
        _merge_body,
        grid=(_cdiv(m, tm),),
        in_specs=[row(D_ATT), row(D_FNET), row(D_SSM), gate(0), gate(1), gate(2), row(d),
                  wspec(D_ATT), wspec(D_FNET), wspec(D_SSM), wspec(d)],
        out_specs=row(d),
        out_shape=jax.ShapeDtypeStruct((m, d), f32),
        compiler_params=_params("parallel"),
        name="merge_branches",
    )(ya, yf, ys, proj2d, proj2d, proj2d, s2d, wa, wf, ws, wo)


def _router_body(s_ref, g_ref, wr_ref, tok_ref, afft_ref, aff_ref, *, m_total, tm):
    i = pl.program_id(0)
    x = s_ref[...]
    ms = jnp.mean(x * x, axis=-1, keepdims=True)
    tok = x * lax.rsqrt(ms + EPS) * g_ref[...]
    row = i * tm + lax.broadcasted_iota(i32, (tm, 1), 0)
    tok = jnp.where(row < m_total, tok, 0.0)
    tok_ref[...] = tok.astype(bf16)
    lg = jnp.dot(tok, wr_ref[...], preferred_element_type=f32, precision=HI)
    ex = jnp.exp(lg - jnp.max(lg, axis=1, keepdims=True))
    aff = ex / jnp.sum(ex, axis=1, keepdims=True)
    aff_ref[...] = aff
    afft_ref[...] = aff.T


def router(s2d, gain, w_router, tm=512):
    m, d = s2d.shape
    mp = _cdiv(m, ROUTE_PAD) * ROUTE_PAD
    last_blk = _cdiv(m, tm) - 1
    wr = w_router.astype(f32)
    return pl.pallas_call(
        functools.partial(_router_body, m_total=m, tm=tm),
        grid=(mp // tm,),
        in_specs=[pl.BlockSpec((tm, d), lambda i: (jnp.minimum(i, last_blk), 0)),
                  pl.BlockSpec((1, d), lambda i: (0, 0)),
                  pl.BlockSpec((d, N_EXPERTS), lambda i: (0, 0))],
        out_specs=[pl.BlockSpec((tm, d), lambda i: (i, 0)),
                   pl.BlockSpec((N_EXPERTS, tm), lambda i: (0, i)),
                   pl.BlockSpec((tm, N_EXPERTS), lambda i: (i, 0))],
        out_shape=[jax.ShapeDtypeStruct((mp, d), bf16),
                   jax.ShapeDtypeStruct((N_EXPERTS, mp), f32),
                   jax.ShapeDtypeStruct((mp, N_EXPERTS), f32)],
        compiler_params=_params("parallel"),
        name="moe_router",
    )(s2d, gain.astype(f32).reshape(1, d), wr)


def _exclusive_rank(x, nt):
    li = lax.broadcasted_iota(i32, (128, 128), 0)
    lj = lax.broadcasted_iota(i32, (128, 128), 1)
    lane_before = (li < lj).astype(bf16)
    ti = lax.broadcasted_iota(i32, (nt, nt), 0)
    tj = lax.broadcasted_iota(i32, (nt, nt), 1)
    row_before = (tj < ti).astype(bf16)
    xb = x.astype(bf16)
    within = jnp.dot(xb, lane_before, preferred_element_type=f32)
    before = jnp.sum(jnp.dot(row_before, xb, preferred_element_type=f32), axis=1, keepdims=True)
    return within + before


def _count(mask):
    return jnp.sum(jnp.sum(mask.astype(f32), axis=-1, keepdims=True), axis=-2, keepdims=True)


def _select_body(a_ref, sel_ref, *, cap, nt):
    bits = pltpu.bitcast(a_ref[...], i32)

    def step(i, pref):
        cand = pref | jnp.left_shift(jnp.int32(1), 30 - i)
        return jnp.where(_count(bits >= cand) >= cap, cand, pref)

    thr = lax.fori_loop(0, 31, step, jnp.zeros((N_EXPERTS, 1, 1), i32))
    for e in range(N_EXPERTS):
        be = bits[e]
        gt = be > thr[e]
        eq = be == thr[e]
        need = cap - _count(gt)
        take = jnp.logical_and(eq, _exclusive_rank(eq.astype(f32), nt) < need)
        sel_ref[e] = jnp.logical_or(gt, take).astype(f32)


def _rank_body(sel_ref, slot_ref, *, nt):
    for e in range(N_EXPERTS):
        sel = sel_ref[e]
        slot_ref[e] = jnp.where(sel > 0.5, _exclusive_rank(sel, nt), -1.0).astype(i32)


def select_tokens(aff3, cap):
    e, nt, _ = aff3.shape
    return pl.pallas_call(
        functools.partial(_select_body, cap=cap, nt=nt),
        out_shape=jax.ShapeDtypeStruct(aff3.shape, f32),
        compiler_params=pltpu.CompilerParams(vmem_limit_bytes=VMEM_LIMIT),
        name="moe_select",
    )(aff3)


def rank_tokens(sel3):
    e, nt, _ = sel3.shape
    return pl.pallas_call(
        functools.partial(_rank_body, nt=nt),
        out_shape=jax.ShapeDtypeStruct(sel3.shape, i32),
        compiler_params=pltpu.CompilerParams(vmem_limit_bytes=VMEM_LIMIT),
        name="moe_rank",
    )(sel3)


def _gather_body(i_ref, j_ref, f_ref, tok_ref, slot_ref, o_ref, acc_ref, *, ns):
    k = pl.program_id(0) * ns + pl.program_id(1)
    flags = f_ref[k]

    @pl.when((flags & FLAG_VALID) != 0)
    def _():
        want = lax.broadcasted_iota(i32, (SLOT_BLK, GATHER_TOK), 0) + j_ref[k] * SLOT_BLK
        onehot = jnp.where(slot_ref[...] == want, 1.0, 0.0).astype(bf16)
        rows = jnp.dot(onehot, tok_ref[...], preferred_element_type=f32)

        @pl.when((flags & FLAG_FIRST) != 0)
        def _():
            acc_ref[...] = rows

        @pl.when((flags & FLAG_FIRST) == 0)
        def _():
            acc_ref[...] += rows

    @pl.when((flags & FLAG_LAST) != 0)
    def _():
        o_ref[...] = acc_ref[...].astype(bf16)


def gather_tokens(tok, slot_row3, sched, cap_pad, ns):
    d = tok.shape[1]
    ii, jj, ff = sched
    return pl.pallas_call(
        functools.partial(_gather_body, ns=ns),
        grid_spec=pltpu.PrefetchScalarGridSpec(
            num_scalar_prefetch=3,
            grid=(N_EXPERTS, ns),
            in_specs=[pl.BlockSpec((GATHER_TOK, d), lambda e, s, ii, jj, ff: (ii[e * ns + s], 0)),
                      pl.BlockSpec((None, 1, GATHER_TOK), lambda e, s, ii, jj, ff: (e, 0, ii[e * ns + s]))],
            out_specs=pl.BlockSpec((None, SLOT_BLK, d), lambda e, s, ii, jj, ff: (e, jj[e * ns + s], 0)),
            scratch_shapes=[pltpu.VMEM((SLOT_BLK, d), f32)]),
        out_shape=jax.ShapeDtypeStruct((N_EXPERTS, cap_pad, d), bf16),
        compiler_params=_params("parallel", "arbitrary"),
        name="moe_gather",
    )(ii, jj, ff, tok, slot_row3)


def _ffn_body(x_ref, wg_ref, wu_ref, wd_ref, o_ref, acc_ref):
    fi = pl.program_id(2)
    x = x_ref[...]
    gt = jnp.dot(x, wg_ref[...].astype(bf16), preferred_element_type=f32)
    up = jnp.dot(x, wu_ref[...].astype(bf16), preferred_element_type=f32)
    hid = (gt * _sigmoid(gt) * up).astype(bf16)
    part = jnp.dot(hid, wd_ref[...].astype(bf16), preferred_element_type=f32)

    @pl.when(fi == 0)
    def _():
        acc_ref[...] = part

    @pl.when(fi > 0)
    def _():
        acc_ref[...] += part

    @pl.when(fi == pl.num_programs(2) - 1)
    def _():
        o_ref[...] = acc_ref[...].astype(bf16)


def expert_ffn(xe, w_gate, w_up, w_down, layer, tf=512, tm_max=768):
    e, cap_pad, d = xe.shape
    dff = w_gate.shape[-1]
    nm = 1
    while cap_pad // nm > tm_max or cap_pad % nm or (cap_pad // nm) % 8:
        nm += 1
    tm = cap_pad // nm
    return pl.pallas_call(
        _ffn_body,
        grid=(e, nm, dff // tf),
        in_specs=[pl.BlockSpec((None, tm, d), lambda ei, mi, fi: (ei, mi, 0)),
                  pl.BlockSpec((None, None, d, tf), lambda ei, mi, fi: (layer, ei, 0, fi)),
                  pl.BlockSpec((None, None, d, tf), lambda ei, mi, fi: (layer, ei, 0, fi)),
                  pl.BlockSpec((None, None, tf, d), lambda ei, mi, fi: (layer, ei, fi, 0))],
        out_specs=pl.BlockSpec((None, tm, d), lambda ei, mi, fi: (ei, mi, 0)),
        out_shape=jax.ShapeDtypeStruct((e, cap_pad, d), bf16),
        scratch_shapes=[pltpu.VMEM((tm, d), f32)],
        compiler_params=_params("parallel", "parallel", "arbitrary"),
        name="moe_expert_ffn",
    )(xe, w_gate, w_up, w_down)


def _combine_body(i_ref, e_ref, j_ref, f_ref, s_ref, ye_ref, slot_ref, aff_ref, o_ref):
    k = pl.program_id(0)
    flags = f_ref[k]

    @pl.when((flags & FLAG_FIRST) != 0)
    def _():
        o_ref[...] = s_ref[...]

    @pl.when((flags & FLAG_VALID) != 0)
    def _():
        lane = lax.broadcasted_iota(i32, (1, N_EXPERTS), 1)
        pick = lane == e_ref[k]
        slot_col = jnp.sum(jnp.where(pick, slot_ref[...], 0.0), axis=1, keepdims=True)
        gate_col = jnp.sum(jnp.where(pick, aff_ref[...], 0.0), axis=1, keepdims=True)
        want = (lax.broadcasted_iota(i32, (1, SLOT_BLK), 1) + j_ref[k] * SLOT_BLK).astype(f32)
        onehot = jnp.where(slot_col == want, 1.0, 0.0).astype(bf16)
        o_ref[...] += gate_col * jnp.dot(onehot, ye_ref[...], preferred_element_type=f32)


def combine_tokens(s2d, ye, slot_tok, aff, sched):
    m, d = s2d.shape
    ii, ee, jj, ff = sched
    nsteps = ii.shape[0]
    return pl.pallas_call(
        _combine_body,
        grid_spec=pltpu.PrefetchScalarGridSpec(
            num_scalar_prefetch=4,
            grid=(nsteps,),
            in_specs=[pl.BlockSpec((COMBINE_TOK, d), lambda k, ii, ee, jj, ff: (ii[k], 0)),
                      pl.BlockSpec((None, SLOT_BLK, d), lambda k, ii, ee, jj, ff: (ee[k], jj[k], 0)),
                      pl.BlockSpec((COMBINE_TOK, N_EXPERTS), lambda k, ii, ee, jj, ff: (ii[k], 0)),
                      pl.BlockSpec((COMBINE_TOK, N_EXPERTS), lambda k, ii, ee, jj, ff: (ii[k], 0))],
            out_specs=pl.BlockSpec((COMBINE_TOK, d), lambda k, ii, ee, jj, ff: (ii[k], 0))),
        out_shape=jax.ShapeDtypeStruct((m, d), f32),
        compiler_params=_params("arbitrary"),
        name="moe_combine",
    )(ii, ee, jj, ff, s2d, ye, slot_tok, aff)


def _pair_lists(slot, m, tok_size, n_slot_blk):
    e = slot.shape[0]
    n_tok_blk = _cdiv(m, tok_size)
    cnt = jnp.sum((slot[:, :n_tok_blk * tok_size] >= 0).reshape(e, n_tok_blk, tok_size), axis=-1).astype(i32)
    cum_in = jnp.cumsum(cnt, axis=1)
    cum_ex = cum_in - cnt
    jlo = cum_ex // SLOT_BLK
    jhi = (cum_in - 1) // SLOT_BLK
    npairs = jnp.where(cnt > 0, jhi - jlo + 1, 0)
    off_in = jnp.cumsum(npairs, axis=1)
    off_ex = off_in - npairs
    total = off_in[:, -1:]
    ns = n_tok_blk + n_slot_blk
    step = jnp.arange(ns, dtype=i32)[None, :]
    valid = step < total
    sc = jnp.minimum(step, total - 1)
    tok_blk = jnp.sum(off_in[:, None, :] <= sc[:, :, None], axis=-1).astype(i32)
    tok_blk = jnp.minimum(tok_blk, n_tok_blk - 1)
    slot_blk = (jnp.take_along_axis(jlo, tok_blk, axis=1)
                + sc - jnp.take_along_axis(off_ex, tok_blk, axis=1)).astype(i32)
    return tok_blk, slot_blk, valid


def _gather_schedule(slot, m, n_slot_blk):
    tok_blk, slot_blk, valid = _pair_lists(slot, m, GATHER_TOK, n_slot_blk)
    e, ns = tok_blk.shape
    change = slot_blk[:, 1:] != slot_blk[:, :-1]
    first = jnp.concatenate([jnp.ones((e, 1), bool), change], axis=1)
    last = jnp.concatenate([change | ~valid[:, 1:], jnp.ones((e, 1), bool)], axis=1) & valid
    flags = first * FLAG_FIRST + valid * FLAG_VALID + last * FLAG_LAST
    return tuple(a.reshape(-1).astype(i32) for a in (tok_blk, slot_blk, flags)), ns


def _combine_schedule(slot, m, n_slot_blk):
    tok_blk, slot_blk, valid = _pair_lists(slot, m, COMBINE_TOK, n_slot_blk)
    e, ns = tok_blk.shape
    n_tok_blk = _cdiv(m, COMBINE_TOK)
    stride = e * (n_slot_blk + 1) + 1
    eid = jnp.broadcast_to(jnp.arange(e, dtype=i32)[:, None], (e, ns))
    key_pairs = jnp.where(valid, tok_blk * stride + 1 + eid * (n_slot_blk + 1) + slot_blk, jnp.int32(2 ** 30))
    init_tok = jnp.arange(n_tok_blk, dtype=i32)
    key = jnp.concatenate([init_tok * stride, key_pairs.reshape(-1)])
    c_tok = jnp.concatenate([init_tok, tok_blk.reshape(-1)])
    c_exp = jnp.concatenate([jnp.zeros_like(init_tok), eid.reshape(-1)])
    c_slot = jnp.concatenate([jnp.zeros_like(init_tok), slot_blk.reshape(-1)])
    c_valid = jnp.concatenate([jnp.zeros_like(init_tok), valid.reshape(-1).astype(i32)])
    n_all = key.shape[0]
    n_real = n_tok_blk + jnp.sum(valid)
    order = jnp.argsort(key)[jnp.minimum(jnp.arange(n_all), n_real - 1)]
    c_tok, c_exp, c_slot = c_tok[order], c_exp[order], c_slot[order]
    c_valid = jnp.where(jnp.arange(n_all) < n_real, c_valid[order], 0)
    c_first = jnp.concatenate([jnp.ones((1,), i32), (c_tok[1:] != c_tok[:-1]).astype(i32)])
    flags = c_first * FLAG_FIRST + c_valid * FLAG_VALID
    return tuple(a.astype(i32) for a in (c_tok, c_exp, c_slot, flags))


def expert_choice_ffn(s3, gain, w_router, w_gate, w_up, w_down, layer):
    b, t, d = s3.shape
    m = b * t
    cap = (EC_CAPACITY * m) // N_EXPERTS
    s2d = s3.reshape(m, d)
    tok, afft, aff = router(s2d, gain, w_router)

    n_pad = _cdiv(m, ROUTE_PAD) * ROUTE_PAD
    nt = n_pad // 128
    a = jnp.roll(afft[:, :m].reshape(N_EXPERTS, b, t), N_META, axis=2).reshape(N_EXPERTS, m)
    a = jnp.pad(a, ((0, 0), (0, n_pad - m)), constant_values=-1.0)
    sel = select_tokens(a.reshape(N_EXPERTS, nt, 128), cap).reshape(N_EXPERTS, n_pad)
    sel = jnp.roll(sel[:, :m].reshape(N_EXPERTS, b, t), -N_META, axis=2).reshape(N_EXPERTS, m)
    sel = jnp.pad(sel, ((0, 0), (0, n_pad - m)))
    slot = rank_tokens(sel.reshape(N_EXPERTS, nt, 128)).reshape(N_EXPERTS, n_pad)

    n_slot_blk = _cdiv(cap, SLOT_BLK)
    cap_pad = n_slot_blk * SLOT_BLK
    g_sched, ns = _gather_schedule(slot, m, n_slot_blk)
    xe = gather_tokens(tok, slot.reshape(N_EXPERTS, 1, n_pad), g_sched, cap_pad, ns)
    ye = expert_ffn(xe, w_gate, w_up, w_down, layer)
    out = combine_tokens(s2d, ye, slot.T.astype(f32), aff, _combine_schedule(slot, m, n_slot_blk))
    return out.reshape(b, t, d)


def _final_norm_body(x_ref, g_ref, o_ref):
    x = x_ref[...]
    ms = jnp.mean(x * x, axis=-1, keepdims=True)
    o_ref[...] = x * lax.rsqrt(ms + EPS) * g_ref[...]


def final_norm(s3, gain, g, tm=512):
    b, t, d = s3.shape
    assert g % tm == 0
    return pl.pallas_call(
        _final_norm_body,
        grid=(b, g // tm),
        in_specs=[pl.BlockSpec((None, tm, d), lambda bi, i: (bi, i, 0)),
                  pl.BlockSpec((1, d), lambda bi, i: (0, 0))],
        out_specs=pl.BlockSpec((None, tm, d), lambda bi, i: (bi, i, 0)),
        out_shape=jax.ShapeDtypeStruct((b, g, d), f32),
        compiler_params=_params("parallel", "parallel"),
        name="final_norm",
    )(s3, gain.astype(f32).reshape(1, d))


def _reorder_w_in(w):
    cuts = np.cumsum([3 * D_ATT, D_FNET, D_SSM_CONV, D_SSM, 2 * SSM_HEADS])
    qkv, u_f, xbc, z, dt, gate = jnp.split(w, [int(c) for c in cuts], axis=1)
    dt = jnp.pad(dt, ((0, 0), (0, N_PROJ - COL_DT - 2 * SSM_HEADS)))
    return jnp.concatenate([gate, qkv, u_f, xbc, z, dt], axis=1).astype(bf16)


def encoder_layer(s3, rows, layer, lw):
    b, t, d = s3.shape
    m = b * t
    proj = norm_matmul(s3.reshape(m, d), lw["norm1_gain"], lw["w_in"])
    proj3 = proj.reshape(b, t, N_PROJ)
    y_att = neighbourhood_attention(proj3, lw["bias_tab"], lw["meta_bias"], rows)
    y_f = fourier_mix(proj3, COL_F, N_META)
    y_s = ssd_mixer(proj3, rows, lw["conv_w"], lw["conv_b"], lw["a_log"], lw["dt_bias"], lw["d_skip"],
                    lw["ssd_norm_gain"])
    s2d = merge_branches(s3.reshape(m, d), proj, y_att.reshape(m, -1), y_f.reshape(m, -1), y_s.reshape(m, -1),
                         lw["w_branch_a"], lw["w_branch_f"], lw["w_branch_s"], lw["w_out"])
    return expert_choice_ffn(s2d.reshape(b, t, d), lw["norm2_gain"], lw["w_router"],
                             lw["w_exp_gate"], lw["w_exp_up"], lw["w_exp_down"], layer)


def encode(x, meta_tokens, final_gain, layers):
    b, g, d = x.shape
    rows = g // GRID_W
    meta = jnp.broadcast_to(meta_tokens.astype(x.dtype)[None], (b, N_META, d))
    s = jnp.concatenate([x, meta], axis=1)
    for layer, lw in enumerate(layers):
        s = encoder_layer(s, rows, layer, lw)
    return final_norm(s, final_gain, g)


def kernel(x_prompt, x_sample, meta_tokens, norm1_gain, w_in, rel_bias, meta_bias, conv_w, conv_b, a_log, dt_bias,
           d_skip, ssd_norm_gain, w_branch_a, w_branch_f, w_branch_s, w_out, norm2_gain, w_router, w_exp_gate,
           w_exp_up, w_exp_down, final_gain):
    depth = w_in.shape[0]
    rows_set = {x_prompt.shape[1] // GRID_W, x_sample.shape[1] // GRID_W}
    layers = []
    for l in range(depth):
        rb = rel_bias[l]
        layers.append({
            "norm1_gain": norm1_gain[l].astype(f32), "w_in": _reorder_w_in(w_in[l]),
            "rel_bias": rb, "meta_bias": meta_bias[l],
            "conv_w": conv_w[l], "conv_b": conv_b[l], "a_log": a_log[l], "dt_bias": dt_bias[l],
            "d_skip": d_skip[l], "ssd_norm_gain": ssd_norm_gain[l],
            "w_branch_a": w_branch_a[l].astype(bf16), "w_branch_f": w_branch_f[l].astype(bf16),
            "w_branch_s": w_branch_s[l].astype(bf16), "w_out": w_out[l].astype(bf16),
            "norm2_gain": norm2_gain[l], "w_router": w_router[l],
            "w_exp_gate": w_exp_gate, "w_exp_up": w_exp_up, "w_exp_down": w_exp_down,
            "bias_tab": attention_bias_tables(rb, max(rows_set)),
        })
    y_prompt = encode(x_prompt, meta_tokens, final_gain, layers)
    y_sample = encode(x_sample, meta_tokens, final_gain, layers)
    return (y_prompt, y_sample)
```

```python
import functools

import numpy as np
import jax
import jax.numpy as jnp
from jax import lax
from jax.experimental import pallas as pl
from jax.experimental.pallas import tpu as pltpu

f32 = jnp.float32
bf16 = jnp.bfloat16
i32 = jnp.int32
HI = lax.Precision.HIGHEST

D_MODEL = 2048
N_META = 16
GRID_W = 64
ATT_HEADS = 16
ATT_HEAD_DIM = 64
D_ATT = 1024
WIN_R = 8
WIN_C = 16
FNET_GROUPS = 4
D_FNET = 1024
FNET_GROUP_DIM = 256
SSM_HEADS = 16
D_SSM = 1024
SSM_GROUPS = 4
SSM_STATE = 128
SSM_CHUNK = 128
D_SSM_CONV = 2048
N_EXPERTS = 16
EC_CAPACITY = 2
EPS = 1e-6

COL_GATE = 0
COL_QKV = 6144
COL_F = 9216
COL_XBC = 10240
COL_Z = 12288
COL_DT = 13312
N_PROJ = 13824

NEG = -1e30
VMEM_LIMIT = 56 * 1024 * 1024

SLOT_BLK = 256
GATHER_TOK = 512
COMBINE_TOK = 256
COMBINE_ROUNDS = COMBINE_TOK // SLOT_BLK + 1
NO_SLOT_BLK = -2
ROUTE_PAD = 1024
FLAG_FIRST, FLAG_VALID, FLAG_LAST = 1, 2, 4
ATT_RB = 4
ATT_KR = 12


def _cdiv(a, b):
    return -(-a // b)


def _params(*sem):
    return pltpu.CompilerParams(dimension_semantics=sem, vmem_limit_bytes=VMEM_LIMIT)


def _sigmoid(x):
    return 1.0 / (1.0 + jnp.exp(-x))


def _softplus(x):
    return jnp.maximum(x, 0.0) + jnp.log(1.0 + jnp.exp(-jnp.abs(x)))


def _norm_matmul_body(x_ref, g_ref, w_ref, o_ref, h_ref):
    @pl.when(pl.program_id(1) == 0)
    def _():
        x = x_ref[...]
        ms = jnp.mean(x * x, axis=-1, keepdims=True)
        h_ref[...] = (x * lax.rsqrt(ms + EPS) * g_ref[...]).astype(bf16)

    o_ref[...] = jnp.dot(h_ref[...], w_ref[...], preferred_element_type=f32)


def norm_matmul(x2d, gain, w, tm=1024, tn=1536):
    m, d = x2d.shape
    n = w.shape[1]
    assert n % tn == 0
    return pl.pallas_call(
        _norm_matmul_body,
        grid=(_cdiv(m, tm), n // tn),
        in_specs=[pl.BlockSpec((tm, d), lambda i, j: (i, 0)),
                  pl.BlockSpec((1, d), lambda i, j: (0, 0)),
                  pl.BlockSpec((d, tn), lambda i, j: (0, j))],
        out_specs=pl.BlockSpec((tm, tn), lambda i, j: (i, j)),
        out_shape=jax.ShapeDtypeStruct((m, n), f32),
        scratch_shapes=[pltpu.VMEM((tm, d), bf16)],
        compiler_params=_params("parallel", "arbitrary"),
        name="norm_matmul",
    )(x2d, gain.reshape(1, d), w)


def attention_bias_tables(rel_bias, rows):
    rel = rel_bias.astype(f32)
    nh = rel.shape[0]
    cols = []
    for c in range(GRID_W):
        cs = min(max(c - WIN_C // 2, 0), GRID_W - WIN_C)
        j0 = cs - c + (WIN_C - 1)
        cols.append(jnp.pad(rel[:, :, j0:j0 + WIN_C], ((0, 0), (0, 0), (cs, GRID_W - WIN_C - cs)),
                            constant_values=NEG))
    colbias = jnp.transpose(jnp.stack(cols, axis=2), (0, 2, 1, 3))
    lo = ATT_KR - WIN_R
    nd = 2 * WIN_R - 1 + 2 * lo
    flat = jnp.pad(colbias, ((0, 0), (0, 0), (lo, lo), (0, 0)), constant_values=NEG).reshape(nh, GRID_W, nd * GRID_W)
    lane_row = np.arange(ATT_KR * GRID_W) // GRID_W
    tabs = []
    for r0, k0 in ((0, 0), (ATT_RB, ATT_RB - WIN_R // 2), (rows - ATT_RB, rows - ATT_KR)):
        per_row = []
        for rq in range(ATT_RB):
            r = r0 + rq
            rs = min(max(r - WIN_R // 2, 0), rows - WIN_R)
            start = (k0 - r + (WIN_R - 1) + lo) * GRID_W
            seen = (lane_row >= rs - k0) & (lane_row < rs - k0 + WIN_R)
            per_row.append(jnp.where(seen, flat[:, :, start:start + ATT_KR * GRID_W], NEG))
        tabs.append(jnp.stack(per_row, axis=1).reshape(nh, ATT_RB * GRID_W, ATT_KR * GRID_W))
    return jnp.stack(tabs)


def _attn_body(q_ref, *refs, nblk, nk):
    k_refs, v_refs = refs[:nk], refs[nk:2 * nk]
    km_ref, vm_ref, bias_ref, mb_ref, o_ref = refs[2 * nk:]
    i = pl.program_id(1)
    lane = lax.broadcasted_iota(i32, (1, 128), 1)
    q = q_ref[...] * (ATT_HEAD_DIM ** -0.5)
    km = km_ref[...].astype(bf16)
    vm = vm_ref[...]

    @pl.when(i < nblk)
    def _():
        k = jnp.concatenate([r[...] for r in k_refs], axis=0).astype(bf16)
        v = jnp.concatenate([r[...] for r in v_refs], axis=0)
        acc = jnp.zeros(q.shape, f32)
        for h in range(2):
            hm = (lane // ATT_HEAD_DIM) == h
            qh = jnp.where(hm, q, 0.0).astype(bf16)
            s = lax.dot_general(qh, k, (((1,), (1,)), ((), ())), preferred_element_type=f32) + bias_ref[h]
            sm = lax.dot_general(qh, km, (((1,), (1,)), ((), ())), preferred_element_type=f32) + mb_ref[h:h + 1, :]
            mx = jnp.maximum(jnp.max(s, axis=1, keepdims=True), jnp.max(sm, axis=1, keepdims=True))
            p = jnp.exp(s - mx)
            pm = jnp.exp(sm - mx)
            den = jnp.sum(p, axis=1, keepdims=True) + jnp.sum(pm, axis=1, keepdims=True)
            vh = jnp.where(hm, v, 0.0).astype(bf16)
            vmh = jnp.where(hm, vm, 0.0).astype(bf16)
            o = (jnp.dot(p.astype(bf16), vh, preferred_element_type=f32)
                 + jnp.dot(pm.astype(bf16), vmh, preferred_element_type=f32))
            acc = acc + o / den
        o_ref[...] = acc

    @pl.when(i == nblk)
    def _():
        acc = jnp.zeros(q.shape, f32)
        for h in range(2):
            hm = (lane // ATT_HEAD_DIM) == h
            qh = jnp.where(hm, q, 0.0).astype(bf16)
            sm = lax.dot_general(qh, km, (((1,), (1,)), ((), ())), preferred_element_type=f32) + mb_ref[h:h + 1, :]
            mx = jnp.max(sm, axis=1, keepdims=True)
            pm = jnp.exp(sm - mx)
            den = jnp.sum(pm, axis=1, keepdims=True)
            vmh = jnp.where(hm, vm, 0.0).astype(bf16)
            acc = acc + jnp.dot(pm.astype(bf16), vmh, preferred_element_type=f32) / den
        o_ref[...] = acc


def neighbourhood_attention(proj3, bias_tab, meta_bias, rows):
    b, t, _ = proj3.shape
    g = rows * GRID_W
    assert rows % ATT_RB == 0 and rows >= ATT_KR + ATT_RB
    nblk = rows // ATT_RB
    tq = ATT_RB * GRID_W
    half = WIN_R // 2
    assert ATT_RB % half == 0 and ATT_KR % half == 0 and ATT_KR >= ATT_RB + WIN_R - 1
    tk = half * GRID_W
    nk = ATT_KR // half
    qc, kc, vc = COL_QKV // 128, (COL_QKV + D_ATT) // 128, (COL_QKV + 2 * D_ATT) // 128
    nkb = g // tk

    def kstart(i):
        return jnp.clip(i * (ATT_RB // half) - 1, 0, nkb - nk)

    def kspec(j, col):
        return pl.BlockSpec((None, tk, 128), lambda bi, i, p: (bi, kstart(i) + j, col + p))

    def variant(i):
        return jnp.where(i == 0, 0, jnp.where(i >= nblk - 1, 2, 1))

    in_specs = ([pl.BlockSpec((None, tq, 128), lambda bi, i, p: (bi, i, qc + p))]
                + [kspec(j, kc) for j in range(nk)]
                + [kspec(j, vc) for j in range(nk)]
                + [pl.BlockSpec((None, N_META, 128), lambda bi, i, p: (bi, g // N_META, kc + p)),
                   pl.BlockSpec((None, N_META, 128), lambda bi, i, p: (bi, g // N_META, vc + p)),
                   pl.BlockSpec((None, 2, tq, ATT_KR * GRID_W), lambda bi, i, p: (variant(i), p, 0, 0)),
                   pl.BlockSpec((None, 2, N_META), lambda bi, i, p: (p, 0, 0))])
    return pl.pallas_call(
        functools.partial(_attn_body, nblk=nblk, nk=nk),
        grid=(b, nblk + 1, ATT_HEADS // 2),
        in_specs=in_specs,
        out_specs=pl.BlockSpec((None, tq, 128), lambda bi, i, p: (bi, i, p)),
        out_shape=jax.ShapeDtypeStruct((b, t, D_ATT), f32),
        compiler_params=_params("parallel", "arbitrary", "arbitrary"),
        name="nbr_attention",
    )(proj3, *([proj3] * (2 * nk + 2)), bias_tab, meta_bias.astype(f32).reshape(ATT_HEADS // 2, 2, N_META))


def _fnet_factors(t):
    best = None
    for n1 in range(8, t, 8):
        if t % n1 == 0 and (best is None or n1 + t // n1 < best[0] + best[1]):
            best = (n1, t // n1)
    assert best is not None
    return best


def _cos_sin(num, den):
    ang = 2 * np.pi * (num % den).astype(np.float64) / den
    return np.cos(ang), np.sin(ang)


def _fnet_tables(t, shift):
    n1, n2 = _fnet_factors(t)
    n2p = _cdiv(n2, 8) * 8
    t1, t2 = np.arange(n1, dtype=np.int64), np.arange(n2, dtype=np.int64)
    k1, k2 = t1, t2
    c2, s2 = _cos_sin(np.outer(k2 + shift, t2), n2)
    fa = np.zeros((2 * n2p, n2), np.float64)
    fa[:n2], fa[n2p:n2p + n2] = c2, -s2
    tc, ts = _cos_sin(np.outer(t1 + shift, k2 + shift), t)
    c1, s1 = _cos_sin(np.outer(k1, t1 + shift), n1)
    fc = np.block([[c1, s1], [-s1, c1]])
    ch = np.arange(FNET_GROUP_DIM, dtype=np.int64)
    cc, sc = _cos_sin(np.outer(ch, ch), FNET_GROUP_DIM)
    scale = 1.0 / np.sqrt(t * FNET_GROUP_DIM)
    to = lambda a: jnp.asarray(a, f32)
    return (n1, n2, n2p, to(fa), to(tc[:, :, None]), to(ts[:, :, None]), to(fc), to(cc * scale), to(sc * scale))


def _dot_f32(a, b):
    return jnp.dot(a, b, preferred_element_type=f32, precision=HI)


def _fnet_a_body(u_ref, fa_ref, tc_ref, ts_ref, o_ref, *, n2, n2p):
    fa = fa_ref[...]
    for j in range(8):
        r = _dot_f32(fa, u_ref[:, j, :])
        re, im = r[:n2], r[n2p:n2p + n2]
        tc, ts = tc_ref[j], ts_ref[j]
        o_ref[0, j] = re * tc + im * ts
        o_ref[1, j] = im * tc - re * ts


def _fnet_c_body(p_ref, fc_ref, o_ref):
    o_ref[...] = _dot_f32(fc_ref[...], p_ref[...])


def _fnet_d_body(q_ref, cc_ref, sc_ref, o_ref):
    cc, sc = cc_ref[...], sc_ref[...]
    for g in range(FNET_GROUPS):
        sl = slice(g * FNET_GROUP_DIM, (g + 1) * FNET_GROUP_DIM)
        o_ref[:, sl] = _dot_f32(q_ref[0, :, sl], cc) + _dot_f32(q_ref[1, :, sl], sc)


def fourier_mix(x3, col0, shift):
    b, t, ncols = x3.shape
    d = D_FNET
    n1, n2, n2p, fa, tc, ts, fc, cc, sc = _fnet_tables(t, shift)
    cb = 512
    assert col0 % cb == 0
    const2 = lambda shape: pl.BlockSpec(shape, lambda *_: (0, 0))
    p = pl.pallas_call(
        functools.partial(_fnet_a_body, n2=n2, n2p=n2p),
        grid=(b, n1 // 8, d // cb),
        in_specs=[pl.BlockSpec((None, n2, 8, cb), lambda bi, i, c: (bi, 0, i, col0 // cb + c)),
                  const2((2 * n2p, n2)),
                  pl.BlockSpec((8, n2, 1), lambda bi, i, c: (i, 0, 0)),
                  pl.BlockSpec((8, n2, 1), lambda bi, i, c: (i, 0, 0))],
        out_specs=pl.BlockSpec((None, 2, 8, n2, cb), lambda bi, i, c: (bi, 0, i, 0, c)),
        out_shape=jax.ShapeDtypeStruct((b, 2, n1, n2, d), f32),
        compiler_params=_params("parallel", "parallel", "parallel"),
        name="fnet_stage_a",
    )(x3.reshape(b, n2, n1, ncols), fa, tc, ts)
    cw = 2048
    ncol = n2 * d
    q = pl.pallas_call(
        _fnet_c_body,
        grid=(b, _cdiv(ncol, cw)),
        in_specs=[pl.BlockSpec((None, 2 * n1, cw), lambda bi, c: (bi, 0, c)),
                  const2((2 * n1, 2 * n1))],
        out_specs=pl.BlockSpec((None, 2 * n1, cw), lambda bi, c: (bi, 0, c)),
        out_shape=jax.ShapeDtypeStruct((b, 2 * n1, ncol), f32),
        compiler_params=_params("parallel", "parallel"),
        name="fnet_stage_c",
    )(p.reshape(b, 2 * n1, ncol), fc)
    tm = 512
    gd = (FNET_GROUP_DIM, FNET_GROUP_DIM)
    return pl.pallas_call(
        _fnet_d_body,
        grid=(b, _cdiv(t, tm)),
        in_specs=[pl.BlockSpec((None, 2, tm, d), lambda bi, i: (bi, 0, i, 0)),
                  const2(gd), const2(gd)],
        out_specs=pl.BlockSpec((None, tm, d), lambda bi, i: (bi, i, 0)),
        out_shape=jax.ShapeDtypeStruct((b, t, d), f32),
        compiler_params=_params("parallel", "parallel"),
        name="fnet_stage_d",
    )(q.reshape(b, 2, t, d), cc, sc)


def _ssd_body(*refs, d, nchunks):
    if d == 0:
        (xbc_ref, prev_ref, next_ref, dt_ref, cw_ref, cb_ref, alog_ref, dtb_ref,
         z_ref, yr_ref, dsk_ref, ng_ref, o_ref, st_ref, y_ref) = refs
    else:
        (xbc_ref, prev_ref, next_ref, dt_ref, cw_ref, cb_ref, alog_ref, dtb_ref,
         o_ref, st_ref) = refs
    L = SSM_CHUNK
    j = pl.program_id(1)
    if d == 0:
        ci = jnp.where(j == 0, nchunks, j - 1)
    else:
        ci = jnp.where(j == nchunks, nchunks, nchunks - 1 - j)
    is_meta = ci == nchunks

    @pl.when(j == 0)
    def _():
        st_ref[...] = jnp.zeros(st_ref.shape, f32)

    row = lax.broadcasted_iota(i32, (L, 1), 0)
    nvalid = jnp.where(is_meta, N_META, L)
    valid = row < nvalid

    x = jnp.where(valid, xbc_ref[...], 0.0)
    prev = jnp.where(is_meta, 0.0, prev_ref[7:8, :])
    nxt = jnp.where(ci == nchunks - 1, 0.0, next_ref[0:1, :])
    xp = jnp.where(row == 0, prev, pltpu.roll(x, 1, axis=0))
    xn = jnp.where(row == nvalid - 1, nxt, pltpu.roll(x, L - 1, axis=0))
    w = cw_ref[...]
    pre = w[0:1] * xp + w[1:2] * x + w[2:3] * xn + cb_ref[...]
    xc = jnp.where(valid, pre * _sigmoid(pre), 0.0)
    xs = xc[:, :D_SSM]
    bm = xc[:, D_SSM:D_SSM + SSM_GROUPS * SSM_STATE]
    cm = xc[:, D_SSM + SSM_GROUPS * SSM_STATE:]

    dtf = jnp.where(valid, _softplus(dt_ref[...] + dtb_ref[...]), 0.0)
    da = dtf * (-jnp.exp(alog_ref[...]))
    li = lax.broadcasted_iota(i32, (L, L), 0)
    si = lax.broadcasted_iota(i32, (L, L), 1)
    causal = (si <= li) if d == 0 else (si >= li)
    ac = jnp.dot(causal.astype(f32), da, preferred_element_type=f32, precision=HI)
    act = ac.T
    dtt = dtf.T
    tot = jnp.sum(da, axis=0, keepdims=True)

    lane = lax.broadcasted_iota(i32, (1, 128), 1)
    cbs = {}
    for p in range(SSM_HEADS // 2):
        g = (2 * p) // (SSM_HEADS // SSM_GROUPS)
        bg = bm[:, g * SSM_STATE:(g + 1) * SSM_STATE]
        cg = cm[:, g * SSM_STATE:(g + 1) * SSM_STATE]
        if g not in cbs:
            cbs[g] = lax.dot_general(cg.astype(bf16), bg.astype(bf16), (((1,), (1,)), ((), ())),
                                     preferred_element_type=f32)
        cbg = cbs[g]
        xs_p = xs[:, p * 128:(p + 1) * 128]
        st = st_ref[p]
        y_p = jnp.zeros((L, 128), f32)
        new_st = jnp.zeros((SSM_STATE, 128), f32)
        dec_row = jnp.zeros((1, 128), f32)
        for hh in range(2):
            col = d * SSM_HEADS + 2 * p + hh
            ac_c, ac_r = ac[:, col:col + 1], act[col:col + 1, :]
            dt_c, dt_r = dtf[:, col:col + 1], dtt[col:col + 1, :]
            tot_h = tot[:, col:col + 1]
            hm = (lane // 64) == hh
            m = cbg * jnp.exp(jnp.where(causal, ac_c - ac_r, NEG)) * dt_r
            xm = jnp.where(hm, xs_p, 0.0).astype(bf16)
            stm = jnp.where(hm, st, 0.0).astype(bf16)
            cwt = cg * jnp.exp(ac_c)
            lhs = jnp.concatenate([m, cwt], axis=1).astype(bf16)
            rhs = jnp.concatenate([xm, stm], axis=0)
            y_p = y_p + jnp.dot(lhs, rhs, preferred_element_type=f32)
            bw = (bg * (jnp.exp(tot_h - ac_c) * dt_c)).astype(bf16)
            new_st = new_st + lax.dot_general(bw, xm, (((0,), (0,)), ((), ())), preferred_element_type=f32)
            dec_row = jnp.where(hm, jnp.exp(tot_h), dec_row)
        st_ref[p] = st * dec_row + new_st
        if d == 0:
            y_ref[:, p * 128:(p + 1) * 128] = y_p
        else:
            o_ref[:, p * 128:(p + 1) * 128] = y_p

    if d == 0:
        y = y_ref[...] + yr_ref[...] + xs * dsk_ref[...]
        z = z_ref[...]
        y = y * (z * _sigmoid(z))
        ms = jnp.mean(y * y, axis=-1, keepdims=True)
        o_ref[...] = y * lax.rsqrt(ms + EPS) * ng_ref[...]


def ssd_mixer(proj3, rows, conv_w, conv_b, a_log, dt_bias, d_skip, norm_gain):
    b, t, _ = proj3.shape
    g = rows * GRID_W
    L = SSM_CHUNK
    assert g % L == 0
    nch = g // L
    pad128 = lambda v: jnp.pad(v.astype(f32).reshape(1, -1), ((0, 0), (0, 128 - 2 * SSM_HEADS)))
    alog, dtb = pad128(a_log), pad128(dt_bias)
    cw = conv_w.astype(f32)
    cb = conv_b.astype(f32).reshape(1, -1)
    dsk = jnp.repeat(d_skip.astype(f32), D_SSM // SSM_HEADS).reshape(1, -1)
    ng = norm_gain.astype(f32).reshape(1, -1)
    xcol, zcol, dcol = COL_XBC // D_SSM_CONV, COL_Z // D_SSM, COL_DT // 128
    hb = L // 8

    def run(d, extra_in, extra_specs, scratch):
        if d == 0:
            cidx = lambda j: jnp.where(j == 0, nch, j - 1)
        else:
            cidx = lambda j: jnp.where(j == nch, nch, nch - 1 - j)

        def pidx(j):
            ci = cidx(j)
            return jnp.where(ci == nch, 0, jnp.where(ci == 0, hb * nch + 1, hb * ci - 1))

        def nidx(j):
            ci = cidx(j)
            return jnp.where(ci >= nch - 1, 0, hb * (ci + 1))

        const = lambda shape: pl.BlockSpec(shape, lambda bi, j: (0,) * len(shape))
        in_specs = [pl.BlockSpec((None, L, D_SSM_CONV), lambda bi, j: (bi, cidx(j), xcol)),
                    pl.BlockSpec((None, 8, D_SSM_CONV), lambda bi, j: (bi, pidx(j), xcol)),
                    pl.BlockSpec((None, 8, D_SSM_CONV), lambda bi, j: (bi, nidx(j), xcol)),
                    pl.BlockSpec((None, L, 128), lambda bi, j: (bi, cidx(j), dcol)),
                    const((3, D_SSM_CONV)), const((1, D_SSM_CONV)), const((1, 128)), const((1, 128))]
        in_specs += extra_specs(cidx, const)
        return pl.pallas_call(
            functools.partial(_ssd_body, d=d, nchunks=nch),
            grid=(b, nch + 1),
            in_specs=in_specs,
            out_specs=pl.BlockSpec((None, L, D_SSM), lambda bi, j: (bi, cidx(j), 0)),
            out_shape=jax.ShapeDtypeStruct((b, t, D_SSM), f32),
            scratch_shapes=[pltpu.VMEM((SSM_HEADS // 2, SSM_STATE, 128), f32)] + scratch,
            compiler_params=_params("parallel", "arbitrary"),
            name="ssd_fwd" if d == 0 else "ssd_rev",
        )(proj3, proj3, proj3, proj3, cw, cb, alog, dtb, *extra_in)

    y_rev = run(1, (), lambda cidx, const: [], [])
    return run(
        0, (proj3, y_rev, dsk, ng),
        lambda cidx, const: [pl.BlockSpec((None, L, D_SSM), lambda bi, j: (bi, cidx(j), zcol)),
                             pl.BlockSpec((None, L, D_SSM), lambda bi, j: (bi, cidx(j), 0)),
                             const((1, D_SSM)), const((1, D_SSM))],
        [pltpu.VMEM((L, D_SSM), f32)])


def _merge_body(ya, yf, ys, g0, g1, g2, s_ref, wa, wf, ws, wo, o_ref):
    def branch(y, g, w):
        return _sigmoid(g[...]) * jnp.dot(y[...].astype(bf16), w[...], preferred_element_type=f32)

    merged = branch(ya, g0, wa) + branch(yf, g1, wf) + branch(ys, g2, ws)
    o_ref[...] = s_ref[...] + jnp.dot(merged.astype(bf16), wo[...], preferred_element_type=f32)


def merge_branches(s2d, proj2d, ya, yf, ys, wa, wf, ws, wo, tm=256):
    m, d = s2d.shape
    row = lambda w: pl.BlockSpec((tm, w), lambda i: (i, 0))
    gate = lambda k: pl.BlockSpec((tm, d), lambda i: (i, COL_GATE // d + k))
    wspec = lambda r: pl.BlockSpec((r, d), lambda i: (0, 0), pipeline_mode=pl.Buffered(1))
    return pl.pallas_call(
        _merge_body,
        grid=(_cdiv(m, tm),),
        in_specs=[row(D_ATT), row(D_FNET), row(D_SSM), gate(0), gate(1), gate(2), row(d),
                  wspec(D_ATT), wspec(D_FNET), wspec(D_SSM), wspec(d)],
        out_specs=row(d),
        out_shape=jax.ShapeDtypeStruct((m, d), f32),
        compiler_params=_params("parallel"),
        name="merge_branches",
    )(ya, yf, ys, proj2d, proj2d, proj2d, s2d, wa, wf, ws, wo)


def _router_body(s_ref, g_ref, wr_ref, tokt_ref, afft_ref, *, m_total, tm):
    i = pl.program_id(0)
    x = s_ref[...]
    ms = jnp.mean(x * x, axis=-1, keepdims=True)
    tok = x * lax.rsqrt(ms + EPS) * g_ref[...]
    row = i * tm + lax.broadcasted_iota(i32, (tm, 1), 0)
    tok = jnp.where(row < m_total, tok, 0.0)
    tokt_ref[...] = tok.T.astype(bf16)
    lg = jnp.dot(tok, wr_ref[...], preferred_element_type=f32, precision=HI)
    ex = jnp.exp(lg - jnp.max(lg, axis=1, keepdims=True))
    afft_ref[...] = (ex / jnp.sum(ex, axis=1, keepdims=True)).T


def router(s2d, gain, w_router, tm=512):
    m, d = s2d.shape
    mp = _cdiv(m, ROUTE_PAD) * ROUTE_PAD
    last_blk = _cdiv(m, tm) - 1
    return pl.pallas_call(
        functools.partial(_router_body, m_total=m, tm=tm),
        grid=(mp // tm,),
        in_specs=[pl.BlockSpec((tm, d), lambda i: (jnp.minimum(i, last_blk), 0)),
                  pl.BlockSpec((1, d), lambda i: (0, 0)),
                  pl.BlockSpec((d, N_EXPERTS), lambda i: (0, 0))],
        out_specs=[pl.BlockSpec((d, tm), lambda i: (0, i)),
                   pl.BlockSpec((N_EXPERTS, tm), lambda i: (0, i))],
        out_shape=[jax.ShapeDtypeStruct((d, mp), bf16),
                   jax.ShapeDtypeStruct((N_EXPERTS, mp), f32)],
        compiler_params=_params("parallel"),
        name="moe_router",
    )(s2d, gain.astype(f32).reshape(1, d), w_router.astype(f32))


def _exclusive_rank(x, nt):
    li = lax.broadcasted_iota(i32, (128, 128), 0)
    lj = lax.broadcasted_iota(i32, (128, 128), 1)
    lane_before = (li < lj).astype(bf16)
    ti = lax.broadcasted_iota(i32, (nt, nt), 0)
    tj = lax.broadcasted_iota(i32, (nt, nt), 1)
    row_before = (tj < ti).astype(bf16)
    xb = x.astype(bf16)
    within = jnp.dot(xb, lane_before, preferred_element_type=f32)
    before = jnp.sum(jnp.dot(row_before, xb, preferred_element_type=f32), axis=1, keepdims=True)
    return within + before


def _count(mask):
    return jnp.sum(jnp.sum(mask.astype(f32), axis=-1, keepdims=True), axis=-2, keepdims=True)


def _select_body(a_ref, sel_ref, *, cap, nt):
    bits = pltpu.bitcast(a_ref[...], i32)

    def step(i, pref):
        cand = pref | jnp.left_shift(jnp.int32(1), 30 - i)
        return jnp.where(_count(bits >= cand) >= cap, cand, pref)

    thr = lax.fori_loop(0, 31, step, jnp.zeros((N_EXPERTS, 1, 1), i32))
    for e in range(N_EXPERTS):
        be = bits[e]
        gt = be > thr[e]
        eq = be == thr[e]
        need = cap - _count(gt)
        take = jnp.logical_and(eq, _exclusive_rank(eq.astype(f32), nt) < need)
        sel_ref[e] = jnp.logical_or(gt, take).astype(f32)


def _rank_body(sel_ref, slot_ref, *, nt):
    for e in range(N_EXPERTS):
        sel = sel_ref[e]
        slot_ref[e] = jnp.where(sel > 0.5, _exclusive_rank(sel, nt), -1.0).astype(i32)


def select_tokens(aff3, cap):
    e, nt, _ = aff3.shape
    return pl.pallas_call(
        functools.partial(_select_body, cap=cap, nt=nt),
        out_shape=jax.ShapeDtypeStruct(aff3.shape, f32),
        compiler_params=pltpu.CompilerParams(vmem_limit_bytes=VMEM_LIMIT),
        name="moe_select",
    )(aff3)


def rank_tokens(sel3):
    e, nt, _ = sel3.shape
    return pl.pallas_call(
        functools.partial(_rank_body, nt=nt),
        out_shape=jax.ShapeDtypeStruct(sel3.shape, i32),
        compiler_params=pltpu.CompilerParams(vmem_limit_bytes=VMEM_LIMIT),
        name="moe_rank",
    )(sel3)


GATE_ROWS = 8


def _gather_body(i_ref, j_ref, f_ref, tokt_ref, slot_ref, aff_ref, o_ref, g_ref, acc_ref, gacc_ref, *, ns):
    e = pl.program_id(0)
    k = e * ns + pl.program_id(1)
    flags = f_ref[k]

    @pl.when((flags & FLAG_VALID) != 0)
    def _():
        pick = lax.broadcasted_iota(i32, (1, N_EXPERTS), 1) == e
        slot_col = jnp.sum(jnp.where(pick, slot_ref[...], 0.0), axis=1, keepdims=True)
        want = (lax.broadcasted_iota(i32, (1, SLOT_BLK), 1) + j_ref[k] * SLOT_BLK).astype(f32)
        onehot = jnp.where(slot_col == want, 1.0, 0.0).astype(bf16)
        cols = jnp.dot(tokt_ref[...], onehot, preferred_element_type=f32)
        g = aff_ref[...]
        g0 = g.astype(bf16).astype(f32)
        g1 = (g - g0).astype(bf16).astype(f32)
        g2 = g - g0 - g1
        piece = lax.broadcasted_iota(i32, (GATE_ROWS, 1), 0)
        pieces = jnp.where(piece == 0, g0, jnp.where(piece == 1, g1, jnp.where(piece == 2, g2, 0.0)))
        gate = jnp.dot(pieces.astype(bf16), onehot, preferred_element_type=f32)

        @pl.when((flags & FLAG_FIRST) != 0)
        def _():
            acc_ref[...] = cols
            gacc_ref[...] = gate

        @pl.when((flags & FLAG_FIRST) == 0)
        def _():
            acc_ref[...] += cols
            gacc_ref[...] += gate

    @pl.when((flags & FLAG_LAST) != 0)
    def _():
        o_ref[...] = acc_ref[...].astype(bf16)
        g_ref[...] = gacc_ref[...]


def gather_tokens(tokt, slot_tok, afft3, sched, cap_pad, ns):
    d = tokt.shape[0]
    ii, jj, ff = sched
    return pl.pallas_call(
        functools.partial(_gather_body, ns=ns),
        grid_spec=pltpu.PrefetchScalarGridSpec(
            num_scalar_prefetch=3,
            grid=(N_EXPERTS, ns),
            in_specs=[pl.BlockSpec((d, GATHER_TOK), lambda e, s, ii, jj, ff: (0, ii[e * ns + s])),
                      pl.BlockSpec((GATHER_TOK, N_EXPERTS), lambda e, s, ii, jj, ff: (ii[e * ns + s], 0)),
                      pl.BlockSpec((None, 1, GATHER_TOK), lambda e, s, ii, jj, ff: (e, 0, ii[e * ns + s]))],
            out_specs=[pl.BlockSpec((None, d, SLOT_BLK), lambda e, s, ii, jj, ff: (e, 0, jj[e * ns + s])),
                       pl.BlockSpec((None, GATE_ROWS, SLOT_BLK), lambda e, s, ii, jj, ff: (e, 0, jj[e * ns + s]))],
            scratch_shapes=[pltpu.VMEM((d, SLOT_BLK), f32), pltpu.VMEM((GATE_ROWS, SLOT_BLK), f32)]),
        out_shape=[jax.ShapeDtypeStruct((N_EXPERTS, d, cap_pad), bf16),
                   jax.ShapeDtypeStruct((N_EXPERTS, GATE_ROWS, cap_pad), f32)],
        compiler_params=_params("parallel", "arbitrary"),
        name="moe_gather",
    )(ii, jj, ff, tokt, slot_tok, afft3)


def _ffn_body(xt_ref, g_ref, wg_ref, wu_ref, wd_ref, o_ref, x_ref, hid_ref, wdb_ref, *, tf):
    fi = pl.program_id(2)

    @pl.when(fi == 0)
    def _():
        x_ref[...] = xt_ref[...].T

    x = x_ref[...]
    gt = jnp.dot(x, wg_ref[...].astype(bf16), preferred_element_type=f32)
    up = jnp.dot(x, wu_ref[...].astype(bf16), preferred_element_type=f32)
    off = pl.multiple_of(fi * tf, tf)
    hid_ref[:, pl.ds(off, tf)] = (gt * _sigmoid(gt) * up).astype(bf16)
    wdb_ref[pl.ds(off, tf), :] = wd_ref[...].astype(bf16)

    @pl.when(fi == pl.num_programs(2) - 1)
    def _():
        ye = jnp.dot(hid_ref[...], wdb_ref[...], preferred_element_type=f32)
        gate = g_ref[0:1, :] + g_ref[1:2, :] + g_ref[2:3, :]
        o_ref[...] = (ye.T * gate).astype(bf16)


def expert_ffn(xet, gslot, w_gate, w_up, w_down, layer, tf=256, tm_max=768):
    e, d, cap_pad = xet.shape
    dff = w_gate.shape[-1]
    nm = 1
    while cap_pad // nm > tm_max or cap_pad % nm or (cap_pad // nm) % 128:
        nm += 1
    tm = cap_pad // nm
    return pl.pallas_call(
        functools.partial(_ffn_body, tf=tf),
        grid=(e, nm, dff // tf),
        in_specs=[pl.BlockSpec((None, d, tm), lambda ei, mi, fi: (ei, 0, mi)),
                  pl.BlockSpec((None, GATE_ROWS, tm), lambda ei, mi, fi: (ei, 0, mi)),
                  pl.BlockSpec((None, None, d, tf), lambda ei, mi, fi: (layer, ei, 0, fi)),
                  pl.BlockSpec((None, None, d, tf), lambda ei, mi, fi: (layer, ei, 0, fi)),
                  pl.BlockSpec((None, None, tf, d), lambda ei, mi, fi: (layer, ei, fi, 0))],
        out_specs=pl.BlockSpec((None, d, tm), lambda ei, mi, fi: (ei, 0, mi)),
        out_shape=jax.ShapeDtypeStruct((e, d, cap_pad), bf16),
        scratch_shapes=[pltpu.VMEM((tm, d), bf16), pltpu.VMEM((tm, dff), bf16), pltpu.VMEM((dff, d), bf16)],
        compiler_params=_params("parallel", "parallel", "arbitrary"),
        name="moe_expert_ffn",
    )(xet, gslot, w_gate, w_up, w_down)


def _combine_body(fetch_ref, match_ref, any_ref, s_ref, slot_ref, *rest):
    ye_refs, (o_ref, acc_ref) = rest[:N_EXPERTS], rest[N_EXPERTS:]
    r = pl.program_id(1)
    step = pl.program_id(0) * COMBINE_ROUNDS + r

    @pl.when(r == 0)
    def _():
        acc_ref[...] = jnp.zeros(acc_ref.shape, f32)

    @pl.when(any_ref[step] != 0)
    def _():
        row = lax.broadcasted_iota(i32, (SLOT_BLK, 1), 0)
        total = None
        for e in range(N_EXPERTS):
            want = row + match_ref[step * N_EXPERTS + e] * SLOT_BLK
            onehot = jnp.where(slot_ref[e] == want, 1.0, 0.0).astype(bf16)
            part = jnp.dot(ye_refs[e][...], onehot, preferred_element_type=f32)
            total = part if total is None else part + total
        acc_ref[...] += total

    @pl.when(r == COMBINE_ROUNDS - 1)
    def _():
        o_ref[...] = s_ref[...] + acc_ref[...].T


def combine_tokens(s2d, yet, slot_row3, sched):
    m, d = s2d.shape
    fetch, match, anyv = sched

    def ye_spec(e):
        return pl.BlockSpec((None, d, SLOT_BLK),
                            lambda i, r, fetch, match, anyv: (e, 0, fetch[(i * COMBINE_ROUNDS + r) * N_EXPERTS + e]))

    return pl.pallas_call(
        _combine_body,
        grid_spec=pltpu.PrefetchScalarGridSpec(
            num_scalar_prefetch=3,
            grid=(_cdiv(m, COMBINE_TOK), COMBINE_ROUNDS),
            in_specs=[pl.BlockSpec((COMBINE_TOK, d), lambda i, r, *_: (i, 0)),
                      pl.BlockSpec((N_EXPERTS, 1, COMBINE_TOK), lambda i, r, *_: (0, 0, i))]
                     + [ye_spec(e) for e in range(N_EXPERTS)],
            out_specs=pl.BlockSpec((COMBINE_TOK, d), lambda i, r, *_: (i, 0)),
            scratch_shapes=[pltpu.VMEM((d, COMBINE_TOK), f32)]),
        out_shape=jax.ShapeDtypeStruct((m, d), f32),
        compiler_params=_params("parallel", "arbitrary"),
        name="moe_combine",
    )(fetch, match, anyv, s2d, slot_row3, *([yet] * N_EXPERTS))


def _pair_lists(slot, m, tok_size, n_slot_blk):
    e = slot.shape[0]
    n_tok_blk = _cdiv(m, tok_size)
    cnt = jnp.sum((slot[:, :n_tok_blk * tok_size] >= 0).reshape(e, n_tok_blk, tok_size), axis=-1).astype(i32)
    cum_in = jnp.cumsum(cnt, axis=1)
    cum_ex = cum_in - cnt
    jlo = cum_ex // SLOT_BLK
    jhi = (cum_in - 1) // SLOT_BLK
    npairs = jnp.where(cnt > 0, jhi - jlo + 1, 0)
    off_in = jnp.cumsum(npairs, axis=1)
    off_ex = off_in - npairs
    total = off_in[:, -1:]
    ns = n_tok_blk + n_slot_blk
    step = jnp.arange(ns, dtype=i32)[None, :]
    valid = step < total
    sc = jnp.minimum(step, total - 1)
    tok_blk = jnp.sum(off_in[:, None, :] <= sc[:, :, None], axis=-1).astype(i32)
    tok_blk = jnp.minimum(tok_blk, n_tok_blk - 1)
    slot_blk = (jnp.take_along_axis(jlo, tok_blk, axis=1)
                + sc - jnp.take_along_axis(off_ex, tok_blk, axis=1)).astype(i32)
    return tok_blk, slot_blk, valid


def _gather_schedule(slot, m, n_slot_blk):
    tok_blk, slot_blk, valid = _pair_lists(slot, m, GATHER_TOK, n_slot_blk)
    e, ns = tok_blk.shape
    change = slot_blk[:, 1:] != slot_blk[:, :-1]
    first = jnp.concatenate([jnp.ones((e, 1), bool), change], axis=1)
    last = jnp.concatenate([change | ~valid[:, 1:], jnp.ones((e, 1), bool)], axis=1) & valid
    flags = first * FLAG_FIRST + valid * FLAG_VALID + last * FLAG_LAST
    return tuple(a.reshape(-1).astype(i32) for a in (tok_blk, slot_blk, flags)), ns


def _combine_schedule(slot, m):
    e = slot.shape[0]
    n_tok_blk = _cdiv(m, COMBINE_TOK)
    cnt = jnp.sum((slot[:, :n_tok_blk * COMBINE_TOK] >= 0).reshape(e, n_tok_blk, COMBINE_TOK), axis=-1).astype(i32)
    cum_in = jnp.cumsum(cnt, axis=1)
    jlo = ((cum_in - cnt) // SLOT_BLK)[:, :, None]
    jhi = (cum_in - 1) // SLOT_BLK
    has = (cnt > 0)[:, :, None]
    npairs = jnp.where(has, jhi[:, :, None] - jlo + 1, 0)
    held = lax.cummax(jnp.where(cnt > 0, jhi, 0), axis=1)[:, :, None]
    rnd = jnp.arange(COMBINE_ROUNDS, dtype=i32)[None, None, :]
    valid = rnd < npairs
    fetch = jnp.where(has, jlo + jnp.minimum(rnd, npairs - 1), held)
    match = jnp.where(valid, jlo + rnd, NO_SLOT_BLK)
    flat = lambda a: jnp.transpose(a, (1, 2, 0)).reshape(-1).astype(i32)
    return flat(fetch), flat(match), jnp.any(valid, axis=0).reshape(-1).astype(i32)


def expert_choice_ffn(s3, gain, w_router, w_gate, w_up, w_down, layer):
    b, t, d = s3.shape
    m = b * t
    cap = (EC_CAPACITY * m) // N_EXPERTS
    s2d = s3.reshape(m, d)
    tokt, afft = router(s2d, gain, w_router)

    n_pad = _cdiv(m, ROUTE_PAD) * ROUTE_PAD
    nt = n_pad // 128
    a = jnp.roll(afft[:, :m].reshape(N_EXPERTS, b, t), N_META, axis=2).reshape(N_EXPERTS, m)
    a = jnp.pad(a, ((0, 0), (0, n_pad - m)), constant_values=-1.0)
    sel = select_tokens(a.reshape(N_EXPERTS, nt, 128), cap).reshape(N_EXPERTS, n_pad)
    sel = jnp.roll(sel[:, :m].reshape(N_EXPERTS, b, t), -N_META, axis=2).reshape(N_EXPERTS, m)
    sel = jnp.pad(sel, ((0, 0), (0, n_pad - m)))
    slot = rank_tokens(sel.reshape(N_EXPERTS, nt, 128)).reshape(N_EXPERTS, n_pad)

    n_slot_blk = _cdiv(cap, SLOT_BLK)
    cap_pad = n_slot_blk * SLOT_BLK
    g_sched, ns = _gather_schedule(slot, m, n_slot_blk)
    xet, gslot = gather_tokens(tokt, slot.T.astype(f32), afft.reshape(N_EXPERTS, 1, n_pad), g_sched, cap_pad, ns)
    yet = expert_ffn(xet, gslot, w_gate, w_up, w_down, layer)
    out = combine_tokens(s2d, yet, slot.reshape(N_EXPERTS, 1, n_pad), _combine_schedule(slot, m))
    return out.reshape(b, t, d)


def _final_norm_body(x_ref, g_ref, o_ref):
    x = x_ref[...]
    ms = jnp.mean(x * x, axis=-1, keepdims=True)
    o_ref[...] = x * lax.rsqrt(ms + EPS) * g_ref[...]


def final_norm(s3, gain, g, tm=512):
    b, t, d = s3.shape
    assert g % tm == 0
    return pl.pallas_call(
        _final_norm_body,
        grid=(b, g // tm),
        in_specs=[pl.BlockSpec((None, tm, d), lambda bi, i: (bi, i, 0)),
                  pl.BlockSpec((1, d), lambda bi, i: (0, 0))],
        out_specs=pl.BlockSpec((None, tm, d), lambda bi, i: (bi, i, 0)),
        out_shape=jax.ShapeDtypeStruct((b, g, d), f32),
        compiler_params=_params("parallel", "parallel"),
        name="final_norm",
    )(s3, gain.astype(f32).reshape(1, d))


def _reorder_w_in(w):
    cuts = np.cumsum([3 * D_ATT, D_FNET, D_SSM_CONV, D_SSM, 2 * SSM_HEADS])
    qkv, u_f, xbc, z, dt, gate = jnp.split(w, [int(c) for c in cuts], axis=1)
    dt = jnp.pad(dt, ((0, 0), (0, N_PROJ - COL_DT - 2 * SSM_HEADS)))
    return jnp.concatenate([gate, qkv, u_f, xbc, z, dt], axis=1).astype(bf16)


def encoder_layer(s3, rows, layer, lw):
    b, t, d = s3.shape
    m = b * t
    proj = norm_matmul(s3.reshape(m, d), lw["norm1_gain"], lw["w_in"])
    proj3 = proj.reshape(b, t, N_PROJ)
    y_att = neighbourhood_attention(proj3, lw["bias_tab"], lw["meta_bias"], rows)
    y_f = fourier_mix(proj3, COL_F, N_META)
    y_s = ssd_mixer(proj3, rows, lw["conv_w"], lw["conv_b"], lw["a_log"], lw["dt_bias"], lw["d_skip"],
                    lw["ssd_norm_gain"])
    s2d = merge_branches(s3.reshape(m, d), proj, y_att.reshape(m, -1), y_f.reshape(m, -1), y_s.reshape(m, -1),
                         lw["w_branch_a"], lw["w_branch_f"], lw["w_branch_s"], lw["w_out"])
    return expert_choice_ffn(s2d.reshape(b, t, d), lw["norm2_gain"], lw["w_router"],
                             lw["w_exp_gate"], lw["w_exp_up"], lw["w_exp_down"], layer)


def encode(x, meta_tokens, final_gain, layers):
    b, g, d = x.shape
    rows = g // GRID_W
    meta = jnp.broadcast_to(meta_tokens.astype(x.dtype)[None], (b, N_META, d))
    s = jnp.concatenate([x, meta], axis=1)
    for layer, lw in enumerate(layers):
        s = encoder_layer(s, rows, layer, lw)
    return final_norm(s, final_gain, g)


def kernel(x_prompt, x_sample, meta_tokens, norm1_gain, w_in, rel_bias, meta_bias, conv_w, conv_b, a_log, dt_bias,
           d_skip, ssd_norm_gain, w_branch_a, w_branch_f, w_branch_s, w_out, norm2_gain, w_router, w_exp_gate,
           w_exp_up, w_exp_down, final_gain):
    depth = w_in.shape[0]
    rows_set = {x_prompt.shape[1] // GRID_W, x_sample.shape[1] // GRID_W}
    layers = []
    for l in range(depth):
        rb = rel_bias[l]
        layers.append({
            "norm1_gain": norm1_gain[l].astype(f32), "w_in": _reorder_w_in(w_in[l]),
            "rel_bias": rb, "meta_bias": meta_bias[l],
            "conv_w": conv_w[l], "conv_b": conv_b[l], "a_log": a_log[l], "dt_bias": dt_bias[l],
            "d_skip": d_skip[l], "ssd_norm_gain": ssd_norm_gain[l],
            "w_branch_a": w_branch_a[l].astype(bf16), "w_branch_f": w_branch_f[l].astype(bf16),
            "w_branch_s": w_branch_s[l].astype(bf16), "w_out": w_out[l].astype(bf16),
            "norm2_gain": norm2_gain[l], "w_router": w_router[l],
            "w_exp_gate": w_exp_gate, "w_exp_up": w_exp_up, "w_exp_down": w_exp_down,
            "bias_tab": attention_bias_tables(rb, max(rows_set)),
        })
    y_prompt = encode(x_prompt, meta_tokens, final_gain, layers)
    y_sample = encode(x_sample, meta_tokens, final_gain, layers)
    return (y_prompt, y_sample)
```

```python
import functools

import numpy as np
import jax
import jax.numpy as jnp
from jax import lax
from jax.experimental import pallas as pl
from jax.experimental.pallas import tpu as pltpu

f32 = jnp.float32
bf16 = jnp.bfloat16
i32 = jnp.int32
HI = lax.Precision.HIGHEST

D_MODEL = 2048
N_META = 16
GRID_W = 64
ATT_HEADS = 16
ATT_HEAD_DIM = 64
D_ATT = 1024
WIN_R = 8
WIN_C = 16
FNET_GROUPS = 4
D_FNET = 1024
FNET_GROUP_DIM = 256
SSM_HEADS = 16
D_SSM = 1024
SSM_GROUPS = 4
SSM_STATE = 128
SSM_CHUNK = 128
D_SSM_CONV = 2048
N_EXPERTS = 16
EC_CAPACITY = 2
EPS = 1e-6

COL_GATE = 0
COL_QKV = 6144
COL_F = 9216
COL_XBC = 10240
COL_Z = 12288
COL_DT = 13312
N_PROJ = 13824

NEG = -1e30
VMEM_LIMIT = 56 * 1024 * 1024

SLOT_BLK = 256
GATHER_TOK = 1024
COMBINE_TOK = 256
COMBINE_ROUNDS = COMBINE_TOK // SLOT_BLK + 1
NO_SLOT_BLK = -2
ROUTE_PAD = 1024
FLAG_FIRST, FLAG_VALID, FLAG_LAST = 1, 2, 4
ATT_RB = 8
ATT_KR = 16


def _cdiv(a, b):
    return -(-a // b)


def _params(*sem):
    return pltpu.CompilerParams(dimension_semantics=sem, vmem_limit_bytes=VMEM_LIMIT)


def _sigmoid(x):
    return 1.0 / (1.0 + jnp.exp(-x))


def _softplus(x):
    return jnp.maximum(x, 0.0) + jnp.log(1.0 + jnp.exp(-jnp.abs(x)))


def _norm_matmul_body(x_ref, g_ref, w_ref, o_ref, h_ref):
    @pl.when(pl.program_id(1) == 0)
    def _():
        x = x_ref[...]
        ms = jnp.mean(x * x, axis=-1, keepdims=True)
        h_ref[...] = (x * lax.rsqrt(ms + EPS) * g_ref[...]).astype(bf16)

    o_ref[...] = jnp.dot(h_ref[...], w_ref[...], preferred_element_type=f32)


def norm_matmul(x2d, gain, w, tm=1024, tn=1536):
    m, d = x2d.shape
    n = w.shape[1]
    assert n % tn == 0
    return pl.pallas_call(
        _norm_matmul_body,
        grid=(_cdiv(m, tm), n // tn),
        in_specs=[pl.BlockSpec((tm, d), lambda i, j: (i, 0)),
                  pl.BlockSpec((1, d), lambda i, j: (0, 0)),
                  pl.BlockSpec((d, tn), lambda i, j: (0, j))],
        out_specs=pl.BlockSpec((tm, tn), lambda i, j: (i, j)),
        out_shape=jax.ShapeDtypeStruct((m, n), f32),
        scratch_shapes=[pltpu.VMEM((tm, d), bf16)],
        compiler_params=_params("parallel", "arbitrary"),
        name="norm_matmul",
    )(x2d, gain.reshape(1, d), w)


def attention_bias_tables(rel_bias, rows):
    rel = rel_bias.astype(f32)
    nh = rel.shape[0]
    cols = []
    for c in range(GRID_W):
        cs = min(max(c - WIN_C // 2, 0), GRID_W - WIN_C)
        j0 = cs - c + (WIN_C - 1)
        cols.append(jnp.pad(rel[:, :, j0:j0 + WIN_C], ((0, 0), (0, 0), (cs, GRID_W - WIN_C - cs)),
                            constant_values=NEG))
    colbias = jnp.transpose(jnp.stack(cols, axis=2), (0, 2, 1, 3))
    lo = ATT_KR - WIN_R
    nd = 2 * WIN_R - 1 + 2 * lo
    flat = jnp.pad(colbias, ((0, 0), (0, 0), (lo, lo), (0, 0)), constant_values=NEG).reshape(nh, GRID_W, nd * GRID_W)
    lane_row = np.arange(ATT_KR * GRID_W) // GRID_W
    tabs = []
    for r0, k0 in ((0, 0), (ATT_RB, ATT_RB - WIN_R // 2), (rows - ATT_RB, rows - ATT_KR)):
        per_row = []
        for rq in range(ATT_RB):
            r = r0 + rq
            rs = min(max(r - WIN_R // 2, 0), rows - WIN_R)
            start = (k0 - r + (WIN_R - 1) + lo) * GRID_W
            seen = (lane_row >= rs - k0) & (lane_row < rs - k0 + WIN_R)
            per_row.append(jnp.where(seen, flat[:, :, start:start + ATT_KR * GRID_W], NEG))
        tabs.append(jnp.stack(per_row, axis=1).reshape(nh, ATT_RB * GRID_W, ATT_KR * GRID_W))
    return jnp.stack(tabs)


def _attn_body(q_ref, *refs, nblk, nk):
    k_refs, v_refs = refs[:nk], refs[nk:2 * nk]
    km_ref, vm_ref, bias_ref, mb_ref, o_ref = refs[2 * nk:]
    i = pl.program_id(1)
    lane = lax.broadcasted_iota(i32, (1, 128), 1)
    q = q_ref[...] * (ATT_HEAD_DIM ** -0.5)
    km = km_ref[...].astype(bf16)
    vm = vm_ref[...]

    @pl.when(i < nblk)
    def _():
        k = jnp.concatenate([r[...] for r in k_refs], axis=0).astype(bf16)
        v = jnp.concatenate([r[...] for r in v_refs], axis=0)
        acc = jnp.zeros(q.shape, f32)
        for h in range(2):
            hm = (lane // ATT_HEAD_DIM) == h
            qh = jnp.where(hm, q, 0.0).astype(bf16)
            s = lax.dot_general(qh, k, (((1,), (1,)), ((), ())), preferred_element_type=f32) + bias_ref[h]
            sm = lax.dot_general(qh, km, (((1,), (1,)), ((), ())), preferred_element_type=f32) + mb_ref[h:h + 1, :]
            mx = jnp.maximum(jnp.max(s, axis=1, keepdims=True), jnp.max(sm, axis=1, keepdims=True))
            p = jnp.exp(s - mx)
            pm = jnp.exp(sm - mx)
            den = jnp.sum(p, axis=1, keepdims=True) + jnp.sum(pm, axis=1, keepdims=True)
            vh = jnp.where(hm, v, 0.0).astype(bf16)
            vmh = jnp.where(hm, vm, 0.0).astype(bf16)
            o = (jnp.dot(p.astype(bf16), vh, preferred_element_type=f32)
                 + jnp.dot(pm.astype(bf16), vmh, preferred_element_type=f32))
            acc = acc + o / den
        o_ref[...] = acc

    @pl.when(i == nblk)
    def _():
        acc = jnp.zeros(q.shape, f32)
        for h in range(2):
            hm = (lane // ATT_HEAD_DIM) == h
            qh = jnp.where(hm, q, 0.0).astype(bf16)
            sm = lax.dot_general(qh, km, (((1,), (1,)), ((), ())), preferred_element_type=f32) + mb_ref[h:h + 1, :]
            mx = jnp.max(sm, axis=1, keepdims=True)
            pm = jnp.exp(sm - mx)
            den = jnp.sum(pm, axis=1, keepdims=True)
            vmh = jnp.where(hm, vm, 0.0).astype(bf16)
            acc = acc + jnp.dot(pm.astype(bf16), vmh, preferred_element_type=f32) / den
        o_ref[...] = acc


def neighbourhood_attention(proj3, bias_tab, meta_bias, rows):
    b, t, _ = proj3.shape
    g = rows * GRID_W
    assert rows % ATT_RB == 0 and rows >= ATT_KR + ATT_RB
    nblk = rows // ATT_RB
    tq = ATT_RB * GRID_W
    half = WIN_R // 2
    assert ATT_RB % half == 0 and ATT_KR % half == 0 and ATT_KR >= ATT_RB + WIN_R - 1
    tk = half * GRID_W
    nk = ATT_KR // half
    qc, kc, vc = COL_QKV // 128, (COL_QKV + D_ATT) // 128, (COL_QKV + 2 * D_ATT) // 128
    nkb = g // tk

    def kstart(i):
        return jnp.clip(i * (ATT_RB // half) - 1, 0, nkb - nk)

    def kspec(j, col):
        return pl.BlockSpec((None, tk, 128), lambda bi, i, p: (bi, kstart(i) + j, col + p))

    def variant(i):
        return jnp.where(i == 0, 0, jnp.where(i >= nblk - 1, 2, 1))

    in_specs = ([pl.BlockSpec((None, tq, 128), lambda bi, i, p: (bi, i, qc + p))]
                + [kspec(j, kc) for j in range(nk)]
                + [kspec(j, vc) for j in range(nk)]
                + [pl.BlockSpec((None, N_META, 128), lambda bi, i, p: (bi, g // N_META, kc + p)),
                   pl.BlockSpec((None, N_META, 128), lambda bi, i, p: (bi, g // N_META, vc + p)),
                   pl.BlockSpec((None, 2, tq, ATT_KR * GRID_W), lambda bi, i, p: (variant(i), p, 0, 0)),
                   pl.BlockSpec((None, 2, N_META), lambda bi, i, p: (p, 0, 0))])
    return pl.pallas_call(
        functools.partial(_attn_body, nblk=nblk, nk=nk),
        grid=(b, nblk + 1, ATT_HEADS // 2),
        in_specs=in_specs,
        out_specs=pl.BlockSpec((None, tq, 128), lambda bi, i, p: (bi, i, p)),
        out_shape=jax.ShapeDtypeStruct((b, t, D_ATT), f32),
        compiler_params=_params("parallel", "arbitrary", "arbitrary"),
        name="nbr_attention",
    )(proj3, *([proj3] * (2 * nk + 2)), bias_tab, meta_bias.astype(f32).reshape(ATT_HEADS // 2, 2, N_META))


def _fnet_factors(t):
    best = None
    for n1 in range(8, t, 8):
        if t % n1 == 0 and (best is None or n1 + t // n1 < best[0] + best[1]):
            best = (n1, t // n1)
    assert best is not None
    return best


def _cos_sin(num, den):
    ang = 2 * np.pi * (num % den).astype(np.float64) / den
    return np.cos(ang), np.sin(ang)


def _fnet_tables(t, shift):
    n1, n2 = _fnet_factors(t)
    n2p = _cdiv(n2, 8) * 8
    t1, t2 = np.arange(n1, dtype=np.int64), np.arange(n2, dtype=np.int64)
    k1, k2 = t1, t2
    c2, s2 = _cos_sin(np.outer(k2 + shift, t2), n2)
    fa = np.zeros((2 * n2p, n2), np.float64)
    fa[:n2], fa[n2p:n2p + n2] = c2, -s2
    tc, ts = _cos_sin(np.outer(t1 + shift, k2 + shift), t)
    c1, s1 = _cos_sin(np.outer(k1, t1 + shift), n1)
    fc = np.block([[c1, s1], [-s1, c1]])
    ch = np.arange(FNET_GROUP_DIM, dtype=np.int64)
    cc, sc = _cos_sin(np.outer(ch, ch), FNET_GROUP_DIM)
    scale = 1.0 / np.sqrt(t * FNET_GROUP_DIM)
    to = lambda a: jnp.asarray(a, f32)
    return (n1, n2, n2p, to(fa), to(tc[:, :, None]), to(ts[:, :, None]), to(fc), to(cc * scale), to(sc * scale))


def _dot_f32(a, b):
    return jnp.dot(a, b, preferred_element_type=f32, precision=HI)


def _fnet_a_body(u_ref, fa_ref, tc_ref, ts_ref, o_ref, *, n2, n2p):
    fa = fa_ref[...]
    for j in range(8):
        r = _dot_f32(fa, u_ref[:, j, :])
        re, im = r[:n2], r[n2p:n2p + n2]
        tc, ts = tc_ref[j], ts_ref[j]
        o_ref[0, j] = re * tc + im * ts
        o_ref[1, j] = im * tc - re * ts


def _fnet_c_body(p_ref, fc_ref, o_ref):
    o_ref[...] = _dot_f32(fc_ref[...], p_ref[...])


def _fnet_d_body(q_ref, cc_ref, sc_ref, o_ref):
    cc, sc = cc_ref[...], sc_ref[...]
    for g in range(FNET_GROUPS):
        sl = slice(g * FNET_GROUP_DIM, (g + 1) * FNET_GROUP_DIM)
        o_ref[:, sl] = _dot_f32(q_ref[0, :, sl], cc) + _dot_f32(q_ref[1, :, sl], sc)


def fourier_mix(x3, col0, shift):
    b, t, ncols = x3.shape
    d = D_FNET
    n1, n2, n2p, fa, tc, ts, fc, cc, sc = _fnet_tables(t, shift)
    cb = 512
    assert col0 % cb == 0
    const2 = lambda shape: pl.BlockSpec(shape, lambda *_: (0, 0))
    p = pl.pallas_call(
        functools.partial(_fnet_a_body, n2=n2, n2p=n2p),
        grid=(b, n1 // 8, d // cb),
        in_specs=[pl.BlockSpec((None, n2, 8, cb), lambda bi, i, c: (bi, 0, i, col0 // cb + c)),
                  const2((2 * n2p, n2)),
                  pl.BlockSpec((8, n2, 1), lambda bi, i, c: (i, 0, 0)),
                  pl.BlockSpec((8, n2, 1), lambda bi, i, c: (i, 0, 0))],
        out_specs=pl.BlockSpec((None, 2, 8, n2, cb), lambda bi, i, c: (bi, 0, i, 0, c)),
        out_shape=jax.ShapeDtypeStruct((b, 2, n1, n2, d), f32),
        compiler_params=_params("parallel", "parallel", "parallel"),
        name="fnet_stage_a",
    )(x3.reshape(b, n2, n1, ncols), fa, tc, ts)
    cw = 2048
    ncol = n2 * d
    q = pl.pallas_call(
        _fnet_c_body,
        grid=(b, _cdiv(ncol, cw)),
        in_specs=[pl.BlockSpec((None, 2 * n1, cw), lambda bi, c: (bi, 0, c)),
                  const2((2 * n1, 2 * n1))],
        out_specs=pl.BlockSpec((None, 2 * n1, cw), lambda bi, c: (bi, 0, c)),
        out_shape=jax.ShapeDtypeStruct((b, 2 * n1, ncol), f32),
        compiler_params=_params("parallel", "parallel"),
        name="fnet_stage_c",
    )(p.reshape(b, 2 * n1, ncol), fc)
    tm = 512
    gd = (FNET_GROUP_DIM, FNET_GROUP_DIM)
    return pl.pallas_call(
        _fnet_d_body,
        grid=(b, _cdiv(t, tm)),
        in_specs=[pl.BlockSpec((None, 2, tm, d), lambda bi, i: (bi, 0, i, 0)),
                  const2(gd), const2(gd)],
        out_specs=pl.BlockSpec((None, tm, d), lambda bi, i: (bi, i, 0)),
        out_shape=jax.ShapeDtypeStruct((b, t, d), f32),
        compiler_params=_params("parallel", "parallel"),
        name="fnet_stage_d",
    )(q.reshape(b, 2, t, d), cc, sc)


def _ssd_body(*refs, d, nchunks):
    if d == 0:
        (xbc_ref, prev_ref, next_ref, dt_ref, cw_ref, cb_ref, alog_ref, dtb_ref,
         z_ref, yr_ref, dsk_ref, ng_ref, o_ref, st_ref, y_ref) = refs
    else:
        (xbc_ref, prev_ref, next_ref, dt_ref, cw_ref, cb_ref, alog_ref, dtb_ref,
         o_ref, st_ref) = refs
    L = SSM_CHUNK
    j = pl.program_id(1)
    if d == 0:
        ci = jnp.where(j == 0, nchunks, j - 1)
    else:
        ci = jnp.where(j == nchunks, nchunks, nchunks - 1 - j)
    is_meta = ci == nchunks

    @pl.when(j == 0)
    def _():
        st_ref[...] = jnp.zeros(st_ref.shape, f32)

    row = lax.broadcasted_iota(i32, (L, 1), 0)
    nvalid = jnp.where(is_meta, N_META, L)
    valid = row < nvalid

    x = jnp.where(valid, xbc_ref[...], 0.0)
    prev = jnp.where(is_meta, 0.0, prev_ref[7:8, :])
    nxt = jnp.where(ci == nchunks - 1, 0.0, next_ref[0:1, :])
    xp = jnp.where(row == 0, prev, pltpu.roll(x, 1, axis=0))
    xn = jnp.where(row == nvalid - 1, nxt, pltpu.roll(x, L - 1, axis=0))
    w = cw_ref[...]
    pre = w[0:1] * xp + w[1:2] * x + w[2:3] * xn + cb_ref[...]
    xc = jnp.where(valid, pre * _sigmoid(pre), 0.0)
    xs = xc[:, :D_SSM]
    bm = xc[:, D_SSM:D_SSM + SSM_GROUPS * SSM_STATE]
    cm = xc[:, D_SSM + SSM_GROUPS * SSM_STATE:]

    dtf = jnp.where(valid, _softplus(dt_ref[...] + dtb_ref[...]), 0.0)
    da = dtf * (-jnp.exp(alog_ref[...]))
    li = lax.broadcasted_iota(i32, (L, L), 0)
    si = lax.broadcasted_iota(i32, (L, L), 1)
    causal = (si <= li) if d == 0 else (si >= li)
    ac = jnp.dot(causal.astype(f32), da, preferred_element_type=f32, precision=HI)
    act = ac.T
    dtt = dtf.T
    tot = jnp.sum(da, axis=0, keepdims=True)

    lane = lax.broadcasted_iota(i32, (1, 128), 1)
    cbs = {}
    for p in range(SSM_HEADS // 2):
        g = (2 * p) // (SSM_HEADS // SSM_GROUPS)
        bg = bm[:, g * SSM_STATE:(g + 1) * SSM_STATE]
        cg = cm[:, g * SSM_STATE:(g + 1) * SSM_STATE]
        if g not in cbs:
            cbs[g] = lax.dot_general(cg.astype(bf16), bg.astype(bf16), (((1,), (1,)), ((), ())),
                                     preferred_element_type=f32)
        cbg = cbs[g]
        xs_p = xs[:, p * 128:(p + 1) * 128]
        st = st_ref[p]
        y_p = jnp.zeros((L, 128), f32)
        new_st = jnp.zeros((SSM_STATE, 128), f32)
        dec_row = jnp.zeros((1, 128), f32)
        for hh in range(2):
            col = d * SSM_HEADS + 2 * p + hh
            ac_c, ac_r = ac[:, col:col + 1], act[col:col + 1, :]
            dt_c, dt_r = dtf[:, col:col + 1], dtt[col:col + 1, :]
            tot_h = tot[:, col:col + 1]
            hm = (lane // 64) == hh
            m = cbg * jnp.exp(jnp.where(causal, ac_c - ac_r, NEG)) * dt_r
            xm = jnp.where(hm, xs_p, 0.0).astype(bf16)
            stm = jnp.where(hm, st, 0.0).astype(bf16)
            cwt = cg * jnp.exp(ac_c)
            lhs = jnp.concatenate([m, cwt], axis=1).astype(bf16)
            rhs = jnp.concatenate([xm, stm], axis=0)
            y_p = y_p + jnp.dot(lhs, rhs, preferred_element_type=f32)
            bw = (bg * (jnp.exp(tot_h - ac_c) * dt_c)).astype(bf16)
            new_st = new_st + lax.dot_general(bw, xm, (((0,), (0,)), ((), ())), preferred_element_type=f32)
            dec_row = jnp.where(hm, jnp.exp(tot_h), dec_row)
        st_ref[p] = st * dec_row + new_st
        if d == 0:
            y_ref[:, p * 128:(p + 1) * 128] = y_p
        else:
            o_ref[:, p * 128:(p + 1) * 128] = y_p

    if d == 0:
        y = y_ref[...] + yr_ref[...] + xs * dsk_ref[...]
        z = z_ref[...]
        y = y * (z * _sigmoid(z))
        ms = jnp.mean(y * y, axis=-1, keepdims=True)
        o_ref[...] = y * lax.rsqrt(ms + EPS) * ng_ref[...]


def ssd_mixer(proj3, rows, conv_w, conv_b, a_log, dt_bias, d_skip, norm_gain):
    b, t, _ = proj3.shape
    g = rows * GRID_W
    L = SSM_CHUNK
    assert g % L == 0
    nch = g // L
    pad128 = lambda v: jnp.pad(v.astype(f32).reshape(1, -1), ((0, 0), (0, 128 - 2 * SSM_HEADS)))
    alog, dtb = pad128(a_log), pad128(dt_bias)
    cw = conv_w.astype(f32)
    cb = conv_b.astype(f32).reshape(1, -1)
    dsk = jnp.repeat(d_skip.astype(f32), D_SSM // SSM_HEADS).reshape(1, -1)
    ng = norm_gain.astype(f32).reshape(1, -1)
    xcol, zcol, dcol = COL_XBC // D_SSM_CONV, COL_Z // D_SSM, COL_DT // 128
    hb = L // 8

    def run(d, extra_in, extra_specs, scratch):
        if d == 0:
            cidx = lambda j: jnp.where(j == 0, nch, j - 1)
        else:
            cidx = lambda j: jnp.where(j == nch, nch, nch - 1 - j)

        def pidx(j):
            ci = cidx(j)
            return jnp.where(ci == nch, 0, jnp.where(ci == 0, hb * nch + 1, hb * ci - 1))

        def nidx(j):
            ci = cidx(j)
            return jnp.where(ci >= nch - 1, 0, hb * (ci + 1))

        const = lambda shape: pl.BlockSpec(shape, lambda bi, j: (0,) * len(shape))
        in_specs = [pl.BlockSpec((None, L, D_SSM_CONV), lambda bi, j: (bi, cidx(j), xcol)),
                    pl.BlockSpec((None, 8, D_SSM_CONV), lambda bi, j: (bi, pidx(j), xcol)),
                    pl.BlockSpec((None, 8, D_SSM_CONV), lambda bi, j: (bi, nidx(j), xcol)),
                    pl.BlockSpec((None, L, 128), lambda bi, j: (bi, cidx(j), dcol)),
                    const((3, D_SSM_CONV)), const((1, D_SSM_CONV)), const((1, 128)), const((1, 128))]
        in_specs += extra_specs(cidx, const)
        return pl.pallas_call(
            functools.partial(_ssd_body, d=d, nchunks=nch),
            grid=(b, nch + 1),
            in_specs=in_specs,
            out_specs=pl.BlockSpec((None, L, D_SSM), lambda bi, j: (bi, cidx(j), 0)),
            out_shape=jax.ShapeDtypeStruct((b, t, D_SSM), f32),
            scratch_shapes=[pltpu.VMEM((SSM_HEADS // 2, SSM_STATE, 128), f32)] + scratch,
            compiler_params=_params("parallel", "arbitrary"),
            name="ssd_fwd" if d == 0 else "ssd_rev",
        )(proj3, proj3, proj3, proj3, cw, cb, alog, dtb, *extra_in)

    y_rev = run(1, (), lambda cidx, const: [], [])
    return run(
        0, (proj3, y_rev, dsk, ng),
        lambda cidx, const: [pl.BlockSpec((None, L, D_SSM), lambda bi, j: (bi, cidx(j), zcol)),
                             pl.BlockSpec((None, L, D_SSM), lambda bi, j: (bi, cidx(j), 0)),
                             const((1, D_SSM)), const((1, D_SSM))],
        [pltpu.VMEM((L, D_SSM), f32)])


def _merge_body(ya, yf, ys, g0, g1, g2, s_ref, wa, wf, ws, wo, o_ref):
    def branch(y, g, w):
        return _sigmoid(g[...]) * jnp.dot(y[...].astype(bf16), w[...], preferred_element_type=f32)

    merged = branch(ya, g0, wa) + branch(yf, g1, wf) + branch(ys, g2, ws)
    o_ref[...] = s_ref[...] + jnp.dot(merged.astype(bf16), wo[...], preferred_element_type=f32)


def merge_branches(s2d, proj2d, ya, yf, ys, wa, wf, ws, wo, tm=256):
    m, d = s2d.shape
    row = lambda w: pl.BlockSpec((tm, w), lambda i: (i, 0))
    gate = lambda k: pl.BlockSpec((tm, d), lambda i: (i, COL_GATE // d + k))
    wspec = lambda r: pl.BlockSpec((r, d), lambda i: (0, 0), pipeline_mode=pl.Buffered(1))
    return pl.pallas_call(
        _merge_body,
        grid=(_cdiv(m, tm),),
        in_specs=[row(D_ATT), row(D_FNET), row(D_SSM), gate(0), gate(1), gate(2), row(d),
                  wspec(D_ATT), wspec(D_FNET), wspec(D_SSM), wspec(d)],
        out_specs=row(d),
        out_shape=jax.ShapeDtypeStruct((m, d), f32),
        compiler_params=_params("parallel"),
        name="merge_branches",
    )(ya, yf, ys, proj2d, proj2d, proj2d, s2d, wa, wf, ws, wo)


def _router_body(s_ref, g_ref, wr_ref, tok_ref, aff_ref, afft_ref, *, m_total, tm):
    i = pl.program_id(0)
    x = s_ref[...]
    ms = jnp.mean(x * x, axis=-1, keepdims=True)
    tok = x * lax.rsqrt(ms + EPS) * g_ref[...]
    row = i * tm + lax.broadcasted_iota(i32, (tm, 1), 0)
    tok = jnp.where(row < m_total, tok, 0.0)
    tok_ref[...] = tok.astype(bf16)
    lg = jnp.dot(tok, wr_ref[...], preferred_element_type=f32, precision=HI)
    ex = jnp.exp(lg - jnp.max(lg, axis=1, keepdims=True))
    aff = ex / jnp.sum(ex, axis=1, keepdims=True)
    aff_ref[...] = aff
    afft_ref[...] = aff.T


def router(s2d, gain, w_router, tm=512):
    m, d = s2d.shape
    mp = _cdiv(m, ROUTE_PAD) * ROUTE_PAD
    last_blk = _cdiv(m, tm) - 1
    return pl.pallas_call(
        functools.partial(_router_body, m_total=m, tm=tm),
        grid=(mp // tm,),
        in_specs=[pl.BlockSpec((tm, d), lambda i: (jnp.minimum(i, last_blk), 0)),
                  pl.BlockSpec((1, d), lambda i: (0, 0)),
                  pl.BlockSpec((d, N_EXPERTS), lambda i: (0, 0))],
        out_specs=[pl.BlockSpec((tm, d), lambda i: (i, 0)),
                   pl.BlockSpec((tm, N_EXPERTS), lambda i: (i, 0)),
                   pl.BlockSpec((N_EXPERTS, tm), lambda i: (0, i))],
        out_shape=[jax.ShapeDtypeStruct((mp, d), bf16),
                   jax.ShapeDtypeStruct((mp, N_EXPERTS), f32),
                   jax.ShapeDtypeStruct((N_EXPERTS, mp), f32)],
        compiler_params=_params("parallel"),
        name="moe_router",
    )(s2d, gain.astype(f32).reshape(1, d), w_router.astype(f32))


def _exclusive_rank(x, nt):
    li = lax.broadcasted_iota(i32, (128, 128), 0)
    lj = lax.broadcasted_iota(i32, (128, 128), 1)
    lane_before = (li < lj).astype(bf16)
    ti = lax.broadcasted_iota(i32, (nt, nt), 0)
    tj = lax.broadcasted_iota(i32, (nt, nt), 1)
    row_before = (tj < ti).astype(bf16)
    xb = x.astype(bf16)
    within = jnp.dot(xb, lane_before, preferred_element_type=f32)
    before = jnp.sum(jnp.dot(row_before, xb, preferred_element_type=f32), axis=1, keepdims=True)
    return within + before


def _count(mask):
    return jnp.sum(jnp.sum(mask.astype(f32), axis=-1, keepdims=True), axis=-2, keepdims=True)


def _select_body(a_ref, sel_ref, *, cap, nt):
    bits = pltpu.bitcast(a_ref[...], i32)

    def step(i, pref):
        cand = pref | jnp.left_shift(jnp.int32(1), 30 - i)
        return jnp.where(_count(bits >= cand) >= cap, cand, pref)

    thr = lax.fori_loop(0, 31, step, jnp.zeros((N_EXPERTS, 1, 1), i32))
    for e in range(N_EXPERTS):
        be = bits[e]
        gt = be > thr[e]
        eq = be == thr[e]
        need = cap - _count(gt)
        take = jnp.logical_and(eq, _exclusive_rank(eq.astype(f32), nt) < need)
        sel_ref[e] = jnp.logical_or(gt, take).astype(f32)


def _rank_body(sel_ref, slot_ref, *, nt):
    for e in range(N_EXPERTS):
        sel = sel_ref[e]
        slot_ref[e] = jnp.where(sel > 0.5, _exclusive_rank(sel, nt), -1.0).astype(i32)


def select_tokens(aff3, cap):
    e, nt, _ = aff3.shape
    return pl.pallas_call(
        functools.partial(_select_body, cap=cap, nt=nt),
        out_shape=jax.ShapeDtypeStruct(aff3.shape, f32),
        compiler_params=pltpu.CompilerParams(vmem_limit_bytes=VMEM_LIMIT),
        name="moe_select",
    )(aff3)


def rank_tokens(sel3):
    e, nt, _ = sel3.shape
    return pl.pallas_call(
        functools.partial(_rank_body, nt=nt),
        out_shape=jax.ShapeDtypeStruct(sel3.shape, i32),
        compiler_params=pltpu.CompilerParams(vmem_limit_bytes=VMEM_LIMIT),
        name="moe_rank",
    )(sel3)


GATE_LANES = 128


def _gather_body(i_ref, j_ref, f_ref, tok_ref, slot_ref, aff_ref, o_ref, g_ref, acc_ref, gacc_ref, *, ns):
    e = pl.program_id(0)
    k = e * ns + pl.program_id(1)
    flags = f_ref[k]

    @pl.when((flags & FLAG_FIRST) != 0)
    def _():
        acc_ref[...] = jnp.zeros(acc_ref.shape, f32)
        gacc_ref[...] = jnp.zeros(gacc_ref.shape, f32)

    @pl.when((flags & FLAG_VALID) != 0)
    def _():
        want = lax.broadcasted_iota(i32, (SLOT_BLK, 1), 0) + j_ref[k] * SLOT_BLK
        onehot = jnp.where(slot_ref[...] == want, 1.0, 0.0).astype(bf16)
        acc_ref[...] += jnp.dot(onehot, tok_ref[...], preferred_element_type=f32)
        pick = lax.broadcasted_iota(i32, (1, N_EXPERTS), 1) == e
        g = jnp.sum(jnp.where(pick, aff_ref[...], 0.0), axis=1, keepdims=True)
        g0 = g.astype(bf16).astype(f32)
        g1 = (g - g0).astype(bf16).astype(f32)
        g2 = g - g0 - g1
        piece = lax.broadcasted_iota(i32, (1, GATE_LANES), 1)
        pieces = jnp.where(piece == 0, g0, jnp.where(piece == 1, g1, jnp.where(piece == 2, g2, 0.0)))
        gacc_ref[...] += jnp.dot(onehot, pieces.astype(bf16), preferred_element_type=f32)

    @pl.when((flags & FLAG_LAST) != 0)
    def _():
        o_ref[...] = acc_ref[...].astype(bf16)
        g_ref[...] = gacc_ref[...]


def gather_tokens(tok, slot_row3, aff, sched, cap_pad, ns):
    d = tok.shape[1]
    ii, jj, ff = sched
    return pl.pallas_call(
        functools.partial(_gather_body, ns=ns),
        grid_spec=pltpu.PrefetchScalarGridSpec(
            num_scalar_prefetch=3,
            grid=(N_EXPERTS, ns),
            in_specs=[pl.BlockSpec((GATHER_TOK, d), lambda e, s, ii, jj, ff: (ii[e * ns + s], 0)),
                      pl.BlockSpec((None, 1, GATHER_TOK), lambda e, s, ii, jj, ff: (e, 0, ii[e * ns + s])),
                      pl.BlockSpec((GATHER_TOK, N_EXPERTS), lambda e, s, ii, jj, ff: (ii[e * ns + s], 0))],
            out_specs=[pl.BlockSpec((None, SLOT_BLK, d), lambda e, s, ii, jj, ff: (e, jj[e * ns + s], 0)),
                       pl.BlockSpec((None, SLOT_BLK, GATE_LANES), lambda e, s, ii, jj, ff: (e, jj[e * ns + s], 0))],
            scratch_shapes=[pltpu.VMEM((SLOT_BLK, d), f32), pltpu.VMEM((SLOT_BLK, GATE_LANES), f32)]),
        out_shape=[jax.ShapeDtypeStruct((N_EXPERTS, cap_pad, d), bf16),
                   jax.ShapeDtypeStruct((N_EXPERTS, cap_pad, GATE_LANES), f32)],
        compiler_params=_params("parallel", "arbitrary"),
        name="moe_gather",
    )(ii, jj, ff, tok, slot_row3, aff)


def _ffn_body(x_ref, g_ref, wg_ref, wu_ref, wd_ref, o_ref, acc_ref):
    fi = pl.program_id(2)

    @pl.when(fi == 0)
    def _():
        acc_ref[...] = jnp.zeros(acc_ref.shape, f32)

    x = x_ref[...]
    gt = jnp.dot(x, wg_ref[...].astype(bf16), preferred_element_type=f32)
    up = jnp.dot(x, wu_ref[...].astype(bf16), preferred_element_type=f32)
    hid = (gt * _sigmoid(gt) * up).astype(bf16)
    acc_ref[...] += jnp.dot(hid, wd_ref[...].astype(bf16), preferred_element_type=f32)

    @pl.when(fi == pl.num_programs(2) - 1)
    def _():
        gate = g_ref[:, 0:1] + g_ref[:, 1:2] + g_ref[:, 2:3]
        o_ref[...] = (acc_ref[...] * gate).T.astype(bf16)


def expert_ffn(xe, gslot, w_gate, w_up, w_down, layer, tf=512, tm_max=768):
    e, cap_pad, d = xe.shape
    dff = w_gate.shape[-1]
    nm = 1
    while cap_pad // nm > tm_max or cap_pad % nm or (cap_pad // nm) % 128:
        nm += 1
    tm = cap_pad // nm
    return pl.pallas_call(
        _ffn_body,
        grid=(e, nm, dff // tf),
        in_specs=[pl.BlockSpec((None, tm, d), lambda ei, mi, fi: (ei, mi, 0)),
                  pl.BlockSpec((None, tm, GATE_LANES), lambda ei, mi, fi: (ei, mi, 0)),
                  pl.BlockSpec((None, None, d, tf), lambda ei, mi, fi: (layer, ei, 0, fi)),
                  pl.BlockSpec((None, None, d, tf), lambda ei, mi, fi: (layer, ei, 0, fi)),
                  pl.BlockSpec((None, None, tf, d), lambda ei, mi, fi: (layer, ei, fi, 0))],
        out_specs=pl.BlockSpec((None, d, tm), lambda ei, mi, fi: (ei, 0, mi)),
        out_shape=jax.ShapeDtypeStruct((e, d, cap_pad), bf16),
        scratch_shapes=[pltpu.VMEM((tm, d), f32)],
        compiler_params=_params("parallel", "parallel", "arbitrary"),
        name="moe_expert_ffn",
    )(xe, gslot, w_gate, w_up, w_down)


def _combine_body(fetch_ref, match_ref, any_ref, s_ref, slot_ref, *rest):
    ye_refs, (o_ref, acc_ref) = rest[:N_EXPERTS], rest[N_EXPERTS:]
    r = pl.program_id(1)
    step = pl.program_id(0) * COMBINE_ROUNDS + r

    @pl.when(r == 0)
    def _():
        acc_ref[...] = jnp.zeros(acc_ref.shape, f32)

    @pl.when(any_ref[step] != 0)
    def _():
        row = lax.broadcasted_iota(i32, (SLOT_BLK, 1), 0)
        total = None
        for e in range(N_EXPERTS):
            want = row + match_ref[step * N_EXPERTS + e] * SLOT_BLK
            onehot = jnp.where(slot_ref[e] == want, 1.0, 0.0).astype(bf16)
            part = jnp.dot(ye_refs[e][...], onehot, preferred_element_type=f32)
            total = part if total is None else part + total
        acc_ref[...] += total

    @pl.when(r == COMBINE_ROUNDS - 1)
    def _():
        o_ref[...] = s_ref[...] + acc_ref[...].T


def combine_tokens(s2d, yet, slot_row3, sched):
    m, d = s2d.shape
    fetch, match, anyv = sched

    def ye_spec(e):
        return pl.BlockSpec((None, d, SLOT_BLK),
                            lambda i, r, fetch, match, anyv: (e, 0, fetch[(i * COMBINE_ROUNDS + r) * N_EXPERTS + e]))

    return pl.pallas_call(
        _combine_body,
        grid_spec=pltpu.PrefetchScalarGridSpec(
            num_scalar_prefetch=3,
            grid=(_cdiv(m, COMBINE_TOK), COMBINE_ROUNDS),
            in_specs=[pl.BlockSpec((COMBINE_TOK, d), lambda i, r, *_: (i, 0)),
                      pl.BlockSpec((N_EXPERTS, 1, COMBINE_TOK), lambda i, r, *_: (0, 0, i))]
                     + [ye_spec(e) for e in range(N_EXPERTS)],
            out_specs=pl.BlockSpec((COMBINE_TOK, d), lambda i, r, *_: (i, 0)),
            scratch_shapes=[pltpu.VMEM((d, COMBINE_TOK), f32)]),
        out_shape=jax.ShapeDtypeStruct((m, d), f32),
        compiler_params=_params("parallel", "arbitrary"),
        name="moe_combine",
    )(fetch, match, anyv, s2d, slot_row3, *([yet] * N_EXPERTS))


def _pair_lists(slot, m, tok_size, n_slot_blk):
    e = slot.shape[0]
    n_tok_blk = _cdiv(m, tok_size)
    cnt = jnp.sum((slot[:, :n_tok_blk * tok_size] >= 0).reshape(e, n_tok_blk, tok_size), axis=-1).astype(i32)
    cum_in = jnp.cumsum(cnt, axis=1)
    cum_ex = cum_in - cnt
    jlo = cum_ex // SLOT_BLK
    jhi = (cum_in - 1) // SLOT_BLK
    npairs = jnp.where(cnt > 0, jhi - jlo + 1, 0)
    off_in = jnp.cumsum(npairs, axis=1)
    off_ex = off_in - npairs
    total = off_in[:, -1:]
    ns = n_tok_blk + n_slot_blk
    step = jnp.arange(ns, dtype=i32)[None, :]
    valid = step < total
    sc = jnp.minimum(step, total - 1)
    tok_blk = jnp.sum(off_in[:, None, :] <= sc[:, :, None], axis=-1).astype(i32)
    tok_blk = jnp.minimum(tok_blk, n_tok_blk - 1)
    slot_blk = (jnp.take_along_axis(jlo, tok_blk, axis=1)
                + sc - jnp.take_along_axis(off_ex, tok_blk, axis=1)).astype(i32)
    return tok_blk, slot_blk, valid


def _gather_schedule(slot, m, n_slot_blk):
    tok_blk, slot_blk, valid = _pair_lists(slot, m, GATHER_TOK, n_slot_blk)
    e, ns = tok_blk.shape
    change = slot_blk[:, 1:] != slot_blk[:, :-1]
    first = jnp.concatenate([jnp.ones((e, 1), bool), change], axis=1)
    last = jnp.concatenate([change | ~valid[:, 1:], jnp.ones((e, 1), bool)], axis=1) & valid
    flags = first * FLAG_FIRST + valid * FLAG_VALID + last * FLAG_LAST
    return tuple(a.reshape(-1).astype(i32) for a in (tok_blk, slot_blk, flags)), ns


def _combine_schedule(slot, m):
    e = slot.shape[0]
    n_tok_blk = _cdiv(m, COMBINE_TOK)
    cnt = jnp.sum((slot[:, :n_tok_blk * COMBINE_TOK] >= 0).reshape(e, n_tok_blk, COMBINE_TOK), axis=-1).astype(i32)
    cum_in = jnp.cumsum(cnt, axis=1)
    jlo = ((cum_in - cnt) // SLOT_BLK)[:, :, None]
    jhi = (cum_in - 1) // SLOT_BLK
    has = (cnt > 0)[:, :, None]
    npairs = jnp.where(has, jhi[:, :, None] - jlo + 1, 0)
    held = lax.cummax(jnp.where(cnt > 0, jhi, 0), axis=1)[:, :, None]
    rnd = jnp.arange(COMBINE_ROUNDS, dtype=i32)[None, None, :]
    valid = rnd < npairs
    fetch = jnp.where(has, jlo + jnp.minimum(rnd, npairs - 1), held)
    match = jnp.where(valid, jlo + rnd, NO_SLOT_BLK)
    flat = lambda a: jnp.transpose(a, (1, 2, 0)).reshape(-1).astype(i32)
    return flat(fetch), flat(match), jnp.any(valid, axis=0).reshape(-1).astype(i32)


def expert_choice_ffn(s3, gain, w_router, w_gate, w_up, w_down, layer):
    b, t, d = s3.shape
    m = b * t
    cap = (EC_CAPACITY * m) // N_EXPERTS
    s2d = s3.reshape(m, d)
    tok, aff, afft = router(s2d, gain, w_router)

    n_pad = _cdiv(m, ROUTE_PAD) * ROUTE_PAD
    nt = n_pad // 128
    a = jnp.roll(afft[:, :m].reshape(N_EXPERTS, b, t), N_META, axis=2).reshape(N_EXPERTS, m)
    a = jnp.pad(a, ((0, 0), (0, n_pad - m)), constant_values=-1.0)
    sel = select_tokens(a.reshape(N_EXPERTS, nt, 128), cap).reshape(N_EXPERTS, n_pad)
    sel = jnp.roll(sel[:, :m].reshape(N_EXPERTS, b, t), -N_META, axis=2).reshape(N_EXPERTS, m)
    sel = jnp.pad(sel, ((0, 0), (0, n_pad - m)))
    slot = rank_tokens(sel.reshape(N_EXPERTS, nt, 128)).reshape(N_EXPERTS, n_pad)

    n_slot_blk = _cdiv(cap, SLOT_BLK)
    cap_pad = n_slot_blk * SLOT_BLK
    g_sched, ns = _gather_schedule(slot, m, n_slot_blk)
    slot_row3 = slot.reshape(N_EXPERTS, 1, n_pad)
    xe, gslot = gather_tokens(tok, slot_row3, aff, g_sched, cap_pad, ns)
    yet = expert_ffn(xe, gslot, w_gate, w_up, w_down, layer)
    out = combine_tokens(s2d, yet, slot_row3, _combine_schedule(slot, m))
    return out.reshape(b, t, d)


def _final_norm_body(x_ref, g_ref, o_ref):
    x = x_ref[...]
    ms = jnp.mean(x * x, axis=-1, keepdims=True)
    o_ref[...] = x * lax.rsqrt(ms + EPS) * g_ref[...]


def final_norm(s3, gain, g, tm=512):
    b, t, d = s3.shape
    assert g % tm == 0
    return pl.pallas_call(
        _final_norm_body,
        grid=(b, g // tm),
        in_specs=[pl.BlockSpec((None, tm, d), lambda bi, i: (bi, i, 0)),
                  pl.BlockSpec((1, d), lambda bi, i: (0, 0))],
        out_specs=pl.BlockSpec((None, tm, d), lambda bi, i: (bi, i, 0)),
        out_shape=jax.ShapeDtypeStruct((b, g, d), f32),
        compiler_params=_params("parallel", "parallel"),
        name="final_norm",
    )(s3, gain.astype(f32).reshape(1, d))


def _reorder_w_in(w):
    cuts = np.cumsum([3 * D_ATT, D_FNET, D_SSM_CONV, D_SSM, 2 * SSM_HEADS])
    qkv, u_f, xbc, z, dt, gate = jnp.split(w, [int(c) for c in cuts], axis=1)
    dt = jnp.pad(dt, ((0, 0), (0, N_PROJ - COL_DT - 2 * SSM_HEADS)))
    return jnp.concatenate([gate, qkv, u_f, xbc, z, dt], axis=1).astype(bf16)


def encoder_layer(s3, rows, layer, lw):
    b, t, d = s3.shape
    m = b * t
    proj = norm_matmul(s3.reshape(m, d), lw["norm1_gain"], lw["w_in"])
    proj3 = proj.reshape(b, t, N_PROJ)
    y_att = neighbourhood_attention(proj3, lw["bias_tab"], lw["meta_bias"], rows)
    y_f = fourier_mix(proj3, COL_F, N_META)
    y_s = ssd_mixer(proj3, rows, lw["conv_w"], lw["conv_b"], lw["a_log"], lw["dt_bias"], lw["d_skip"],
                    lw["ssd_norm_gain"])
    s2d = merge_branches(s3.reshape(m, d), proj, y_att.reshape(m, -1), y_f.reshape(m, -1), y_s.reshape(m, -1),
                         lw["w_branch_a"], lw["w_branch_f"], lw["w_branch_s"], lw["w_out"])
    return expert_choice_ffn(s2d.reshape(b, t, d), lw["norm2_gain"], lw["w_router"],
                             lw["w_exp_gate"], lw["w_exp_up"], lw["w_exp_down"], layer)


def encode(x, meta_tokens, final_gain, layers):
    b, g, d = x.shape
    rows = g // GRID_W
    meta = jnp.broadcast_to(meta_tokens.astype(x.dtype)[None], (b, N_META, d))
    s = jnp.concatenate([x, meta], axis=1)
    for layer, lw in enumerate(layers):
        s = encoder_layer(s, rows, layer, lw)
    return final_norm(s, final_gain, g)


def kernel(x_prompt, x_sample, meta_tokens, norm1_gain, w_in, rel_bias, meta_bias, conv_w, conv_b, a_log, dt_bias,
           d_skip, ssd_norm_gain, w_branch_a, w_branch_f, w_branch_s, w_out, norm2_gain, w_router, w_exp_gate,
           w_exp_up, w_exp_down, final_gain):
    depth = w_in.shape[0]
    rows_set = {x_prompt.shape[1] // GRID_W, x_sample.shape[1] // GRID_W}
    layers = []
    for l in range(depth):
        rb = rel_bias[l]
        layers.append({
            "norm1_gain": norm1_gain[l].astype(f32), "w_in": _reorder_w_in(w_in[l]),
            "rel_bias": rb, "meta_bias": meta_bias[l],
            "conv_w": conv_w[l], "conv_b": conv_b[l], "a_log": a_log[l], "dt_bias": dt_bias[l],
            "d_skip": d_skip[l], "ssd_norm_gain": ssd_norm_gain[l],
            "w_branch_a": w_branch_a[l].astype(bf16), "w_branch_f": w_branch_f[l].astype(bf16),
            "w_branch_s": w_branch_s[l].astype(bf16), "w_out": w_out[l].astype(bf16),
            "norm2_gain": norm2_gain[l], "w_router": w_router[l],
            "w_exp_gate": w_exp_gate, "w_exp_up": w_exp_up, "w_exp_down": w_exp_down,
            "bias_tab": attention_bias_tables(rb, max(rows_set)),
        })
    y_prompt = encode(x_prompt, meta_tokens, final_gain, layers)
    y_sample = encode(x_sample, meta_tokens, final_gain, layers)
    return (y_prompt, y_sample)
```

```python
import functools

import numpy as np
import jax
import jax.numpy as jnp
from jax import lax
from jax.experimental import pallas as pl
from jax.experimental.pallas import tpu as pltpu

f32 = jnp.float32
bf16 = jnp.bfloat16
i32 = jnp.int32
HI = lax.Precision.HIGHEST

D_MODEL = 2048
N_META = 16
GRID_W = 64
ATT_HEADS = 16
ATT_HEAD_DIM = 64
D_ATT = 1024
WIN_R = 8
WIN_C = 16
FNET_GROUPS = 4
D_FNET = 1024
FNET_GROUP_DIM = 256
SSM_HEADS = 16
D_SSM = 1024
SSM_GROUPS = 4
SSM_STATE = 128
SSM_CHUNK = 128
D_SSM_CONV = 2048
N_EXPERTS = 16
EC_CAPACITY = 2
EPS = 1e-6

COL_GATE = 0
COL_QKV = 6144
COL_F = 9216
COL_XBC = 10240
COL_Z = 12288
COL_DT = 13312
N_PROJ = 13824

NEG = -1e30
VMEM_LIMIT = 56 * 1024 * 1024

SLOT_BLK = 256
GATHER_TOK = 1024
COMBINE_TOK = 256
COMBINE_ROUNDS = COMBINE_TOK // SLOT_BLK + 1
NO_SLOT_BLK = -2
ROUTE_PAD = 1024
FLAG_FIRST, FLAG_VALID, FLAG_LAST = 1, 2, 4
ATT_RB = 8
ATT_KR = 16


def _cdiv(a, b):
    return -(-a // b)


def _params(*sem):
    return pltpu.CompilerParams(dimension_semantics=sem, vmem_limit_bytes=VMEM_LIMIT)


def _sigmoid(x):
    return 1.0 / (1.0 + jnp.exp(-x))


def _softplus(x):
    return jnp.maximum(x, 0.0) + jnp.log(1.0 + jnp.exp(-jnp.abs(x)))


def _norm_matmul_body(x_ref, g_ref, w_ref, o_ref, h_ref):
    @pl.when(pl.program_id(1) == 0)
    def _():
        x = x_ref[...]
        ms = jnp.mean(x * x, axis=-1, keepdims=True)
        h_ref[...] = (x * lax.rsqrt(ms + EPS) * g_ref[...]).astype(bf16)

    o_ref[...] = jnp.dot(h_ref[...], w_ref[...], preferred_element_type=f32)


def norm_matmul(x2d, gain, w, tm=1024, tn=1536):
    m, d = x2d.shape
    n = w.shape[1]
    assert n % tn == 0
    return pl.pallas_call(
        _norm_matmul_body,
        grid=(_cdiv(m, tm), n // tn),
        in_specs=[pl.BlockSpec((tm, d), lambda i, j: (i, 0)),
                  pl.BlockSpec((1, d), lambda i, j: (0, 0)),
                  pl.BlockSpec((d, tn), lambda i, j: (0, j))],
        out_specs=pl.BlockSpec((tm, tn), lambda i, j: (i, j)),
        out_shape=jax.ShapeDtypeStruct((m, n), f32),
        scratch_shapes=[pltpu.VMEM((tm, d), bf16)],
        compiler_params=_params("parallel", "arbitrary"),
        name="norm_matmul",
    )(x2d, gain.reshape(1, d), w)


def attention_bias_tables(rel_bias, rows):
    rel = rel_bias.astype(f32)
    nh = rel.shape[0]
    cols = []
    for c in range(GRID_W):
        cs = min(max(c - WIN_C // 2, 0), GRID_W - WIN_C)
        j0 = cs - c + (WIN_C - 1)
        cols.append(jnp.pad(rel[:, :, j0:j0 + WIN_C], ((0, 0), (0, 0), (cs, GRID_W - WIN_C - cs)),
                            constant_values=NEG))
    colbias = jnp.transpose(jnp.stack(cols, axis=2), (0, 2, 1, 3))
    lo = ATT_KR - WIN_R
    nd = 2 * WIN_R - 1 + 2 * lo
    flat = jnp.pad(colbias, ((0, 0), (0, 0), (lo, lo), (0, 0)), constant_values=NEG).reshape(nh, GRID_W, nd * GRID_W)
    lane_row = np.arange(ATT_KR * GRID_W) // GRID_W
    tabs = []
    for r0, k0 in ((0, 0), (ATT_RB, ATT_RB - WIN_R // 2), (rows - ATT_RB, rows - ATT_KR)):
        per_row = []
        for rq in range(ATT_RB):
            r = r0 + rq
            rs = min(max(r - WIN_R // 2, 0), rows - WIN_R)
            start = (k0 - r + (WIN_R - 1) + lo) * GRID_W
            seen = (lane_row >= rs - k0) & (lane_row < rs - k0 + WIN_R)
            per_row.append(jnp.where(seen, flat[:, :, start:start + ATT_KR * GRID_W], NEG))
        tabs.append(jnp.stack(per_row, axis=1).reshape(nh, ATT_RB * GRID_W, ATT_KR * GRID_W))
    return jnp.stack(tabs)


def _attn_body(q_ref, *refs, nblk, nk):
    k_refs, v_refs = refs[:nk], refs[nk:2 * nk]
    km_ref, vm_ref, bias_ref, mb_ref, o_ref = refs[2 * nk:]
    i = pl.program_id(1)
    lane = lax.broadcasted_iota(i32, (1, 128), 1)
    q = q_ref[...] * (ATT_HEAD_DIM ** -0.5)
    km = km_ref[...].astype(bf16)
    vm = vm_ref[...]

    @pl.when(i < nblk)
    def _():
        k = jnp.concatenate([r[...] for r in k_refs], axis=0).astype(bf16)
        v = jnp.concatenate([r[...] for r in v_refs], axis=0)
        acc = jnp.zeros(q.shape, f32)
        for h in range(2):
            hm = (lane // ATT_HEAD_DIM) == h
            qh = jnp.where(hm, q, 0.0).astype(bf16)
            s = lax.dot_general(qh, k, (((1,), (1,)), ((), ())), preferred_element_type=f32) + bias_ref[h]
            sm = lax.dot_general(qh, km, (((1,), (1,)), ((), ())), preferred_element_type=f32) + mb_ref[h:h + 1, :]
            mx = jnp.maximum(jnp.max(s, axis=1, keepdims=True), jnp.max(sm, axis=1, keepdims=True))
            p = jnp.exp(s - mx)
            pm = jnp.exp(sm - mx)
            den = jnp.sum(p, axis=1, keepdims=True) + jnp.sum(pm, axis=1, keepdims=True)
            vh = jnp.where(hm, v, 0.0).astype(bf16)
            vmh = jnp.where(hm, vm, 0.0).astype(bf16)
            o = (jnp.dot(p.astype(bf16), vh, preferred_element_type=f32)
                 + jnp.dot(pm.astype(bf16), vmh, preferred_element_type=f32))
            acc = acc + o / den
        o_ref[...] = acc

    @pl.when(i == nblk)
    def _():
        acc = jnp.zeros(q.shape, f32)
        for h in range(2):
            hm = (lane // ATT_HEAD_DIM) == h
            qh = jnp.where(hm, q, 0.0).astype(bf16)
            sm = lax.dot_general(qh, km, (((1,), (1,)), ((), ())), preferred_element_type=f32) + mb_ref[h:h + 1, :]
            mx = jnp.max(sm, axis=1, keepdims=True)
            pm = jnp.exp(sm - mx)
            den = jnp.sum(pm, axis=1, keepdims=True)
            vmh = jnp.where(hm, vm, 0.0).astype(bf16)
            acc = acc + jnp.dot(pm.astype(bf16), vmh, preferred_element_type=f32) / den
        o_ref[...] = acc


def neighbourhood_attention(proj3, bias_tab, meta_bias, rows):
    b, t, _ = proj3.shape
    g = rows * GRID_W
    assert rows % ATT_RB == 0 and rows >= ATT_KR + ATT_RB
    nblk = rows // ATT_RB
    tq = ATT_RB * GRID_W
    half = WIN_R // 2
    assert ATT_RB % half == 0 and ATT_KR % half == 0 and ATT_KR >= ATT_RB + WIN_R - 1
    tk = half * GRID_W
    nk = ATT_KR // half
    qc, kc, vc = COL_QKV // 128, (COL_QKV + D_ATT) // 128, (COL_QKV + 2 * D_ATT) // 128
    nkb = g // tk

    def kstart(i):
        return jnp.clip(i * (ATT_RB // half) - 1, 0, nkb - nk)

    def kspec(j, col):
        return pl.BlockSpec((None, tk, 128), lambda bi, i, p: (bi, kstart(i) + j, col + p))

    def variant(i):
        return jnp.where(i == 0, 0, jnp.where(i >= nblk - 1, 2, 1))

    in_specs = ([pl.BlockSpec((None, tq, 128), lambda bi, i, p: (bi, i, qc + p))]
                + [kspec(j, kc) for j in range(nk)]
                + [kspec(j, vc) for j in range(nk)]
                + [pl.BlockSpec((None, N_META, 128), lambda bi, i, p: (bi, g // N_META, kc + p)),
                   pl.BlockSpec((None, N_META, 128), lambda bi, i, p: (bi, g // N_META, vc + p)),
                   pl.BlockSpec((None, 2, tq, ATT_KR * GRID_W), lambda bi, i, p: (variant(i), p, 0, 0)),
                   pl.BlockSpec((None, 2, N_META), lambda bi, i, p: (p, 0, 0))])
    return pl.pallas_call(
        functools.partial(_attn_body, nblk=nblk, nk=nk),
        grid=(b, nblk + 1, ATT_HEADS // 2),
        in_specs=in_specs,
        out_specs=pl.BlockSpec((None, tq, 128), lambda bi, i, p: (bi, i, p)),
        out_shape=jax.ShapeDtypeStruct((b, t, D_ATT), f32),
        compiler_params=_params("parallel", "arbitrary", "arbitrary"),
        name="nbr_attention",
    )(proj3, *([proj3] * (2 * nk + 2)), bias_tab, meta_bias.astype(f32).reshape(ATT_HEADS // 2, 2, N_META))


def _fnet_factors(t):
    best = None
    for n1 in range(8, t, 8):
        if t % n1 == 0 and (best is None or n1 + t // n1 < best[0] + best[1]):
            best = (n1, t // n1)
    assert best is not None
    return best


def _cos_sin(num, den):
    ang = 2 * np.pi * (num % den).astype(np.float64) / den
    return np.cos(ang), np.sin(ang)


def _fnet_tables(t, shift):
    n1, n2 = _fnet_factors(t)
    n2p = _cdiv(n2, 8) * 8
    t1, t2 = np.arange(n1, dtype=np.int64), np.arange(n2, dtype=np.int64)
    k1, k2 = t1, t2
    c2, s2 = _cos_sin(np.outer(k2 + shift, t2), n2)
    fa = np.zeros((2 * n2p, n2), np.float64)
    fa[:n2], fa[n2p:n2p + n2] = c2, -s2
    tc, ts = _cos_sin(np.outer(t1 + shift, k2 + shift), t)
    c1, s1 = _cos_sin(np.outer(k1, t1 + shift), n1)
    fc = np.block([[c1, s1], [-s1, c1]])
    ch = np.arange(FNET_GROUP_DIM, dtype=np.int64)
    cc, sc = _cos_sin(np.outer(ch, ch), FNET_GROUP_DIM)
    scale = 1.0 / np.sqrt(t * FNET_GROUP_DIM)
    to = lambda a: jnp.asarray(a, f32)
    return (n1, n2, n2p, to(fa), to(tc[:, :, None]), to(ts[:, :, None]), to(fc), to(cc * scale), to(sc * scale))


def _dot_f32(a, b):
    return jnp.dot(a, b, preferred_element_type=f32, precision=HI)


def _fnet_a_body(u_ref, fa_ref, tc_ref, ts_ref, o_ref, *, n2, n2p):
    fa = fa_ref[...]
    for j in range(8):
        r = _dot_f32(fa, u_ref[:, j, :])
        re, im = r[:n2], r[n2p:n2p + n2]
        tc, ts = tc_ref[j], ts_ref[j]
        o_ref[0, j] = re * tc + im * ts
        o_ref[1, j] = im * tc - re * ts


def _dot_bf16(a, b):
    return jnp.dot(a.astype(bf16), b.astype(bf16), preferred_element_type=f32)


def _fnet_c_body(p_ref, fc_ref, o_ref):
    o_ref[...] = _dot_bf16(fc_ref[...], p_ref[...])


def _fnet_d_body(q_ref, cc_ref, sc_ref, o_ref):
    cc, sc = cc_ref[...], sc_ref[...]
    for g in range(FNET_GROUPS):
        sl = slice(g * FNET_GROUP_DIM, (g + 1) * FNET_GROUP_DIM)
        o_ref[:, sl] = _dot_bf16(q_ref[0, :, sl], cc) + _dot_bf16(q_ref[1, :, sl], sc)


def fourier_mix(x3, col0, shift):
    b, t, ncols = x3.shape
    d = D_FNET
    n1, n2, n2p, fa, tc, ts, fc, cc, sc = _fnet_tables(t, shift)
    cb = 512
    assert col0 % cb == 0
    const2 = lambda shape: pl.BlockSpec(shape, lambda *_: (0, 0))
    p = pl.pallas_call(
        functools.partial(_fnet_a_body, n2=n2, n2p=n2p),
        grid=(b, n1 // 8, d // cb),
        in_specs=[pl.BlockSpec((None, n2, 8, cb), lambda bi, i, c: (bi, 0, i, col0 // cb + c)),
                  const2((2 * n2p, n2)),
                  pl.BlockSpec((8, n2, 1), lambda bi, i, c: (i, 0, 0)),
                  pl.BlockSpec((8, n2, 1), lambda bi, i, c: (i, 0, 0))],
        out_specs=pl.BlockSpec((None, 2, 8, n2, cb), lambda bi, i, c: (bi, 0, i, 0, c)),
        out_shape=jax.ShapeDtypeStruct((b, 2, n1, n2, d), f32),
        compiler_params=_params("parallel", "parallel", "parallel"),
        name="fnet_stage_a",
    )(x3.reshape(b, n2, n1, ncols), fa, tc, ts)
    cw = 2048
    ncol = n2 * d
    q = pl.pallas_call(
        _fnet_c_body,
        grid=(b, _cdiv(ncol, cw)),
        in_specs=[pl.BlockSpec((None, 2 * n1, cw), lambda bi, c: (bi, 0, c)),
                  const2((2 * n1, 2 * n1))],
        out_specs=pl.BlockSpec((None, 2 * n1, cw), lambda bi, c: (bi, 0, c)),
        out_shape=jax.ShapeDtypeStruct((b, 2 * n1, ncol), f32),
        compiler_params=_params("parallel", "parallel"),
        name="fnet_stage_c",
    )(p.reshape(b, 2 * n1, ncol), fc.astype(bf16))
    tm = 512
    gd = (FNET_GROUP_DIM, FNET_GROUP_DIM)
    return pl.pallas_call(
        _fnet_d_body,
        grid=(b, _cdiv(t, tm)),
        in_specs=[pl.BlockSpec((None, 2, tm, d), lambda bi, i: (bi, 0, i, 0)),
                  const2(gd), const2(gd)],
        out_specs=pl.BlockSpec((None, tm, d), lambda bi, i: (bi, i, 0)),
        out_shape=jax.ShapeDtypeStruct((b, t, d), f32),
        compiler_params=_params("parallel", "parallel"),
        name="fnet_stage_d",
    )(q.reshape(b, 2, t, d), cc.astype(bf16), sc.astype(bf16))


def _ssd_body(*refs, d, nchunks):
    if d == 0:
        (xc_ref, dt_ref, alog_ref, dtb_ref, z_ref, yr_ref, dsk_ref, ng_ref, o_ref, st_ref, y_ref) = refs
    else:
        (xbc_ref, prev_ref, next_ref, dt_ref, cw_ref, cb_ref, alog_ref, dtb_ref, o_ref, xc_out_ref, st_ref) = refs
    L = SSM_CHUNK
    j = pl.program_id(1)
    if d == 0:
        ci = jnp.where(j == 0, nchunks, j - 1)
    else:
        ci = jnp.where(j == nchunks, nchunks, nchunks - 1 - j)
    is_meta = ci == nchunks

    @pl.when(j == 0)
    def _():
        st_ref[...] = jnp.zeros(st_ref.shape, f32)

    row = lax.broadcasted_iota(i32, (L, 1), 0)
    nvalid = jnp.where(is_meta, N_META, L)
    valid = row < nvalid

    if d == 0:
        xc = jnp.where(valid, xc_ref[...], 0.0)
    else:
        x = jnp.where(valid, xbc_ref[...], 0.0)
        prev = jnp.where(is_meta, 0.0, prev_ref[7:8, :])
        nxt = jnp.where(ci == nchunks - 1, 0.0, next_ref[0:1, :])
        xp = jnp.where(row == 0, prev, pltpu.roll(x, 1, axis=0))
        xn = jnp.where(row == nvalid - 1, nxt, pltpu.roll(x, L - 1, axis=0))
        w = cw_ref[...]
        pre = w[0:1] * xp + w[1:2] * x + w[2:3] * xn + cb_ref[...]
        xc = jnp.where(valid, pre * _sigmoid(pre), 0.0)
        xc_out_ref[...] = xc
    xs = xc[:, :D_SSM]
    bm = xc[:, D_SSM:D_SSM + SSM_GROUPS * SSM_STATE]
    cm = xc[:, D_SSM + SSM_GROUPS * SSM_STATE:]

    dtf = jnp.where(valid, _softplus(dt_ref[...] + dtb_ref[...]), 0.0)
    da = dtf * (-jnp.exp(alog_ref[...]))
    li = lax.broadcasted_iota(i32, (L, L), 0)
    si = lax.broadcasted_iota(i32, (L, L), 1)
    causal = (si <= li) if d == 0 else (si >= li)
    ac = jnp.dot(causal.astype(f32), da, preferred_element_type=f32, precision=HI)
    act = ac.T
    dtt = dtf.T
    tot = jnp.sum(da, axis=0, keepdims=True)

    lane = lax.broadcasted_iota(i32, (1, 128), 1)
    cbs = {}
    for p in range(SSM_HEADS // 2):
        g = (2 * p) // (SSM_HEADS // SSM_GROUPS)
        bg = bm[:, g * SSM_STATE:(g + 1) * SSM_STATE]
        cg = cm[:, g * SSM_STATE:(g + 1) * SSM_STATE]
        if g not in cbs:
            cbs[g] = lax.dot_general(cg.astype(bf16), bg.astype(bf16), (((1,), (1,)), ((), ())),
                                     preferred_element_type=f32)
        cbg = cbs[g]
        xs_p = xs[:, p * 128:(p + 1) * 128]
        st = st_ref[p]
        y_p = jnp.zeros((L, 128), f32)
        new_st = jnp.zeros((SSM_STATE, 128), f32)
        dec_row = jnp.zeros((1, 128), f32)
        for hh in range(2):
            col = d * SSM_HEADS + 2 * p + hh
            ac_c, ac_r = ac[:, col:col + 1], act[col:col + 1, :]
            dt_c, dt_r = dtf[:, col:col + 1], dtt[col:col + 1, :]
            tot_h = tot[:, col:col + 1]
            hm = (lane // 64) == hh
            m = cbg * jnp.exp(jnp.where(causal, ac_c - ac_r, NEG)) * dt_r
            xm = jnp.where(hm, xs_p, 0.0).astype(bf16)
            stm = jnp.where(hm, st, 0.0).astype(bf16)
            cwt = cg * jnp.exp(ac_c)
            lhs = jnp.concatenate([m, cwt], axis=1).astype(bf16)
            rhs = jnp.concatenate([xm, stm], axis=0)
            y_p = y_p + jnp.dot(lhs, rhs, preferred_element_type=f32)
            bw = (bg * (jnp.exp(tot_h - ac_c) * dt_c)).astype(bf16)
            new_st = new_st + lax.dot_general(bw, xm, (((0,), (0,)), ((), ())), preferred_element_type=f32)
            dec_row = jnp.where(hm, jnp.exp(tot_h), dec_row)
        st_ref[p] = st * dec_row + new_st
        if d == 0:
            y_ref[:, p * 128:(p + 1) * 128] = y_p
        else:
            o_ref[:, p * 128:(p + 1) * 128] = y_p

    if d == 0:
        y = y_ref[...] + yr_ref[...] + xs * dsk_ref[...]
        z = z_ref[...]
        y = y * (z * _sigmoid(z))
        ms = jnp.mean(y * y, axis=-1, keepdims=True)
        o_ref[...] = y * lax.rsqrt(ms + EPS) * ng_ref[...]


def ssd_mixer(proj3, rows, conv_w, conv_b, a_log, dt_bias, d_skip, norm_gain):
    b, t, _ = proj3.shape
    g = rows * GRID_W
    L = SSM_CHUNK
    assert g % L == 0
    nch = g // L
    pad128 = lambda v: jnp.pad(v.astype(f32).reshape(1, -1), ((0, 0), (0, 128 - 2 * SSM_HEADS)))
    alog, dtb = pad128(a_log), pad128(dt_bias)
    cw = conv_w.astype(f32)
    cb = conv_b.astype(f32).reshape(1, -1)
    dsk = jnp.repeat(d_skip.astype(f32), D_SSM // SSM_HEADS).reshape(1, -1)
    ng = norm_gain.astype(f32).reshape(1, -1)
    xcol, zcol, dcol = COL_XBC // D_SSM_CONV, COL_Z // D_SSM, COL_DT // 128
    hb = L // 8

    const = lambda shape: pl.BlockSpec(shape, lambda bi, j: (0,) * len(shape))
    state = pltpu.VMEM((SSM_HEADS // 2, SSM_STATE, 128), f32)
    chunk = lambda cidx, width, col: pl.BlockSpec((None, L, width), lambda bi, j: (bi, cidx(j), col))

    ridx = lambda j: jnp.where(j == nch, nch, nch - 1 - j)

    def pidx(j):
        ci = ridx(j)
        return jnp.where(ci == nch, 0, jnp.where(ci == 0, hb * nch + 1, hb * ci - 1))

    def nidx(j):
        ci = ridx(j)
        return jnp.where(ci >= nch - 1, 0, hb * (ci + 1))

    y_rev, xc = pl.pallas_call(
        functools.partial(_ssd_body, d=1, nchunks=nch),
        grid=(b, nch + 1),
        in_specs=[chunk(ridx, D_SSM_CONV, xcol),
                  pl.BlockSpec((None, 8, D_SSM_CONV), lambda bi, j: (bi, pidx(j), xcol)),
                  pl.BlockSpec((None, 8, D_SSM_CONV), lambda bi, j: (bi, nidx(j), xcol)),
                  chunk(ridx, 128, dcol),
                  const((3, D_SSM_CONV)), const((1, D_SSM_CONV)), const((1, 128)), const((1, 128))],
        out_specs=[chunk(ridx, D_SSM, 0), chunk(ridx, D_SSM_CONV, 0)],
        out_shape=[jax.ShapeDtypeStruct((b, t, D_SSM), f32), jax.ShapeDtypeStruct((b, t, D_SSM_CONV), f32)],
        scratch_shapes=[state],
        compiler_params=_params("parallel", "arbitrary"),
        name="ssd_rev",
    )(proj3, proj3, proj3, proj3, cw, cb, alog, dtb)

    fidx = lambda j: jnp.where(j == 0, nch, j - 1)
    return pl.pallas_call(
        functools.partial(_ssd_body, d=0, nchunks=nch),
        grid=(b, nch + 1),
        in_specs=[chunk(fidx, D_SSM_CONV, 0), chunk(fidx, 128, dcol), const((1, 128)), const((1, 128)),
                  chunk(fidx, D_SSM, zcol), chunk(fidx, D_SSM, 0), const((1, D_SSM)), const((1, D_SSM))],
        out_specs=chunk(fidx, D_SSM, 0),
        out_shape=jax.ShapeDtypeStruct((b, t, D_SSM), f32),
        scratch_shapes=[state, pltpu.VMEM((L, D_SSM), f32)],
        compiler_params=_params("parallel", "arbitrary"),
        name="ssd_fwd",
    )(xc, proj3, alog, dtb, proj3, y_rev, dsk, ng)


def _merge_body(ya, yf, ys, g0, g1, g2, s_ref, wa, wf, ws, wo, o_ref):
    def branch(y, g, w):
        return _sigmoid(g[...]) * jnp.dot(y[...].astype(bf16), w[...], preferred_element_type=f32)

    merged = branch(ya, g0, wa) + branch(yf, g1, wf) + branch(ys, g2, ws)
    o_ref[...] = s_ref[...] + jnp.dot(merged.astype(bf16), wo[...], preferred_element_type=f32)


def merge_branches(s2d, proj2d, ya, yf, ys, wa, wf, ws, wo, tm=256):
    m, d = s2d.shape
    row = lambda w: pl.BlockSpec((tm, w), lambda i: (i, 0))
    gate = lambda k: pl.BlockSpec((tm, d), lambda i: (i, COL_GATE // d + k))
    wspec = lambda r: pl.BlockSpec((r, d), lambda i: (0, 0), pipeline_mode=pl.Buffered(1))
    return pl.pallas_call(
        _merge_body,
        grid=(_cdiv(m, tm),),
        in_specs=[row(D_ATT), row(D_FNET), row(D_SSM), gate(0), gate(1), gate(2), row(d),
                  wspec(D_ATT), wspec(D_FNET), wspec(D_SSM), wspec(d)],
        out_specs=row(d),
        out_shape=jax.ShapeDtypeStruct((m, d), f32),
        compiler_params=_params("parallel"),
        name="merge_branches",
    )(ya, yf, ys, proj2d, proj2d, proj2d, s2d, wa, wf, ws, wo)


def _router_body(s_ref, g_ref, wr_ref, tok_ref, aff_ref, afft_ref, *, m_total, tm):
    i = pl.program_id(0)
    x = s_ref[...]
    ms = jnp.mean(x * x, axis=-1, keepdims=True)
    tok = x * lax.rsqrt(ms + EPS) * g_ref[...]
    row = i * tm + lax.broadcasted_iota(i32, (tm, 1), 0)
    tok = jnp.where(row < m_total, tok, 0.0)
    tok_ref[...] = tok.astype(bf16)
    lg = jnp.dot(tok, wr_ref[...], preferred_element_type=f32, precision=HI)
    ex = jnp.exp(lg - jnp.max(lg, axis=1, keepdims=True))
    aff = ex / jnp.sum(ex, axis=1, keepdims=True)
    aff_ref[...] = aff
    afft_ref[...] = aff.T


def router(s2d, gain, w_router, tm=512):
    m, d = s2d.shape
    mp = _cdiv(m, ROUTE_PAD) * ROUTE_PAD
    last_blk = _cdiv(m, tm) - 1
    return pl.pallas_call(
        functools.partial(_router_body, m_total=m, tm=tm),
        grid=(mp // tm,),
        in_specs=[pl.BlockSpec((tm, d), lambda i: (jnp.minimum(i, last_blk), 0)),
                  pl.BlockSpec((1, d), lambda i: (0, 0)),
                  pl.BlockSpec((d, N_EXPERTS), lambda i: (0, 0))],
        out_specs=[pl.BlockSpec((tm, d), lambda i: (i, 0)),
                   pl.BlockSpec((tm, N_EXPERTS), lambda i: (i, 0)),
                   pl.BlockSpec((N_EXPERTS, tm), lambda i: (0, i))],
        out_shape=[jax.ShapeDtypeStruct((mp, d), bf16),
                   jax.ShapeDtypeStruct((mp, N_EXPERTS), f32),
                   jax.ShapeDtypeStruct((N_EXPERTS, mp), f32)],
        compiler_params=_params("parallel"),
        name="moe_router",
    )(s2d, gain.astype(f32).reshape(1, d), w_router.astype(f32))


def _exclusive_rank(x, nt):
    li = lax.broadcasted_iota(i32, (128, 128), 0)
    lj = lax.broadcasted_iota(i32, (128, 128), 1)
    lane_before = (li < lj).astype(bf16)
    ti = lax.broadcasted_iota(i32, (nt, nt), 0)
    tj = lax.broadcasted_iota(i32, (nt, nt), 1)
    row_before = (tj < ti).astype(bf16)
    xb = x.astype(bf16)
    within = jnp.dot(xb, lane_before, preferred_element_type=f32)
    before = jnp.sum(jnp.dot(row_before, xb, preferred_element_type=f32), axis=1, keepdims=True)
    return within + before


def _count(mask):
    return jnp.sum(jnp.sum(mask.astype(f32), axis=-1, keepdims=True), axis=-2, keepdims=True)


def _select_body(a_ref, sel_ref, *, cap, nt):
    bits = pltpu.bitcast(a_ref[...], i32)

    def step(i, pref):
        cand = pref | jnp.left_shift(jnp.int32(1), 30 - i)
        return jnp.where(_count(bits >= cand) >= cap, cand, pref)

    thr = lax.fori_loop(0, 31, step, jnp.zeros((N_EXPERTS, 1, 1), i32))
    for e in range(N_EXPERTS):
        be = bits[e]
        gt = be > thr[e]
        eq = be == thr[e]
        need = cap - _count(gt)
        take = jnp.logical_and(eq, _exclusive_rank(eq.astype(f32), nt) < need)
        sel_ref[e] = jnp.logical_or(gt, take).astype(f32)


def _rank_body(sel_ref, slot_ref, *, nt):
    for e in range(N_EXPERTS):
        sel = sel_ref[e]
        slot_ref[e] = jnp.where(sel > 0.5, _exclusive_rank(sel, nt), -1.0).astype(i32)


def select_tokens(aff3, cap):
    e, nt, _ = aff3.shape
    return pl.pallas_call(
        functools.partial(_select_body, cap=cap, nt=nt),
        out_shape=jax.ShapeDtypeStruct(aff3.shape, f32),
        compiler_params=pltpu.CompilerParams(vmem_limit_bytes=VMEM_LIMIT),
        name="moe_select",
    )(aff3)


def rank_tokens(sel3):
    e, nt, _ = sel3.shape
    return pl.pallas_call(
        functools.partial(_rank_body, nt=nt),
        out_shape=jax.ShapeDtypeStruct(sel3.shape, i32),
        compiler_params=pltpu.CompilerParams(vmem_limit_bytes=VMEM_LIMIT),
        name="moe_rank",
    )(sel3)


GATE_LANES = 128


def _gather_body(i_ref, j_ref, f_ref, tok_ref, slot_ref, aff_ref, o_ref, g_ref, acc_ref, gacc_ref, *, ns):
    e = pl.program_id(0)
    k = e * ns + pl.program_id(1)
    flags = f_ref[k]

    @pl.when((flags & FLAG_FIRST) != 0)
    def _():
        acc_ref[...] = jnp.zeros(acc_ref.shape, f32)
        gacc_ref[...] = jnp.zeros(gacc_ref.shape, f32)

    @pl.when((flags & FLAG_VALID) != 0)
    def _():
        want = lax.broadcasted_iota(i32, (SLOT_BLK, 1), 0) + j_ref[k] * SLOT_BLK
        onehot = jnp.where(slot_ref[...] == want, 1.0, 0.0).astype(bf16)
        acc_ref[...] += jnp.dot(onehot, tok_ref[...], preferred_element_type=f32)
        pick = lax.broadcasted_iota(i32, (1, N_EXPERTS), 1) == e
        g = jnp.sum(jnp.where(pick, aff_ref[...], 0.0), axis=1, keepdims=True)
        g0 = g.astype(bf16).astype(f32)
        g1 = (g - g0).astype(bf16).astype(f32)
        g2 = g - g0 - g1
        piece = lax.broadcasted_iota(i32, (1, GATE_LANES), 1)
        pieces = jnp.where(piece == 0, g0, jnp.where(piece == 1, g1, jnp.where(piece == 2, g2, 0.0)))
        gacc_ref[...] += jnp.dot(onehot, pieces.astype(bf16), preferred_element_type=f32)

    @pl.when((flags & FLAG_LAST) != 0)
    def _():
        o_ref[...] = acc_ref[...].astype(bf16)
        g_ref[...] = gacc_ref[...]


def gather_tokens(tok, slot_row3, aff, sched, cap_pad, ns):
    d = tok.shape[1]
    ii, jj, ff = sched
    return pl.pallas_call(
        functools.partial(_gather_body, ns=ns),
        grid_spec=pltpu.PrefetchScalarGridSpec(
            num_scalar_prefetch=3,
            grid=(N_EXPERTS, ns),
            in_specs=[pl.BlockSpec((GATHER_TOK, d), lambda e, s, ii, jj, ff: (ii[e * ns + s], 0)),
                      pl.BlockSpec((None, 1, GATHER_TOK), lambda e, s, ii, jj, ff: (e, 0, ii[e * ns + s])),
                      pl.BlockSpec((GATHER_TOK, N_EXPERTS), lambda e, s, ii, jj, ff: (ii[e * ns + s], 0))],
            out_specs=[pl.BlockSpec((None, SLOT_BLK, d), lambda e, s, ii, jj, ff: (e, jj[e * ns + s], 0)),
                       pl.BlockSpec((None, SLOT_BLK, GATE_LANES), lambda e, s, ii, jj, ff: (e, jj[e * ns + s], 0))],
            scratch_shapes=[pltpu.VMEM((SLOT_BLK, d), f32), pltpu.VMEM((SLOT_BLK, GATE_LANES), f32)]),
        out_shape=[jax.ShapeDtypeStruct((N_EXPERTS, cap_pad, d), bf16),
                   jax.ShapeDtypeStruct((N_EXPERTS, cap_pad, GATE_LANES), f32)],
        compiler_params=_params("parallel", "arbitrary"),
        name="moe_gather",
    )(ii, jj, ff, tok, slot_row3, aff)


def _ffn_body(x_ref, g_ref, wg_ref, wu_ref, wd_ref, o_ref, acc_ref):
    fi = pl.program_id(2)

    @pl.when(fi == 0)
    def _():
        acc_ref[...] = jnp.zeros(acc_ref.shape, f32)

    x = x_ref[...]
    gt = jnp.dot(x, wg_ref[...].astype(bf16), preferred_element_type=f32)
    up = jnp.dot(x, wu_ref[...].astype(bf16), preferred_element_type=f32)
    hid = (gt * _sigmoid(gt) * up).astype(bf16)
    acc_ref[...] += jnp.dot(hid, wd_ref[...].astype(bf16), preferred_element_type=f32)

    @pl.when(fi == pl.num_programs(2) - 1)
    def _():
        gate = g_ref[:, 0:1] + g_ref[:, 1:2] + g_ref[:, 2:3]
        o_ref[...] = (acc_ref[...] * gate).T.astype(bf16)


def expert_ffn(xe, gslot, w_gate, w_up, w_down, layer, tf=512, tm_max=768):
    e, cap_pad, d = xe.shape
    dff = w_gate.shape[-1]
    nm = 1
    while cap_pad // nm > tm_max or cap_pad % nm or (cap_pad // nm) % 128:
        nm += 1
    tm = cap_pad // nm
    return pl.pallas_call(
        _ffn_body,
        grid=(e, nm, dff // tf),
        in_specs=[pl.BlockSpec((None, tm, d), lambda ei, mi, fi: (ei, mi, 0)),
                  pl.BlockSpec((None, tm, GATE_LANES), lambda ei, mi, fi: (ei, mi, 0)),
                  pl.BlockSpec((None, None, d, tf), lambda ei, mi, fi: (layer, ei, 0, fi)),
                  pl.BlockSpec((None, None, d, tf), lambda ei, mi, fi: (layer, ei, 0, fi)),
                  pl.BlockSpec((None, None, tf, d), lambda ei, mi, fi: (layer, ei, fi, 0))],
        out_specs=pl.BlockSpec((None, d, tm), lambda ei, mi, fi: (ei, 0, mi)),
        out_shape=jax.ShapeDtypeStruct((e, d, cap_pad), bf16),
        scratch_shapes=[pltpu.VMEM((tm, d), f32)],
        compiler_params=_params("parallel", "parallel", "arbitrary"),
        name="moe_expert_ffn",
    )(xe, gslot, w_gate, w_up, w_down)


def _combine_body(fetch_ref, match_ref, any_ref, s_ref, slot_ref, *rest):
    ye_refs, (o_ref, acc_ref) = rest[:N_EXPERTS], rest[N_EXPERTS:]
    r = pl.program_id(1)
    step = pl.program_id(0) * COMBINE_ROUNDS + r

    @pl.when(r == 0)
    def _():
        acc_ref[...] = jnp.zeros(acc_ref.shape, f32)

    @pl.when(any_ref[step] != 0)
    def _():
        row = lax.broadcasted_iota(i32, (SLOT_BLK, 1), 0)
        total = None
        for e in range(N_EXPERTS):
            want = row + match_ref[step * N_EXPERTS + e] * SLOT_BLK
            onehot = jnp.where(slot_ref[e] == want, 1.0, 0.0).astype(bf16)
            part = jnp.dot(ye_refs[e][...], onehot, preferred_element_type=f32)
            total = part if total is None else part + total
        acc_ref[...] += total

    @pl.when(r == COMBINE_ROUNDS - 1)
    def _():
        o_ref[...] = s_ref[...] + acc_ref[...].T


def combine_tokens(s2d, yet, slot_row3, sched):
    m, d = s2d.shape
    fetch, match, anyv = sched

    def ye_spec(e):
        return pl.BlockSpec((None, d, SLOT_BLK),
                            lambda i, r, fetch, match, anyv: (e, 0, fetch[(i * COMBINE_ROUNDS + r) * N_EXPERTS + e]))

    return pl.pallas_call(
        _combine_body,
        grid_spec=pltpu.PrefetchScalarGridSpec(
            num_scalar_prefetch=3,
            grid=(_cdiv(m, COMBINE_TOK), COMBINE_ROUNDS),
            in_specs=[pl.BlockSpec((COMBINE_TOK, d), lambda i, r, *_: (i, 0)),
                      pl.BlockSpec((N_EXPERTS, 1, COMBINE_TOK), lambda i, r, *_: (0, 0, i))]
                     + [ye_spec(e) for e in range(N_EXPERTS)],
            out_specs=pl.BlockSpec((COMBINE_TOK, d), lambda i, r, *_: (i, 0)),
            scratch_shapes=[pltpu.VMEM((d, COMBINE_TOK), f32)]),
        out_shape=jax.ShapeDtypeStruct((m, d), f32),
        compiler_params=_params("parallel", "arbitrary"),
        name="moe_combine",
    )(fetch, match, anyv, s2d, slot_row3, *([yet] * N_EXPERTS))


def _pair_lists(slot, m, tok_size, n_slot_blk):
    e = slot.shape[0]
    n_tok_blk = _cdiv(m, tok_size)
    cnt = jnp.sum((slot[:, :n_tok_blk * tok_size] >= 0).reshape(e, n_tok_blk, tok_size), axis=-1).astype(i32)
    cum_in = jnp.cumsum(cnt, axis=1)
    cum_ex = cum_in - cnt
    jlo = cum_ex // SLOT_BLK
    jhi = (cum_in - 1) // SLOT_BLK
    npairs = jnp.where(cnt > 0, jhi - jlo + 1, 0)
    off_in = jnp.cumsum(npairs, axis=1)
    off_ex = off_in - npairs
    total = off_in[:, -1:]
    ns = n_tok_blk + n_slot_blk
    step = jnp.arange(ns, dtype=i32)[None, :]
    valid = step < total
    sc = jnp.minimum(step, total - 1)
    tok_blk = jnp.sum(off_in[:, None, :] <= sc[:, :, None], axis=-1).astype(i32)
    tok_blk = jnp.minimum(tok_blk, n_tok_blk - 1)
    slot_blk = (jnp.take_along_axis(jlo, tok_blk, axis=1)
                + sc - jnp.take_along_axis(off_ex, tok_blk, axis=1)).astype(i32)
    return tok_blk, slot_blk, valid


def _gather_schedule(slot, m, n_slot_blk):
    tok_blk, slot_blk, valid = _pair_lists(slot, m, GATHER_TOK, n_slot_blk)
    e, ns = tok_blk.shape
    change = slot_blk[:, 1:] != slot_blk[:, :-1]
    first = jnp.concatenate([jnp.ones((e, 1), bool), change], axis=1)
    last = jnp.concatenate([change | ~valid[:, 1:], jnp.ones((e, 1), bool)], axis=1) & valid
    flags = first * FLAG_FIRST + valid * FLAG_VALID + last * FLAG_LAST
    return tuple(a.reshape(-1).astype(i32) for a in (tok_blk, slot_blk, flags)), ns


def _combine_schedule(slot, m):
    e = slot.shape[0]
    n_tok_blk = _cdiv(m, COMBINE_TOK)
    cnt = jnp.sum((slot[:, :n_tok_blk * COMBINE_TOK] >= 0).reshape(e, n_tok_blk, COMBINE_TOK), axis=-1).astype(i32)
    cum_in = jnp.cumsum(cnt, axis=1)
    jlo = ((cum_in - cnt) // SLOT_BLK)[:, :, None]
    jhi = (cum_in - 1) // SLOT_BLK
    has = (cnt > 0)[:, :, None]
    npairs = jnp.where(has, jhi[:, :, None] - jlo + 1, 0)
    held = lax.cummax(jnp.where(cnt > 0, jhi, 0), axis=1)[:, :, None]
    rnd = jnp.arange(COMBINE_ROUNDS, dtype=i32)[None, None, :]
    valid = rnd < npairs
    fetch = jnp.where(has, jlo + jnp.minimum(rnd, npairs - 1), held)
    match = jnp.where(valid, jlo + rnd, NO_SLOT_BLK)
    flat = lambda a: jnp.transpose(a, (1, 2, 0)).reshape(-1).astype(i32)
    return flat(fetch), flat(match), jnp.any(valid, axis=0).reshape(-1).astype(i32)


def expert_choice_ffn(s3, gain, w_router, w_gate, w_up, w_down, layer):
    b, t, d = s3.shape
    m = b * t
    cap = (EC_CAPACITY * m) // N_EXPERTS
    s2d = s3.reshape(m, d)
    tok, aff, afft = router(s2d, gain, w_router)

    n_pad = _cdiv(m, ROUTE_PAD) * ROUTE_PAD
    nt = n_pad // 128
    a = jnp.roll(afft[:, :m].reshape(N_EXPERTS, b, t), N_META, axis=2).reshape(N_EXPERTS, m)
    a = jnp.pad(a, ((0, 0), (0, n_pad - m)), constant_values=-1.0)
    sel = select_tokens(a.reshape(N_EXPERTS, nt, 128), cap).reshape(N_EXPERTS, n_pad)
    sel = jnp.roll(sel[:, :m].reshape(N_EXPERTS, b, t), -N_META, axis=2).reshape(N_EXPERTS, m)
    sel = jnp.pad(sel, ((0, 0), (0, n_pad - m)))
    slot = rank_tokens(sel.reshape(N_EXPERTS, nt, 128)).reshape(N_EXPERTS, n_pad)

    n_slot_blk = _cdiv(cap, SLOT_BLK)
    cap_pad = n_slot_blk * SLOT_BLK
    g_sched, ns = _gather_schedule(slot, m, n_slot_blk)
    slot_row3 = slot.reshape(N_EXPERTS, 1, n_pad)
    xe, gslot = gather_tokens(tok, slot_row3, aff, g_sched, cap_pad, ns)
    yet = expert_ffn(xe, gslot, w_gate, w_up, w_down, layer)
    out = combine_tokens(s2d, yet, slot_row3, _combine_schedule(slot, m))
    return out.reshape(b, t, d)


def _final_norm_body(x_ref, g_ref, o_ref):
    x = x_ref[...]
    ms = jnp.mean(x * x, axis=-1, keepdims=True)
    o_ref[...] = x * lax.rsqrt(ms + EPS) * g_ref[...]


def final_norm(s3, gain, g, tm=512):
    b, t, d = s3.shape
    assert g % tm == 0
    return pl.pallas_call(
        _final_norm_body,
        grid=(b, g // tm),
        in_specs=[pl.BlockSpec((None, tm, d), lambda bi, i: (bi, i, 0)),
                  pl.BlockSpec((1, d), lambda bi, i: (0, 0))],
        out_specs=pl.BlockSpec((None, tm, d), lambda bi, i: (bi, i, 0)),
        out_shape=jax.ShapeDtypeStruct((b, g, d), f32),
        compiler_params=_params("parallel", "parallel"),
        name="final_norm",
    )(s3, gain.astype(f32).reshape(1, d))


def _reorder_w_in(w):
    cuts = np.cumsum([3 * D_ATT, D_FNET, D_SSM_CONV, D_SSM, 2 * SSM_HEADS])
    qkv, u_f, xbc, z, dt, gate = jnp.split(w, [int(c) for c in cuts], axis=1)
    dt = jnp.pad(dt, ((0, 0), (0, N_PROJ - COL_DT - 2 * SSM_HEADS)))
    return jnp.concatenate([gate, qkv, u_f, xbc, z, dt], axis=1).astype(bf16)


def encoder_layer(s3, rows, layer, lw):
    b, t, d = s3.shape
    m = b * t
    proj = norm_matmul(s3.reshape(m, d), lw["norm1_gain"], lw["w_in"])
    proj3 = proj.reshape(b, t, N_PROJ)
    y_att = neighbourhood_attention(proj3, lw["bias_tab"], lw["meta_bias"], rows)
    y_f = fourier_mix(proj3, COL_F, N_META)
    y_s = ssd_mixer(proj3, rows, lw["conv_w"], lw["conv_b"], lw["a_log"], lw["dt_bias"], lw["d_skip"],
                    lw["ssd_norm_gain"])
    s2d = merge_branches(s3.reshape(m, d), proj, y_att.reshape(m, -1), y_f.reshape(m, -1), y_s.reshape(m, -1),
                         lw["w_branch_a"], lw["w_branch_f"], lw["w_branch_s"], lw["w_out"])
    return expert_choice_ffn(s2d.reshape(b, t, d), lw["norm2_gain"], lw["w_router"],
                             lw["w_exp_gate"], lw["w_exp_up"], lw["w_exp_down"], layer)


def encode(x, meta_tokens, final_gain, layers):
    b, g, d = x.shape
    rows = g // GRID_W
    meta = jnp.broadcast_to(meta_tokens.astype(x.dtype)[None], (b, N_META, d))
    s = jnp.concatenate([x, meta], axis=1)
    for layer, lw in enumerate(layers):
        s = encoder_layer(s, rows, layer, lw)
    return final_norm(s, final_gain, g)


def kernel(x_prompt, x_sample, meta_tokens, norm1_gain, w_in, rel_bias, meta_bias, conv_w, conv_b, a_log, dt_bias,
           d_skip, ssd_norm_gain, w_branch_a, w_branch_f, w_branch_s, w_out, norm2_gain, w_router, w_exp_gate,
           w_exp_up, w_exp_down, final_gain):
    depth = w_in.shape[0]
    rows_set = {x_prompt.shape[1] // GRID_W, x_sample.shape[1] // GRID_W}
    layers = []
    for l in range(depth):
        rb = rel_bias[l]
        layers.append({
            "norm1_gain": norm1_gain[l].astype(f32), "w_in": _reorder_w_in(w_in[l]),
            "rel_bias": rb, "meta_bias": meta_bias[l],
            "conv_w": conv_w[l], "conv_b": conv_b[l], "a_log": a_log[l], "dt_bias": dt_bias[l],
            "d_skip": d_skip[l], "ssd_norm_gain": ssd_norm_gain[l],
            "w_branch_a": w_branch_a[l].astype(bf16), "w_branch_f": w_branch_f[l].astype(bf16),
            "w_branch_s": w_branch_s[l].astype(bf16), "w_out": w_out[l].astype(bf16),
            "norm2_gain": norm2_gain[l], "w_router": w_router[l],
            "w_exp_gate": w_exp_gate, "w_exp_up": w_exp_up, "w_exp_down": w_exp_down,
            "bias_tab": attention_bias_tables(rb, max(rows_set)),
        })
    y_prompt = encode(x_prompt, meta_tokens, final_gain, layers)
    y_sample = encode(x_sample, meta_tokens, final_gain, layers)
    return (y_prompt, y_sample)
```

```python
import functools

import numpy as np
import jax
import jax.numpy as jnp
from jax import lax
from jax.experimental import pallas as pl
from jax.experimental.pallas import tpu as pltpu

f32 = jnp.float32
bf16 = jnp.bfloat16
i32 = jnp.int32
HI = lax.Precision.HIGHEST

D_MODEL = 2048
N_META = 16
GRID_W = 64
ATT_HEADS = 16
ATT_HEAD_DIM = 64
D_ATT = 1024
WIN_R = 8
WIN_C = 16
FNET_GROUPS = 4
D_FNET = 1024
FNET_GROUP_DIM = 256
SSM_HEADS = 16
D_SSM = 1024
SSM_GROUPS = 4
SSM_STATE = 128
SSM_CHUNK = 128
D_SSM_CONV = 2048
N_EXPERTS = 16
EC_CAPACITY = 2
EPS = 1e-6

COL_GATE = 0
COL_QKV = 6144
COL_F = 9216
COL_XBC = 10240
COL_Z = 12288
COL_DT = 13312
N_PROJ = 13824

NEG = -1e30
VMEM_LIMIT = 56 * 1024 * 1024

SLOT_BLK = 256
GATHER_TOK = 512
COMBINE_TOK = 256
COMBINE_ROUNDS = COMBINE_TOK // SLOT_BLK + 1
NO_SLOT_BLK = -2
ROUTE_PAD = 1024
FLAG_FIRST, FLAG_VALID, FLAG_LAST = 1, 2, 4
ATT_RB = 8
ATT_KR = 16


def _cdiv(a, b):
    return -(-a // b)


def _params(*sem):
    return pltpu.CompilerParams(dimension_semantics=sem, vmem_limit_bytes=VMEM_LIMIT)


def _sigmoid(x):
    return 1.0 / (1.0 + jnp.exp(-x))


def _softplus(x):
    return jnp.maximum(x, 0.0) + jnp.log(1.0 + jnp.exp(-jnp.abs(x)))


def _norm_matmul_body(x_ref, g_ref, w_ref, o_ref, h_ref):
    @pl.when(pl.program_id(1) == 0)
    def _():
        x = x_ref[...]
        ms = jnp.mean(x * x, axis=-1, keepdims=True)
        h_ref[...] = (x * lax.rsqrt(ms + EPS) * g_ref[...]).astype(bf16)

    o_ref[...] = jnp.dot(h_ref[...], w_ref[...], preferred_element_type=f32)


def norm_matmul(x2d, gain, w, tm=1024, tn=1536):
    m, d = x2d.shape
    n = w.shape[1]
    assert n % tn == 0
    return pl.pallas_call(
        _norm_matmul_body,
        grid=(_cdiv(m, tm), n // tn),
        in_specs=[pl.BlockSpec((tm, d), lambda i, j: (i, 0)),
                  pl.BlockSpec((1, d), lambda i, j: (0, 0)),
                  pl.BlockSpec((d, tn), lambda i, j: (0, j))],
        out_specs=pl.BlockSpec((tm, tn), lambda i, j: (i, j)),
        out_shape=jax.ShapeDtypeStruct((m, n), f32),
        scratch_shapes=[pltpu.VMEM((tm, d), bf16)],
        compiler_params=_params("parallel", "arbitrary"),
        name="norm_matmul",
    )(x2d, gain.reshape(1, d), w)


def attention_bias_tables(rel_bias, rows):
    rel = rel_bias.astype(f32)
    nh = rel.shape[0]
    cols = []
    for c in range(GRID_W):
        cs = min(max(c - WIN_C // 2, 0), GRID_W - WIN_C)
        j0 = cs - c + (WIN_C - 1)
        cols.append(jnp.pad(rel[:, :, j0:j0 + WIN_C], ((0, 0), (0, 0), (cs, GRID_W - WIN_C - cs)),
                            constant_values=NEG))
    colbias = jnp.transpose(jnp.stack(cols, axis=2), (0, 2, 1, 3))
    lo = ATT_KR - WIN_R
    nd = 2 * WIN_R - 1 + 2 * lo
    flat = jnp.pad(colbias, ((0, 0), (0, 0), (lo, lo), (0, 0)), constant_values=NEG).reshape(nh, GRID_W, nd * GRID_W)
    lane_row = np.arange(ATT_KR * GRID_W) // GRID_W
    tabs = []
    for r0, k0 in ((0, 0), (ATT_RB, ATT_RB - WIN_R // 2), (rows - ATT_RB, rows - ATT_KR)):
        per_row = []
        for rq in range(ATT_RB):
            r = r0 + rq
            rs = min(max(r - WIN_R // 2, 0), rows - WIN_R)
            start = (k0 - r + (WIN_R - 1) + lo) * GRID_W
            seen = (lane_row >= rs - k0) & (lane_row < rs - k0 + WIN_R)
            per_row.append(jnp.where(seen, flat[:, :, start:start + ATT_KR * GRID_W], NEG))
        tabs.append(jnp.stack(per_row, axis=1).reshape(nh, ATT_RB * GRID_W, ATT_KR * GRID_W))
    return jnp.stack(tabs)


def _attn_body(q_ref, *refs, nblk, nk):
    k_refs, v_refs = refs[:nk], refs[nk:2 * nk]
    km_ref, vm_ref, bias_ref, mb_ref, o_ref = refs[2 * nk:]
    i = pl.program_id(1)
    lane = lax.broadcasted_iota(i32, (1, 128), 1)
    q = q_ref[...] * (ATT_HEAD_DIM ** -0.5)
    km = km_ref[...].astype(bf16)
    vm = vm_ref[...]

    @pl.when(i < nblk)
    def _():
        k = jnp.concatenate([r[...] for r in k_refs], axis=0).astype(bf16)
        v = jnp.concatenate([r[...] for r in v_refs], axis=0)
        acc = jnp.zeros(q.shape, f32)
        for h in range(2):
            hm = (lane // ATT_HEAD_DIM) == h
            qh = jnp.where(hm, q, 0.0).astype(bf16)
            s = lax.dot_general(qh, k, (((1,), (1,)), ((), ())), preferred_element_type=f32) + bias_ref[h]
            sm = lax.dot_general(qh, km, (((1,), (1,)), ((), ())), preferred_element_type=f32) + mb_ref[h:h + 1, :]
            mx = jnp.maximum(jnp.max(s, axis=1, keepdims=True), jnp.max(sm, axis=1, keepdims=True))
            p = jnp.exp(s - mx)
            pm = jnp.exp(sm - mx)
            den = jnp.sum(p, axis=1, keepdims=True) + jnp.sum(pm, axis=1, keepdims=True)
            vh = jnp.where(hm, v, 0.0).astype(bf16)
            vmh = jnp.where(hm, vm, 0.0).astype(bf16)
            o = (jnp.dot(p.astype(bf16), vh, preferred_element_type=f32)
                 + jnp.dot(pm.astype(bf16), vmh, preferred_element_type=f32))
            acc = acc + o / den
        o_ref[...] = acc

    @pl.when(i == nblk)
    def _():
        acc = jnp.zeros(q.shape, f32)
        for h in range(2):
            hm = (lane // ATT_HEAD_DIM) == h
            qh = jnp.where(hm, q, 0.0).astype(bf16)
            sm = lax.dot_general(qh, km, (((1,), (1,)), ((), ())), preferred_element_type=f32) + mb_ref[h:h + 1, :]
            mx = jnp.max(sm, axis=1, keepdims=True)
            pm = jnp.exp(sm - mx)
            den = jnp.sum(pm, axis=1, keepdims=True)
            vmh = jnp.where(hm, vm, 0.0).astype(bf16)
            acc = acc + jnp.dot(pm.astype(bf16), vmh, preferred_element_type=f32) / den
        o_ref[...] = acc


def neighbourhood_attention(proj3, bias_tab, meta_bias, rows):
    b, t, _ = proj3.shape
    g = rows * GRID_W
    assert rows % ATT_RB == 0 and rows >= ATT_KR + ATT_RB
    nblk = rows // ATT_RB
    tq = ATT_RB * GRID_W
    half = WIN_R // 2
    assert ATT_RB % half == 0 and ATT_KR % half == 0 and ATT_KR >= ATT_RB + WIN_R - 1
    tk = half * GRID_W
    nk = ATT_KR // half
    qc, kc, vc = COL_QKV // 128, (COL_QKV + D_ATT) // 128, (COL_QKV + 2 * D_ATT) // 128
    nkb = g // tk

    def kstart(i):
        return jnp.clip(i * (ATT_RB // half) - 1, 0, nkb - nk)

    def kspec(j, col):
        return pl.BlockSpec((None, tk, 128), lambda bi, i, p: (bi, kstart(i) + j, col + p))

    def variant(i):
        return jnp.where(i == 0, 0, jnp.where(i >= nblk - 1, 2, 1))

    in_specs = ([pl.BlockSpec((None, tq, 128), lambda bi, i, p: (bi, i, qc + p))]
                + [kspec(j, kc) for j in range(nk)]
                + [kspec(j, vc) for j in range(nk)]
                + [pl.BlockSpec((None, N_META, 128), lambda bi, i, p: (bi, g // N_META, kc + p)),
                   pl.BlockSpec((None, N_META, 128), lambda bi, i, p: (bi, g // N_META, vc + p)),
                   pl.BlockSpec((None, 2, tq, ATT_KR * GRID_W), lambda bi, i, p: (variant(i), p, 0, 0)),
                   pl.BlockSpec((None, 2, N_META), lambda bi, i, p: (p, 0, 0))])
    return pl.pallas_call(
        functools.partial(_attn_body, nblk=nblk, nk=nk),
        grid=(b, nblk + 1, ATT_HEADS // 2),
        in_specs=in_specs,
        out_specs=pl.BlockSpec((None, tq, 128), lambda bi, i, p: (bi, i, p)),
        out_shape=jax.ShapeDtypeStruct((b, t, D_ATT), f32),
        compiler_params=_params("parallel", "arbitrary", "arbitrary"),
        name="nbr_attention",
    )(proj3, *([proj3] * (2 * nk + 2)), bias_tab, meta_bias.astype(f32).reshape(ATT_HEADS // 2, 2, N_META))


def _fnet_factors(t):
    best = None
    for n1 in range(8, t, 8):
        if t % n1 == 0 and (best is None or n1 + t // n1 < best[0] + best[1]):
            best = (n1, t // n1)
    assert best is not None
    return best


def _cos_sin(num, den):
    ang = 2 * np.pi * (num % den).astype(np.float64) / den
    return np.cos(ang), np.sin(ang)


def _fnet_tables(t, shift):
    n1, n2 = _fnet_factors(t)
    n2p = _cdiv(n2, 8) * 8
    t1, t2 = np.arange(n1, dtype=np.int64), np.arange(n2, dtype=np.int64)
    k1, k2 = t1, t2
    c2, s2 = _cos_sin(np.outer(k2 + shift, t2), n2)
    fa = np.zeros((2 * n2p, n2), np.float64)
    fa[:n2], fa[n2p:n2p + n2] = c2, -s2
    tc, ts = _cos_sin(np.outer(t1 + shift, k2 + shift), t)
    c1, s1 = _cos_sin(np.outer(k1, t1 + shift), n1)
    fc = np.block([[c1, s1], [-s1, c1]])
    ch = np.arange(FNET_GROUP_DIM, dtype=np.int64)
    cc, sc = _cos_sin(np.outer(ch, ch), FNET_GROUP_DIM)
    scale = 1.0 / np.sqrt(t * FNET_GROUP_DIM)
    to = lambda a: jnp.asarray(a, f32)
    return (n1, n2, n2p, to(fa), to(tc[:, :, None]), to(ts[:, :, None]), to(fc), to(cc * scale), to(sc * scale))


def _dot_f32(a, b):
    return jnp.dot(a, b, preferred_element_type=f32, precision=HI)


def _fnet_a_body(u_ref, fa_ref, tc_ref, ts_ref, o_ref, *, n2, n2p):
    fa = fa_ref[...]
    for j in range(8):
        r = _dot_f32(fa, u_ref[:, j, :])
        re, im = r[:n2], r[n2p:n2p + n2]
        tc, ts = tc_ref[j], ts_ref[j]
        o_ref[0, j] = (re * tc + im * ts).astype(o_ref.dtype)
        o_ref[1, j] = (im * tc - re * ts).astype(o_ref.dtype)


def _dot_bf16(a, b):
    return jnp.dot(a.astype(bf16), b.astype(bf16), preferred_element_type=f32)


def _fnet_c_body(p_ref, fc_ref, o_ref):
    o_ref[...] = _dot_bf16(fc_ref[...], p_ref[...]).astype(o_ref.dtype)


def _fnet_d_body(q_ref, cc_ref, sc_ref, o_ref):
    cc, sc = cc_ref[...], sc_ref[...]
    for g in range(FNET_GROUPS):
        sl = slice(g * FNET_GROUP_DIM, (g + 1) * FNET_GROUP_DIM)
        o_ref[:, sl] = _dot_bf16(q_ref[0, :, sl], cc) + _dot_bf16(q_ref[1, :, sl], sc)


def fourier_mix(x3, col0, shift):
    b, t, ncols = x3.shape
    d = D_FNET
    n1, n2, n2p, fa, tc, ts, fc, cc, sc = _fnet_tables(t, shift)
    cb = 512
    assert col0 % cb == 0
    const2 = lambda shape: pl.BlockSpec(shape, lambda *_: (0, 0))
    p = pl.pallas_call(
        functools.partial(_fnet_a_body, n2=n2, n2p=n2p),
        grid=(b, n1 // 8, d // cb),
        in_specs=[pl.BlockSpec((None, n2, 8, cb), lambda bi, i, c: (bi, 0, i, col0 // cb + c)),
                  const2((2 * n2p, n2)),
                  pl.BlockSpec((8, n2, 1), lambda bi, i, c: (i, 0, 0)),
                  pl.BlockSpec((8, n2, 1), lambda bi, i, c: (i, 0, 0))],
        out_specs=pl.BlockSpec((None, 2, 8, n2, cb), lambda bi, i, c: (bi, 0, i, 0, c)),
        out_shape=jax.ShapeDtypeStruct((b, 2, n1, n2, d), bf16),
        compiler_params=_params("parallel", "parallel", "parallel"),
        name="fnet_stage_a",
    )(x3.reshape(b, n2, n1, ncols), fa, tc, ts)
    cw = 2048
    ncol = n2 * d
    q = pl.pallas_call(
        _fnet_c_body,
        grid=(b, _cdiv(ncol, cw)),
        in_specs=[pl.BlockSpec((None, 2 * n1, cw), lambda bi, c: (bi, 0, c)),
                  const2((2 * n1, 2 * n1))],
        out_specs=pl.BlockSpec((None, 2 * n1, cw), lambda bi, c: (bi, 0, c)),
        out_shape=jax.ShapeDtypeStruct((b, 2 * n1, ncol), bf16),
        compiler_params=_params("parallel", "parallel"),
        name="fnet_stage_c",
    )(p.reshape(b, 2 * n1, ncol), fc.astype(bf16))
    tm = 512
    gd = (FNET_GROUP_DIM, FNET_GROUP_DIM)
    return pl.pallas_call(
        _fnet_d_body,
        grid=(b, _cdiv(t, tm)),
        in_specs=[pl.BlockSpec((None, 2, tm, d), lambda bi, i: (bi, 0, i, 0)),
                  const2(gd), const2(gd)],
        out_specs=pl.BlockSpec((None, tm, d), lambda bi, i: (bi, i, 0)),
        out_shape=jax.ShapeDtypeStruct((b, t, d), f32),
        compiler_params=_params("parallel", "parallel"),
        name="fnet_stage_d",
    )(q.reshape(b, 2, t, d), cc.astype(bf16), sc.astype(bf16))


def _ssd_body(*refs, d, nchunks):
    if d == 0:
        (xc_ref, dt_ref, alog_ref, dtb_ref, z_ref, yr_ref, dsk_ref, ng_ref, o_ref, st_ref, y_ref) = refs
    else:
        (xbc_ref, prev_ref, next_ref, dt_ref, cw_ref, cb_ref, alog_ref, dtb_ref, o_ref, xc_out_ref, st_ref) = refs
    L = SSM_CHUNK
    j = pl.program_id(1)
    if d == 0:
        ci = jnp.where(j == 0, nchunks, j - 1)
    else:
        ci = jnp.where(j == nchunks, nchunks, nchunks - 1 - j)
    is_meta = ci == nchunks

    @pl.when(j == 0)
    def _():
        st_ref[...] = jnp.zeros(st_ref.shape, f32)

    row = lax.broadcasted_iota(i32, (L, 1), 0)
    nvalid = jnp.where(is_meta, N_META, L)
    valid = row < nvalid

    if d == 0:
        xc = jnp.where(valid, xc_ref[...], 0.0)
    else:
        x = jnp.where(valid, xbc_ref[...], 0.0)
        prev = jnp.where(is_meta, 0.0, prev_ref[7:8, :])
        nxt = jnp.where(ci == nchunks - 1, 0.0, next_ref[0:1, :])
        xp = jnp.where(row == 0, prev, pltpu.roll(x, 1, axis=0))
        xn = jnp.where(row == nvalid - 1, nxt, pltpu.roll(x, L - 1, axis=0))
        w = cw_ref[...]
        pre = w[0:1] * xp + w[1:2] * x + w[2:3] * xn + cb_ref[...]
        xc = jnp.where(valid, pre * _sigmoid(pre), 0.0)
        xc_out_ref[...] = xc
    xs = xc[:, :D_SSM]
    bm = xc[:, D_SSM:D_SSM + SSM_GROUPS * SSM_STATE]
    cm = xc[:, D_SSM + SSM_GROUPS * SSM_STATE:]

    dtf = jnp.where(valid, _softplus(dt_ref[...] + dtb_ref[...]), 0.0)
    da = dtf * (-jnp.exp(alog_ref[...]))
    li = lax.broadcasted_iota(i32, (L, L), 0)
    si = lax.broadcasted_iota(i32, (L, L), 1)
    causal = (si <= li) if d == 0 else (si >= li)
    ac = jnp.dot(causal.astype(f32), da, preferred_element_type=f32, precision=HI)
    act = ac.T
    dtt = dtf.T
    tot = jnp.sum(da, axis=0, keepdims=True)

    lane = lax.broadcasted_iota(i32, (1, 128), 1)
    cbs = {}
    for p in range(SSM_HEADS // 2):
        g = (2 * p) // (SSM_HEADS // SSM_GROUPS)
        bg = bm[:, g * SSM_STATE:(g + 1) * SSM_STATE]
        cg = cm[:, g * SSM_STATE:(g + 1) * SSM_STATE]
        if g not in cbs:
            cbs[g] = lax.dot_general(cg.astype(bf16), bg.astype(bf16), (((1,), (1,)), ((), ())),
                                     preferred_element_type=f32)
        cbg = cbs[g]
        xs_p = xs[:, p * 128:(p + 1) * 128]
        st = st_ref[p]
        y_p = jnp.zeros((L, 128), f32)
        new_st = jnp.zeros((SSM_STATE, 128), f32)
        dec_row = jnp.zeros((1, 128), f32)
        for hh in range(2):
            col = d * SSM_HEADS + 2 * p + hh
            ac_c, ac_r = ac[:, col:col + 1], act[col:col + 1, :]
            dt_c, dt_r = dtf[:, col:col + 1], dtt[col:col + 1, :]
            tot_h = tot[:, col:col + 1]
            hm = (lane // 64) == hh
            m = cbg * jnp.exp(jnp.where(causal, ac_c - ac_r, NEG)) * dt_r
            xm = jnp.where(hm, xs_p, 0.0).astype(bf16)
            stm = jnp.where(hm, st, 0.0).astype(bf16)
            cwt = cg * jnp.exp(ac_c)
            lhs = jnp.concatenate([m, cwt], axis=1).astype(bf16)
            rhs = jnp.concatenate([xm, stm], axis=0)
            y_p = y_p + jnp.dot(lhs, rhs, preferred_element_type=f32)
            bw = (bg * (jnp.exp(tot_h - ac_c) * dt_c)).astype(bf16)
            new_st = new_st + lax.dot_general(bw, xm, (((0,), (0,)), ((), ())), preferred_element_type=f32)
            dec_row = jnp.where(hm, jnp.exp(tot_h), dec_row)
        st_ref[p] = st * dec_row + new_st
        if d == 0:
            y_ref[:, p * 128:(p + 1) * 128] = y_p
        else:
            o_ref[:, p * 128:(p + 1) * 128] = y_p

    if d == 0:
        y = y_ref[...] + yr_ref[...] + xs * dsk_ref[...]
        z = z_ref[...]
        y = y * (z * _sigmoid(z))
        ms = jnp.mean(y * y, axis=-1, keepdims=True)
        o_ref[...] = y * lax.rsqrt(ms + EPS) * ng_ref[...]


def ssd_mixer(proj3, rows, conv_w, conv_b, a_log, dt_bias, d_skip, norm_gain):
    b, t, _ = proj3.shape
    g = rows * GRID_W
    L = SSM_CHUNK
    assert g % L == 0
    nch = g // L
    pad128 = lambda v: jnp.pad(v.astype(f32).reshape(1, -1), ((0, 0), (0, 128 - 2 * SSM_HEADS)))
    alog, dtb = pad128(a_log), pad128(dt_bias)
    cw = conv_w.astype(f32)
    cb = conv_b.astype(f32).reshape(1, -1)
    dsk = jnp.repeat(d_skip.astype(f32), D_SSM // SSM_HEADS).reshape(1, -1)
    ng = norm_gain.astype(f32).reshape(1, -1)
    xcol, zcol, dcol = COL_XBC // D_SSM_CONV, COL_Z // D_SSM, COL_DT // 128
    hb = L // 8

    const = lambda shape: pl.BlockSpec(shape, lambda bi, j: (0,) * len(shape))
    state = pltpu.VMEM((SSM_HEADS // 2, SSM_STATE, 128), f32)
    chunk = lambda cidx, width, col: pl.BlockSpec((None, L, width), lambda bi, j: (bi, cidx(j), col))

    ridx = lambda j: jnp.where(j == nch, nch, nch - 1 - j)

    def pidx(j):
        ci = ridx(j)
        return jnp.where(ci == nch, 0, jnp.where(ci == 0, hb * nch + 1, hb * ci - 1))

    def nidx(j):
        ci = ridx(j)
        return jnp.where(ci >= nch - 1, 0, hb * (ci + 1))

    y_rev, xc = pl.pallas_call(
        functools.partial(_ssd_body, d=1, nchunks=nch),
        grid=(b, nch + 1),
        in_specs=[chunk(ridx, D_SSM_CONV, xcol),
                  pl.BlockSpec((None, 8, D_SSM_CONV), lambda bi, j: (bi, pidx(j), xcol)),
                  pl.BlockSpec((None, 8, D_SSM_CONV), lambda bi, j: (bi, nidx(j), xcol)),
                  chunk(ridx, 128, dcol),
                  const((3, D_SSM_CONV)), const((1, D_SSM_CONV)), const((1, 128)), const((1, 128))],
        out_specs=[chunk(ridx, D_SSM, 0), chunk(ridx, D_SSM_CONV, 0)],
        out_shape=[jax.ShapeDtypeStruct((b, t, D_SSM), f32), jax.ShapeDtypeStruct((b, t, D_SSM_CONV), f32)],
        scratch_shapes=[state],
        compiler_params=_params("parallel", "arbitrary"),
        name="ssd_rev",
    )(proj3, proj3, proj3, proj3, cw, cb, alog, dtb)

    fidx = lambda j: jnp.where(j == 0, nch, j - 1)
    return pl.pallas_call(
        functools.partial(_ssd_body, d=0, nchunks=nch),
        grid=(b, nch + 1),
        in_specs=[chunk(fidx, D_SSM_CONV, 0), chunk(fidx, 128, dcol), const((1, 128)), const((1, 128)),
                  chunk(fidx, D_SSM, zcol), chunk(fidx, D_SSM, 0), const((1, D_SSM)), const((1, D_SSM))],
        out_specs=chunk(fidx, D_SSM, 0),
        out_shape=jax.ShapeDtypeStruct((b, t, D_SSM), f32),
        scratch_shapes=[state, pltpu.VMEM((L, D_SSM), f32)],
        compiler_params=_params("parallel", "arbitrary"),
        name="ssd_fwd",
    )(xc, proj3, alog, dtb, proj3, y_rev, dsk, ng)


def _merge_body(ya, yf, ys, g0, g1, g2, s_ref, wa, wf, ws, wo, o_ref):
    def branch(y, g, w):
        return _sigmoid(g[...]) * jnp.dot(y[...].astype(bf16), w[...], preferred_element_type=f32)

    merged = branch(ya, g0, wa) + branch(yf, g1, wf) + branch(ys, g2, ws)
    o_ref[...] = s_ref[...] + jnp.dot(merged.astype(bf16), wo[...], preferred_element_type=f32)


def merge_branches(s2d, proj2d, ya, yf, ys, wa, wf, ws, wo, tm=256):
    m, d = s2d.shape
    row = lambda w: pl.BlockSpec((tm, w), lambda i: (i, 0))
    gate = lambda k: pl.BlockSpec((tm, d), lambda i: (i, COL_GATE // d + k))
    wspec = lambda r: pl.BlockSpec((r, d), lambda i: (0, 0), pipeline_mode=pl.Buffered(1))
    return pl.pallas_call(
        _merge_body,
        grid=(_cdiv(m, tm),),
        in_specs=[row(D_ATT), row(D_FNET), row(D_SSM), gate(0), gate(1), gate(2), row(d),
                  wspec(D_ATT), wspec(D_FNET), wspec(D_SSM), wspec(d)],
        out_specs=row(d),
        out_shape=jax.ShapeDtypeStruct((m, d), f32),
        compiler_params=_params("parallel"),
        name="merge_branches",
    )(ya, yf, ys, proj2d, proj2d, proj2d, s2d, wa, wf, ws, wo)


def _router_body(s_ref, g_ref, wr_ref, tok_ref, aff_ref, afft_ref, *, m_total, tm):
    i = pl.program_id(0)
    x = s_ref[...]
    ms = jnp.mean(x * x, axis=-1, keepdims=True)
    tok = x * lax.rsqrt(ms + EPS) * g_ref[...]
    row = i * tm + lax.broadcasted_iota(i32, (tm, 1), 0)
    tok = jnp.where(row < m_total, tok, 0.0)
    tok_ref[...] = tok.astype(bf16)
    lg = jnp.dot(tok, wr_ref[...], preferred_element_type=f32, precision=HI)
    ex = jnp.exp(lg - jnp.max(lg, axis=1, keepdims=True))
    aff = ex / jnp.sum(ex, axis=1, keepdims=True)
    aff_ref[...] = aff
    afft_ref[...] = aff.T


def router(s2d, gain, w_router, tm=512):
    m, d = s2d.shape
    mp = _cdiv(m, ROUTE_PAD) * ROUTE_PAD
    last_blk = _cdiv(m, tm) - 1
    return pl.pallas_call(
        functools.partial(_router_body, m_total=m, tm=tm),
        grid=(mp // tm,),
        in_specs=[pl.BlockSpec((tm, d), lambda i: (jnp.minimum(i, last_blk), 0)),
                  pl.BlockSpec((1, d), lambda i: (0, 0)),
                  pl.BlockSpec((d, N_EXPERTS), lambda i: (0, 0))],
        out_specs=[pl.BlockSpec((tm, d), lambda i: (i, 0)),
                   pl.BlockSpec((tm, N_EXPERTS), lambda i: (i, 0)),
                   pl.BlockSpec((N_EXPERTS, tm), lambda i: (0, i))],
        out_shape=[jax.ShapeDtypeStruct((mp, d), bf16),
                   jax.ShapeDtypeStruct((mp, N_EXPERTS), f32),
                   jax.ShapeDtypeStruct((N_EXPERTS, mp), f32)],
        compiler_params=_params("parallel"),
        name="moe_router",
    )(s2d, gain.astype(f32).reshape(1, d), w_router.astype(f32))


def _exclusive_rank(x, nt):
    li = lax.broadcasted_iota(i32, (128, 128), 0)
    lj = lax.broadcasted_iota(i32, (128, 128), 1)
    lane_before = (li < lj).astype(bf16)
    ti = lax.broadcasted_iota(i32, (nt, nt), 0)
    tj = lax.broadcasted_iota(i32, (nt, nt), 1)
    row_before = (tj < ti).astype(bf16)
    xb = x.astype(bf16)
    within = jnp.dot(xb, lane_before, preferred_element_type=f32)
    before = jnp.sum(jnp.dot(row_before, xb, preferred_element_type=f32), axis=1, keepdims=True)
    return within + before


def _count(mask):
    return jnp.sum(jnp.sum(mask.astype(f32), axis=-1, keepdims=True), axis=-2, keepdims=True)


def _select_body(a_ref, sel_ref, *, cap, nt):
    bits = pltpu.bitcast(a_ref[...], i32)

    def step(i, pref):
        cand = pref | jnp.left_shift(jnp.int32(1), 30 - i)
        return jnp.where(_count(bits >= cand) >= cap, cand, pref)

    thr = lax.fori_loop(0, 31, step, jnp.zeros((N_EXPERTS, 1, 1), i32))
    for e in range(N_EXPERTS):
        be = bits[e]
        gt = be > thr[e]
        eq = be == thr[e]
        need = cap - _count(gt)
        take = jnp.logical_and(eq, _exclusive_rank(eq.astype(f32), nt) < need)
        sel_ref[e] = jnp.logical_or(gt, take).astype(f32)


def _rank_body(sel_ref, slot_ref, *, nt):
    for e in range(N_EXPERTS):
        sel = sel_ref[e]
        slot_ref[e] = jnp.where(sel > 0.5, _exclusive_rank(sel, nt), -1.0).astype(i32)


def select_tokens(aff3, cap):
    e, nt, _ = aff3.shape
    return pl.pallas_call(
        functools.partial(_select_body, cap=cap, nt=nt),
        out_shape=jax.ShapeDtypeStruct(aff3.shape, f32),
        compiler_params=pltpu.CompilerParams(vmem_limit_bytes=VMEM_LIMIT),
        name="moe_select",
    )(aff3)


def rank_tokens(sel3):
    e, nt, _ = sel3.shape
    return pl.pallas_call(
        functools.partial(_rank_body, nt=nt),
        out_shape=jax.ShapeDtypeStruct(sel3.shape, i32),
        compiler_params=pltpu.CompilerParams(vmem_limit_bytes=VMEM_LIMIT),
        name="moe_rank",
    )(sel3)


GATE_LANES = 128


def _gather_body(i_ref, j_ref, f_ref, tok_ref, slot_ref, aff_ref, o_ref, g_ref, acc_ref, gacc_ref, *, ns):
    e = pl.program_id(0)
    k = e * ns + pl.program_id(1)
    flags = f_ref[k]

    @pl.when((flags & FLAG_FIRST) != 0)
    def _():
        acc_ref[...] = jnp.zeros(acc_ref.shape, f32)
        gacc_ref[...] = jnp.zeros(gacc_ref.shape, f32)

    @pl.when((flags & FLAG_VALID) != 0)
    def _():
        want = lax.broadcasted_iota(i32, (SLOT_BLK, 1), 0) + j_ref[k] * SLOT_BLK
        onehot = jnp.where(slot_ref[...] == want, 1.0, 0.0).astype(bf16)
        acc_ref[...] += jnp.dot(onehot, tok_ref[...], preferred_element_type=f32)
        pick = lax.broadcasted_iota(i32, (1, N_EXPERTS), 1) == e
        g = jnp.sum(jnp.where(pick, aff_ref[...], 0.0), axis=1, keepdims=True)
        g0 = g.astype(bf16).astype(f32)
        g1 = (g - g0).astype(bf16).astype(f32)
        g2 = g - g0 - g1
        piece = lax.broadcasted_iota(i32, (1, GATE_LANES), 1)
        pieces = jnp.where(piece == 0, g0, jnp.where(piece == 1, g1, jnp.where(piece == 2, g2, 0.0)))
        gacc_ref[...] += jnp.dot(onehot, pieces.astype(bf16), preferred_element_type=f32)

    @pl.when((flags & FLAG_LAST) != 0)
    def _():
        o_ref[...] = acc_ref[...].astype(bf16)
        g_ref[...] = gacc_ref[...]


def gather_tokens(tok, slot_row3, aff, sched, cap_pad, ns):
    d = tok.shape[1]
    ii, jj, ff = sched
    return pl.pallas_call(
        functools.partial(_gather_body, ns=ns),
        grid_spec=pltpu.PrefetchScalarGridSpec(
            num_scalar_prefetch=3,
            grid=(N_EXPERTS, ns),
            in_specs=[pl.BlockSpec((GATHER_TOK, d), lambda e, s, ii, jj, ff: (ii[e * ns + s], 0)),
                      pl.BlockSpec((None, 1, GATHER_TOK), lambda e, s, ii, jj, ff: (e, 0, ii[e * ns + s])),
                      pl.BlockSpec((GATHER_TOK, N_EXPERTS), lambda e, s, ii, jj, ff: (ii[e * ns + s], 0))],
            out_specs=[pl.BlockSpec((None, SLOT_BLK, d), lambda e, s, ii, jj, ff: (e, jj[e * ns + s], 0)),
                       pl.BlockSpec((None, SLOT_BLK, GATE_LANES), lambda e, s, ii, jj, ff: (e, jj[e * ns + s], 0))],
            scratch_shapes=[pltpu.VMEM((SLOT_BLK, d), f32), pltpu.VMEM((SLOT_BLK, GATE_LANES), f32)]),
        out_shape=[jax.ShapeDtypeStruct((N_EXPERTS, cap_pad, d), bf16),
                   jax.ShapeDtypeStruct((N_EXPERTS, cap_pad, GATE_LANES), f32)],
        compiler_params=_params("parallel", "arbitrary"),
        name="moe_gather",
    )(ii, jj, ff, tok, slot_row3, aff)


def _ffn_body(x_ref, g_ref, wg_ref, wu_ref, wd_ref, o_ref, acc_ref):
    fi = pl.program_id(2)

    @pl.when(fi == 0)
    def _():
        acc_ref[...] = jnp.zeros(acc_ref.shape, f32)

    x = x_ref[...]
    gt = jnp.dot(x, wg_ref[...].astype(bf16), preferred_element_type=f32)
    up = jnp.dot(x, wu_ref[...].astype(bf16), preferred_element_type=f32)
    hid = (gt * _sigmoid(gt) * up).astype(bf16)
    acc_ref[...] += jnp.dot(hid, wd_ref[...].astype(bf16), preferred_element_type=f32)

    @pl.when(fi == pl.num_programs(2) - 1)
    def _():
        gate = g_ref[:, 0:1] + g_ref[:, 1:2] + g_ref[:, 2:3]
        o_ref[...] = (acc_ref[...] * gate).T.astype(bf16)


def expert_ffn(xe, gslot, w_gate, w_up, w_down, layer, tf=512, tm_max=768):
    e, cap_pad, d = xe.shape
    dff = w_gate.shape[-1]
    nm = 1
    while cap_pad // nm > tm_max or cap_pad % nm or (cap_pad // nm) % 128:
        nm += 1
    tm = cap_pad // nm
    return pl.pallas_call(
        _ffn_body,
        grid=(e, nm, dff // tf),
        in_specs=[pl.BlockSpec((None, tm, d), lambda ei, mi, fi: (ei, mi, 0)),
                  pl.BlockSpec((None, tm, GATE_LANES), lambda ei, mi, fi: (ei, mi, 0)),
                  pl.BlockSpec((None, None, d, tf), lambda ei, mi, fi: (layer, ei, 0, fi)),
                  pl.BlockSpec((None, None, d, tf), lambda ei, mi, fi: (layer, ei, 0, fi)),
                  pl.BlockSpec((None, None, tf, d), lambda ei, mi, fi: (layer, ei, fi, 0))],
        out_specs=pl.BlockSpec((None, d, tm), lambda ei, mi, fi: (ei, 0, mi)),
        out_shape=jax.ShapeDtypeStruct((e, d, cap_pad), bf16),
        scratch_shapes=[pltpu.VMEM((tm, d), f32)],
        compiler_params=_params("parallel", "parallel", "arbitrary"),
        name="moe_expert_ffn",
    )(xe, gslot, w_gate, w_up, w_down)


def _combine_body(fetch_ref, match_ref, any_ref, s_ref, slot_ref, *rest):
    ye_refs, (o_ref, acc_ref) = rest[:N_EXPERTS], rest[N_EXPERTS:]
    r = pl.program_id(1)
    step = pl.program_id(0) * COMBINE_ROUNDS + r

    @pl.when(r == 0)
    def _():
        acc_ref[...] = jnp.zeros(acc_ref.shape, f32)

    @pl.when(any_ref[step] != 0)
    def _():
        row = lax.broadcasted_iota(i32, (SLOT_BLK, 1), 0)
        total = None
        for e in range(N_EXPERTS):
            want = row + match_ref[step * N_EXPERTS + e] * SLOT_BLK
            onehot = jnp.where(slot_ref[e] == want, 1.0, 0.0).astype(bf16)
            part = jnp.dot(ye_refs[e][...], onehot, preferred_element_type=f32)
            total = part if total is None else part + total
        acc_ref[...] += total

    @pl.when(r == COMBINE_ROUNDS - 1)
    def _():
        o_ref[...] = s_ref[...] + acc_ref[...].T


def combine_tokens(s2d, yet, slot_row3, sched):
    m, d = s2d.shape
    fetch, match, anyv = sched

    def ye_spec(e):
        return pl.BlockSpec((None, d, SLOT_BLK),
                            lambda i, r, fetch, match, anyv: (e, 0, fetch[(i * COMBINE_ROUNDS + r) * N_EXPERTS + e]))

    return pl.pallas_call(
        _combine_body,
        grid_spec=pltpu.PrefetchScalarGridSpec(
            num_scalar_prefetch=3,
            grid=(_cdiv(m, COMBINE_TOK), COMBINE_ROUNDS),
            in_specs=[pl.BlockSpec((COMBINE_TOK, d), lambda i, r, *_: (i, 0)),
                      pl.BlockSpec((N_EXPERTS, 1, COMBINE_TOK), lambda i, r, *_: (0, 0, i))]
                     + [ye_spec(e) for e in range(N_EXPERTS)],
            out_specs=pl.BlockSpec((COMBINE_TOK, d), lambda i, r, *_: (i, 0)),
            scratch_shapes=[pltpu.VMEM((d, COMBINE_TOK), f32)]),
        out_shape=jax.ShapeDtypeStruct((m, d), f32),
        compiler_params=_params("parallel", "arbitrary"),
        name="moe_combine",
    )(fetch, match, anyv, s2d, slot_row3, *([yet] * N_EXPERTS))


def _pair_lists(slot, m, tok_size, n_slot_blk):
    e = slot.shape[0]
    n_tok_blk = _cdiv(m, tok_size)
    cnt = jnp.sum((slot[:, :n_tok_blk * tok_size] >= 0).reshape(e, n_tok_blk, tok_size), axis=-1).astype(i32)
    cum_in = jnp.cumsum(cnt, axis=1)
    cum_ex = cum_in - cnt
    jlo = cum_ex // SLOT_BLK
    jhi = (cum_in - 1) // SLOT_BLK
    npairs = jnp.where(cnt > 0, jhi - jlo + 1, 0)
    off_in = jnp.cumsum(npairs, axis=1)
    off_ex = off_in - npairs
    total = off_in[:, -1:]
    ns = n_tok_blk + n_slot_blk
    step = jnp.arange(ns, dtype=i32)[None, :]
    valid = step < total
    sc = jnp.minimum(step, total - 1)
    tok_blk = jnp.sum(off_in[:, None, :] <= sc[:, :, None], axis=-1).astype(i32)
    tok_blk = jnp.minimum(tok_blk, n_tok_blk - 1)
    slot_blk = (jnp.take_along_axis(jlo, tok_blk, axis=1)
                + sc - jnp.take_along_axis(off_ex, tok_blk, axis=1)).astype(i32)
    return tok_blk, slot_blk, valid


def _gather_schedule(slot, m, n_slot_blk):
    tok_blk, slot_blk, valid = _pair_lists(slot, m, GATHER_TOK, n_slot_blk)
    e, ns = tok_blk.shape
    change = slot_blk[:, 1:] != slot_blk[:, :-1]
    first = jnp.concatenate([jnp.ones((e, 1), bool), change], axis=1)
    last = jnp.concatenate([change | ~valid[:, 1:], jnp.ones((e, 1), bool)], axis=1) & valid
    flags = first * FLAG_FIRST + valid * FLAG_VALID + last * FLAG_LAST
    return tuple(a.reshape(-1).astype(i32) for a in (tok_blk, slot_blk, flags)), ns


def _combine_schedule(slot, m):
    e = slot.shape[0]
    n_tok_blk = _cdiv(m, COMBINE_TOK)
    cnt = jnp.sum((slot[:, :n_tok_blk * COMBINE_TOK] >= 0).reshape(e, n_tok_blk, COMBINE_TOK), axis=-1).astype(i32)
    cum_in = jnp.cumsum(cnt, axis=1)
    jlo = ((cum_in - cnt) // SLOT_BLK)[:, :, None]
    jhi = (cum_in - 1) // SLOT_BLK
    has = (cnt > 0)[:, :, None]
    npairs = jnp.where(has, jhi[:, :, None] - jlo + 1, 0)
    held = lax.cummax(jnp.where(cnt > 0, jhi, 0), axis=1)[:, :, None]
    rnd = jnp.arange(COMBINE_ROUNDS, dtype=i32)[None, None, :]
    valid = rnd < npairs
    fetch = jnp.where(has, jlo + jnp.minimum(rnd, npairs - 1), held)
    match = jnp.where(valid, jlo + rnd, NO_SLOT_BLK)
    flat = lambda a: jnp.transpose(a, (1, 2, 0)).reshape(-1).astype(i32)
    return flat(fetch), flat(match), jnp.any(valid, axis=0).reshape(-1).astype(i32)


def expert_choice_ffn(s3, gain, w_router, w_gate, w_up, w_down, layer):
    b, t, d = s3.shape
    m = b * t
    cap = (EC_CAPACITY * m) // N_EXPERTS
    s2d = s3.reshape(m, d)
    tok, aff, afft = router(s2d, gain, w_router)

    n_pad = _cdiv(m, ROUTE_PAD) * ROUTE_PAD
    nt = n_pad // 128
    a = jnp.roll(afft[:, :m].reshape(N_EXPERTS, b, t), N_META, axis=2).reshape(N_EXPERTS, m)
    a = jnp.pad(a, ((0, 0), (0, n_pad - m)), constant_values=-1.0)
    sel = select_tokens(a.reshape(N_EXPERTS, nt, 128), cap).reshape(N_EXPERTS, n_pad)
    sel = jnp.roll(sel[:, :m].reshape(N_EXPERTS, b, t), -N_META, axis=2).reshape(N_EXPERTS, m)
    sel = jnp.pad(sel, ((0, 0), (0, n_pad - m)))
    slot = rank_tokens(sel.reshape(N_EXPERTS, nt, 128)).reshape(N_EXPERTS, n_pad)

    n_slot_blk = _cdiv(cap, SLOT_BLK)
    cap_pad = n_slot_blk * SLOT_BLK
    g_sched, ns = _gather_schedule(slot, m, n_slot_blk)
    slot_row3 = slot.reshape(N_EXPERTS, 1, n_pad)
    xe, gslot = gather_tokens(tok, slot_row3, aff, g_sched, cap_pad, ns)
    yet = expert_ffn(xe, gslot, w_gate, w_up, w_down, layer)
    out = combine_tokens(s2d, yet, slot_row3, _combine_schedule(slot, m))
    return out.reshape(b, t, d)


def _final_norm_body(x_ref, g_ref, o_ref):
    x = x_ref[...]
    ms = jnp.mean(x * x, axis=-1, keepdims=True)
    o_ref[...] = x * lax.rsqrt(ms + EPS) * g_ref[...]


def final_norm(s3, gain, g, tm=512):
    b, t, d = s3.shape
    assert g % tm == 0
    return pl.pallas_call(
        _final_norm_body,
        grid=(b, g // tm),
        in_specs=[pl.BlockSpec((None, tm, d), lambda bi, i: (bi, i, 0)),
                  pl.BlockSpec((1, d), lambda bi, i: (0, 0))],
        out_specs=pl.BlockSpec((None, tm, d), lambda bi, i: (bi, i, 0)),
        out_shape=jax.ShapeDtypeStruct((b, g, d), f32),
        compiler_params=_params("parallel", "parallel"),
        name="final_norm",
    )(s3, gain.astype(f32).reshape(1, d))


def _reorder_w_in(w):
    cuts = np.cumsum([3 * D_ATT, D_FNET, D_SSM_CONV, D_SSM, 2 * SSM_HEADS])
    qkv, u_f, xbc, z, dt, gate = jnp.split(w, [int(c) for c in cuts], axis=1)
    dt = jnp.pad(dt, ((0, 0), (0, N_PROJ - COL_DT - 2 * SSM_HEADS)))
    return jnp.concatenate([gate, qkv, u_f, xbc, z, dt], axis=1).astype(bf16)


def encoder_layer(s3, rows, layer, lw):
    b, t, d = s3.shape
    m = b * t
    proj = norm_matmul(s3.reshape(m, d), lw["norm1_gain"], lw["w_in"])
    proj3 = proj.reshape(b, t, N_PROJ)
    y_att = neighbourhood_attention(proj3, lw["bias_tab"], lw["meta_bias"], rows)
    y_f = fourier_mix(proj3, COL_F, N_META)
    y_s = ssd_mixer(proj3, rows, lw["conv_w"], lw["conv_b"], lw["a_log"], lw["dt_bias"], lw["d_skip"],
                    lw["ssd_norm_gain"])
    s2d = merge_branches(s3.reshape(m, d), proj, y_att.reshape(m, -1), y_f.reshape(m, -1), y_s.reshape(m, -1),
                         lw["w_branch_a"], lw["w_branch_f"], lw["w_branch_s"], lw["w_out"])
    return expert_choice_ffn(s2d.reshape(b, t, d), lw["norm2_gain"], lw["w_router"],
                             lw["w_exp_gate"], lw["w_exp_up"], lw["w_exp_down"], layer)


def encode(x, meta_tokens, final_gain, layers):
    b, g, d = x.shape
    rows = g // GRID_W
    meta = jnp.broadcast_to(meta_tokens.astype(x.dtype)[None], (b, N_META, d))
    s = jnp.concatenate([x, meta], axis=1)
    for layer, lw in enumerate(layers):
        s = encoder_layer(s, rows, layer, lw)
    return final_norm(s, final_gain, g)


def kernel(x_prompt, x_sample, meta_tokens, norm1_gain, w_in, rel_bias, meta_bias, conv_w, conv_b, a_log, dt_bias,
           d_skip, ssd_norm_gain, w_branch_a, w_branch_f, w_branch_s, w_out, norm2_gain, w_router, w_exp_gate,
           w_exp_up, w_exp_down, final_gain):
    depth = w_in.shape[0]
    rows_set = {x_prompt.shape[1] // GRID_W, x_sample.shape[1] // GRID_W}
    layers = []
    for l in range(depth):
        rb = rel_bias[l]
        layers.append({
            "norm1_gain": norm1_gain[l].astype(f32), "w_in": _reorder_w_in(w_in[l]),
            "rel_bias": rb, "meta_bias": meta_bias[l],
            "conv_w": conv_w[l], "conv_b": conv_b[l], "a_log": a_log[l], "dt_bias": dt_bias[l],
            "d_skip": d_skip[l], "ssd_norm_gain": ssd_norm_gain[l],
            "w_branch_a": w_branch_a[l].astype(bf16), "w_branch_f": w_branch_f[l].astype(bf16),
            "w_branch_s": w_branch_s[l].astype(bf16), "w_out": w_out[l].astype(bf16),
            "norm2_gain": norm2_gain[l], "w_router": w_router[l],
            "w_exp_gate": w_exp_gate, "w_exp_up": w_exp_up, "w_exp_down": w_exp_down,
            "bias_tab": attention_bias_tables(rb, max(rows_set)),
        })
    y_prompt = encode(x_prompt, meta_tokens, final_gain, layers)
    y_sample = encode(x_sample, meta_tokens, final_gain, layers)
    return (y_prompt, y_sample)
```

```python
import functools

import numpy as np
import jax
import jax.numpy as jnp
from jax import lax
from jax.experimental import pallas as pl
from jax.experimental.pallas import tpu as pltpu

f32 = jnp.float32
bf16 = jnp.bfloat16
i32 = jnp.int32
HI = lax.Precision.HIGHEST

D_MODEL = 2048
N_META = 16
GRID_W = 64
ATT_HEADS = 16
ATT_HEAD_DIM = 64
D_ATT = 1024
WIN_R = 8
WIN_C = 16
FNET_GROUPS = 4
D_FNET = 1024
FNET_GROUP_DIM = 256
SSM_HEADS = 16
D_SSM = 1024
SSM_GROUPS = 4
SSM_STATE = 128
SSM_CHUNK = 128
D_SSM_CONV = 2048
N_EXPERTS = 16
EC_CAPACITY = 2
EPS = 1e-6

COL_GATE = 0
COL_QKV = 6144
COL_F = 9216
COL_XBC = 10240
COL_Z = 12288
COL_DT = 13312
N_PROJ = 13824

NEG = -1e30
VMEM_LIMIT = 56 * 1024 * 1024

SLOT_BLK = 256
GATHER_TOK = 1024
COMBINE_TOK = 256
COMBINE_ROUNDS = COMBINE_TOK // SLOT_BLK + 1
NO_SLOT_BLK = -2
ROUTE_PAD = 1024
FLAG_FIRST, FLAG_VALID, FLAG_LAST = 1, 2, 4
ATT_RB = 8
ATT_KR = 16


def _cdiv(a, b):
    return -(-a // b)


def _params(*sem):
    return pltpu.CompilerParams(dimension_semantics=sem, vmem_limit_bytes=VMEM_LIMIT)


def _sigmoid(x):
    return 1.0 / (1.0 + jnp.exp(-x))


def _softplus(x):
    return jnp.maximum(x, 0.0) + jnp.log(1.0 + jnp.exp(-jnp.abs(x)))


def _norm_matmul_body(x_ref, g_ref, w_ref, o_ref, h_ref):
    @pl.when(pl.program_id(1) == 0)
    def _():
        x = x_ref[...]
        ms = jnp.mean(x * x, axis=-1, keepdims=True)
        h_ref[...] = (x * lax.rsqrt(ms + EPS) * g_ref[...]).astype(bf16)

    o_ref[...] = jnp.dot(h_ref[...], w_ref[...], preferred_element_type=f32)


def norm_matmul(x2d, gain, w, tm=1024, tn=1536):
    m, d = x2d.shape
    n = w.shape[1]
    assert n % tn == 0
    return pl.pallas_call(
        _norm_matmul_body,
        grid=(_cdiv(m, tm), n // tn),
        in_specs=[pl.BlockSpec((tm, d), lambda i, j: (i, 0)),
                  pl.BlockSpec((1, d), lambda i, j: (0, 0)),
                  pl.BlockSpec((d, tn), lambda i, j: (0, j))],
        out_specs=pl.BlockSpec((tm, tn), lambda i, j: (i, j)),
        out_shape=jax.ShapeDtypeStruct((m, n), f32),
        scratch_shapes=[pltpu.VMEM((tm, d), bf16)],
        compiler_params=_params("parallel", "arbitrary"),
        name="norm_matmul",
    )(x2d, gain.reshape(1, d), w)


def attention_bias_tables(rel_bias, rows):
    rel = rel_bias.astype(f32)
    nh = rel.shape[0]
    cols = []
    for c in range(GRID_W):
        cs = min(max(c - WIN_C // 2, 0), GRID_W - WIN_C)
        j0 = cs - c + (WIN_C - 1)
        cols.append(jnp.pad(rel[:, :, j0:j0 + WIN_C], ((0, 0), (0, 0), (cs, GRID_W - WIN_C - cs)),
                            constant_values=NEG))
    colbias = jnp.transpose(jnp.stack(cols, axis=2), (0, 2, 1, 3))
    lo = ATT_KR - WIN_R
    nd = 2 * WIN_R - 1 + 2 * lo
    flat = jnp.pad(colbias, ((0, 0), (0, 0), (lo, lo), (0, 0)), constant_values=NEG).reshape(nh, GRID_W, nd * GRID_W)
    lane_row = np.arange(ATT_KR * GRID_W) // GRID_W
    tabs = []
    for r0, k0 in ((0, 0), (ATT_RB, ATT_RB - WIN_R // 2), (rows - ATT_RB, rows - ATT_KR)):
        per_row = []
        for rq in range(ATT_RB):
            r = r0 + rq
            rs = min(max(r - WIN_R // 2, 0), rows - WIN_R)
            start = (k0 - r + (WIN_R - 1) + lo) * GRID_W
            seen = (lane_row >= rs - k0) & (lane_row < rs - k0 + WIN_R)
            per_row.append(jnp.where(seen, flat[:, :, start:start + ATT_KR * GRID_W], NEG))
        tabs.append(jnp.stack(per_row, axis=1).reshape(nh, ATT_RB * GRID_W, ATT_KR * GRID_W))
    return jnp.stack(tabs)


def _attn_body(q_ref, *refs, nblk, nk):
    k_refs, v_refs = refs[:nk], refs[nk:2 * nk]
    km_ref, vm_ref, bias_ref, mb_ref, o_ref = refs[2 * nk:]
    i = pl.program_id(1)
    lane = lax.broadcasted_iota(i32, (1, 128), 1)
    q = q_ref[...] * (ATT_HEAD_DIM ** -0.5)
    km = km_ref[...].astype(bf16)
    vm = vm_ref[...]

    @pl.when(i < nblk)
    def _():
        k = jnp.concatenate([r[...] for r in k_refs], axis=0).astype(bf16)
        v = jnp.concatenate([r[...] for r in v_refs], axis=0)
        acc = jnp.zeros(q.shape, f32)
        for h in range(2):
            hm = (lane // ATT_HEAD_DIM) == h
            qh = jnp.where(hm, q, 0.0).astype(bf16)
            s = lax.dot_general(qh, k, (((1,), (1,)), ((), ())), preferred_element_type=f32) + bias_ref[h]
            sm = lax.dot_general(qh, km, (((1,), (1,)), ((), ())), preferred_element_type=f32) + mb_ref[h:h + 1, :]
            mx = jnp.maximum(jnp.max(s, axis=1, keepdims=True), jnp.max(sm, axis=1, keepdims=True))
            p = jnp.exp(s - mx)
            pm = jnp.exp(sm - mx)
            den = jnp.sum(p, axis=1, keepdims=True) + jnp.sum(pm, axis=1, keepdims=True)
            vh = jnp.where(hm, v, 0.0).astype(bf16)
            vmh = jnp.where(hm, vm, 0.0).astype(bf16)
            o = (jnp.dot(p.astype(bf16), vh, preferred_element_type=f32)
                 + jnp.dot(pm.astype(bf16), vmh, preferred_element_type=f32))
            acc = acc + o / den
        o_ref[...] = acc

    @pl.when(i == nblk)
    def _():
        acc = jnp.zeros(q.shape, f32)
        for h in range(2):
            hm = (lane // ATT_HEAD_DIM) == h
            qh = jnp.where(hm, q, 0.0).astype(bf16)
            sm = lax.dot_general(qh, km, (((1,), (1,)), ((), ())), preferred_element_type=f32) + mb_ref[h:h + 1, :]
            mx = jnp.max(sm, axis=1, keepdims=True)
            pm = jnp.exp(sm - mx)
            den = jnp.sum(pm, axis=1, keepdims=True)
            vmh = jnp.where(hm, vm, 0.0).astype(bf16)
            acc = acc + jnp.dot(pm.astype(bf16), vmh, preferred_element_type=f32) / den
        o_ref[...] = acc


def neighbourhood_attention(proj3, bias_tab, meta_bias, rows):
    b, t, _ = proj3.shape
    g = rows * GRID_W
    assert rows % ATT_RB == 0 and rows >= ATT_KR + ATT_RB
    nblk = rows // ATT_RB
    tq = ATT_RB * GRID_W
    half = WIN_R // 2
    assert ATT_RB % half == 0 and ATT_KR % half == 0 and ATT_KR >= ATT_RB + WIN_R - 1
    tk = half * GRID_W
    nk = ATT_KR // half
    qc, kc, vc = COL_QKV // 128, (COL_QKV + D_ATT) // 128, (COL_QKV + 2 * D_ATT) // 128
    nkb = g // tk

    def kstart(i):
        return jnp.clip(i * (ATT_RB // half) - 1, 0, nkb - nk)

    def kspec(j, col):
        return pl.BlockSpec((None, tk, 128), lambda bi, i, p: (bi, kstart(i) + j, col + p))

    def variant(i):
        return jnp.where(i == 0, 0, jnp.where(i >= nblk - 1, 2, 1))

    in_specs = ([pl.BlockSpec((None, tq, 128), lambda bi, i, p: (bi, i, qc + p))]
                + [kspec(j, kc) for j in range(nk)]
                + [kspec(j, vc) for j in range(nk)]
                + [pl.BlockSpec((None, N_META, 128), lambda bi, i, p: (bi, g // N_META, kc + p)),
                   pl.BlockSpec((None, N_META, 128), lambda bi, i, p: (bi, g // N_META, vc + p)),
                   pl.BlockSpec((None, 2, tq, ATT_KR * GRID_W), lambda bi, i, p: (variant(i), p, 0, 0)),
                   pl.BlockSpec((None, 2, N_META), lambda bi, i, p: (p, 0, 0))])
    return pl.pallas_call(
        functools.partial(_attn_body, nblk=nblk, nk=nk),
        grid=(b, nblk + 1, ATT_HEADS // 2),
        in_specs=in_specs,
        out_specs=pl.BlockSpec((None, tq, 128), lambda bi, i, p: (bi, i, p)),
        out_shape=jax.ShapeDtypeStruct((b, t, D_ATT), f32),
        compiler_params=_params("parallel", "arbitrary", "arbitrary"),
        name="nbr_attention",
    )(proj3, *([proj3] * (2 * nk + 2)), bias_tab, meta_bias.astype(f32).reshape(ATT_HEADS // 2, 2, N_META))


def _fnet_factors(t):
    best = None
    for n1 in range(8, t, 8):
        if t % n1 == 0 and (best is None or n1 + t // n1 < best[0] + best[1]):
            best = (n1, t // n1)
    assert best is not None
    return best


def _cos_sin(num, den):
    ang = 2 * np.pi * (num % den).astype(np.float64) / den
    return np.cos(ang), np.sin(ang)


def _fnet_tables(t, shift):
    n1, n2 = _fnet_factors(t)
    n2p = _cdiv(n2, 8) * 8
    t1, t2 = np.arange(n1, dtype=np.int64), np.arange(n2, dtype=np.int64)
    k1, k2 = t1, t2
    c2, s2 = _cos_sin(np.outer(k2 + shift, t2), n2)
    fa = np.zeros((2 * n2p, n2), np.float64)
    fa[:n2], fa[n2p:n2p + n2] = c2, -s2
    tc, ts = _cos_sin(np.outer(t1 + shift, k2 + shift), t)
    c1, s1 = _cos_sin(np.outer(k1, t1 + shift), n1)
    fc = np.block([[c1, s1], [-s1, c1]])
    ch = np.arange(FNET_GROUP_DIM, dtype=np.int64)
    cc, sc = _cos_sin(np.outer(ch, ch), FNET_GROUP_DIM)
    scale = 1.0 / np.sqrt(t * FNET_GROUP_DIM)
    to = lambda a: jnp.asarray(a, f32)
    return (n1, n2, n2p, to(fa), to(tc[:, :, None]), to(ts[:, :, None]), to(fc), to(cc * scale), to(sc * scale))


def _dot_f32(a, b):
    return jnp.dot(a, b, preferred_element_type=f32, precision=HI)


def _fnet_a_body(u_ref, fa_ref, tc_ref, ts_ref, o_ref, *, n2, n2p):
    fa = fa_ref[...]
    for j in range(8):
        r = _dot_f32(fa, u_ref[:, j, :])
        re, im = r[:n2], r[n2p:n2p + n2]
        tc, ts = tc_ref[j], ts_ref[j]
        o_ref[0, j] = (re * tc + im * ts).astype(o_ref.dtype)
        o_ref[1, j] = (im * tc - re * ts).astype(o_ref.dtype)


def _dot_bf16(a, b):
    return jnp.dot(a.astype(bf16), b.astype(bf16), preferred_element_type=f32)


def _fnet_c_body(p_ref, fc_ref, o_ref):
    o_ref[...] = _dot_bf16(fc_ref[...], p_ref[...]).astype(o_ref.dtype)


def _fnet_d_body(q_ref, cc_ref, sc_ref, o_ref):
    cc, sc = cc_ref[...], sc_ref[...]
    for g in range(FNET_GROUPS):
        sl = slice(g * FNET_GROUP_DIM, (g + 1) * FNET_GROUP_DIM)
        o_ref[:, sl] = _dot_bf16(q_ref[0, :, sl], cc) + _dot_bf16(q_ref[1, :, sl], sc)


def fourier_mix(x3, col0, shift):
    b, t, ncols = x3.shape
    d = D_FNET
    n1, n2, n2p, fa, tc, ts, fc, cc, sc = _fnet_tables(t, shift)
    cb = 512
    assert col0 % cb == 0
    const2 = lambda shape: pl.BlockSpec(shape, lambda *_: (0, 0))
    p = pl.pallas_call(
        functools.partial(_fnet_a_body, n2=n2, n2p=n2p),
        grid=(b, n1 // 8, d // cb),
        in_specs=[pl.BlockSpec((None, n2, 8, cb), lambda bi, i, c: (bi, 0, i, col0 // cb + c)),
                  const2((2 * n2p, n2)),
                  pl.BlockSpec((8, n2, 1), lambda bi, i, c: (i, 0, 0)),
                  pl.BlockSpec((8, n2, 1), lambda bi, i, c: (i, 0, 0))],
        out_specs=pl.BlockSpec((None, 2, 8, n2, cb), lambda bi, i, c: (bi, 0, i, 0, c)),
        out_shape=jax.ShapeDtypeStruct((b, 2, n1, n2, d), bf16),
        compiler_params=_params("parallel", "parallel", "parallel"),
        name="fnet_stage_a",
    )(x3.reshape(b, n2, n1, ncols), fa, tc, ts)
    cw = 2048
    ncol = n2 * d
    q = pl.pallas_call(
        _fnet_c_body,
        grid=(b, _cdiv(ncol, cw)),
        in_specs=[pl.BlockSpec((None, 2 * n1, cw), lambda bi, c: (bi, 0, c)),
                  const2((2 * n1, 2 * n1))],
        out_specs=pl.BlockSpec((None, 2 * n1, cw), lambda bi, c: (bi, 0, c)),
        out_shape=jax.ShapeDtypeStruct((b, 2 * n1, ncol), bf16),
        compiler_params=_params("parallel", "parallel"),
        name="fnet_stage_c",
    )(p.reshape(b, 2 * n1, ncol), fc.astype(bf16))
    tm = 512
    gd = (FNET_GROUP_DIM, FNET_GROUP_DIM)
    return pl.pallas_call(
        _fnet_d_body,
        grid=(b, _cdiv(t, tm)),
        in_specs=[pl.BlockSpec((None, 2, tm, d), lambda bi, i: (bi, 0, i, 0)),
                  const2(gd), const2(gd)],
        out_specs=pl.BlockSpec((None, tm, d), lambda bi, i: (bi, i, 0)),
        out_shape=jax.ShapeDtypeStruct((b, t, d), f32),
        compiler_params=_params("parallel", "parallel"),
        name="fnet_stage_d",
    )(q.reshape(b, 2, t, d), cc.astype(bf16), sc.astype(bf16))


def _ssd_body(*refs, d, nchunks):
    if d == 0:
        (xc_ref, dt_ref, alog_ref, dtb_ref, z_ref, yr_ref, dsk_ref, ng_ref, o_ref, st_ref, y_ref) = refs
    else:
        (xbc_ref, prev_ref, next_ref, dt_ref, cw_ref, cb_ref, alog_ref, dtb_ref, o_ref, xc_out_ref, st_ref) = refs
    L = SSM_CHUNK
    j = pl.program_id(1)
    if d == 0:
        ci = jnp.where(j == 0, nchunks, j - 1)
    else:
        ci = jnp.where(j == nchunks, nchunks, nchunks - 1 - j)
    is_meta = ci == nchunks

    @pl.when(j == 0)
    def _():
        st_ref[...] = jnp.zeros(st_ref.shape, f32)

    row = lax.broadcasted_iota(i32, (L, 1), 0)
    nvalid = jnp.where(is_meta, N_META, L)
    valid = row < nvalid

    if d == 0:
        xc = jnp.where(valid, xc_ref[...], 0.0)
    else:
        x = jnp.where(valid, xbc_ref[...], 0.0)
        prev = jnp.where(is_meta, 0.0, prev_ref[7:8, :])
        nxt = jnp.where(ci == nchunks - 1, 0.0, next_ref[0:1, :])
        xp = jnp.where(row == 0, prev, pltpu.roll(x, 1, axis=0))
        xn = jnp.where(row == nvalid - 1, nxt, pltpu.roll(x, L - 1, axis=0))
        w = cw_ref[...]
        pre = w[0:1] * xp + w[1:2] * x + w[2:3] * xn + cb_ref[...]
        xc = jnp.where(valid, pre * _sigmoid(pre), 0.0)
        xc_out_ref[...] = xc
    xs = xc[:, :D_SSM]
    bm = xc[:, D_SSM:D_SSM + SSM_GROUPS * SSM_STATE]
    cm = xc[:, D_SSM + SSM_GROUPS * SSM_STATE:]

    dtf = jnp.where(valid, _softplus(dt_ref[...] + dtb_ref[...]), 0.0)
    da = dtf * (-jnp.exp(alog_ref[...]))
    li = lax.broadcasted_iota(i32, (L, L), 0)
    si = lax.broadcasted_iota(i32, (L, L), 1)
    causal = (si <= li) if d == 0 else (si >= li)
    ac = jnp.dot(causal.astype(f32), da, preferred_element_type=f32, precision=HI)
    act = ac.T
    dtt = dtf.T
    tot = jnp.sum(da, axis=0, keepdims=True)

    lane = lax.broadcasted_iota(i32, (1, 128), 1)
    cbs = {}
    for p in range(SSM_HEADS // 2):
        g = (2 * p) // (SSM_HEADS // SSM_GROUPS)
        bg = bm[:, g * SSM_STATE:(g + 1) * SSM_STATE]
        cg = cm[:, g * SSM_STATE:(g + 1) * SSM_STATE]
        if g not in cbs:
            cbs[g] = lax.dot_general(cg.astype(bf16), bg.astype(bf16), (((1,), (1,)), ((), ())),
                                     preferred_element_type=f32)
        cbg = cbs[g]
        xs_p = xs[:, p * 128:(p + 1) * 128]
        st = st_ref[p]
        y_p = jnp.zeros((L, 128), f32)
        new_st = jnp.zeros((SSM_STATE, 128), f32)
        dec_row = jnp.zeros((1, 128), f32)
        for hh in range(2):
            col = d * SSM_HEADS + 2 * p + hh
            ac_c, ac_r = ac[:, col:col + 1], act[col:col + 1, :]
            dt_c, dt_r = dtf[:, col:col + 1], dtt[col:col + 1, :]
            tot_h = tot[:, col:col + 1]
            hm = (lane // 64) == hh
            m = cbg * jnp.exp(jnp.where(causal, ac_c - ac_r, NEG)) * dt_r
            xm = jnp.where(hm, xs_p, 0.0).astype(bf16)
            stm = jnp.where(hm, st, 0.0).astype(bf16)
            cwt = cg * jnp.exp(ac_c)
            lhs = jnp.concatenate([m, cwt], axis=1).astype(bf16)
            rhs = jnp.concatenate([xm, stm], axis=0)
            y_p = y_p + jnp.dot(lhs, rhs, preferred_element_type=f32)
            bw = (bg * (jnp.exp(tot_h - ac_c) * dt_c)).astype(bf16)
            new_st = new_st + lax.dot_general(bw, xm, (((0,), (0,)), ((), ())), preferred_element_type=f32)
            dec_row = jnp.where(hm, jnp.exp(tot_h), dec_row)
        st_ref[p] = st * dec_row + new_st
        if d == 0:
            y_ref[:, p * 128:(p + 1) * 128] = y_p
        else:
            o_ref[:, p * 128:(p + 1) * 128] = y_p

    if d == 0:
        y = y_ref[...] + yr_ref[...] + xs * dsk_ref[...]
        z = z_ref[...]
        y = y * (z * _sigmoid(z))
        ms = jnp.mean(y * y, axis=-1, keepdims=True)
        o_ref[...] = y * lax.rsqrt(ms + EPS) * ng_ref[...]


def ssd_mixer(proj3, rows, conv_w, conv_b, a_log, dt_bias, d_skip, norm_gain):
    b, t, _ = proj3.shape
    g = rows * GRID_W
    L = SSM_CHUNK
    assert g % L == 0
    nch = g // L
    pad128 = lambda v: jnp.pad(v.astype(f32).reshape(1, -1), ((0, 0), (0, 128 - 2 * SSM_HEADS)))
    alog, dtb = pad128(a_log), pad128(dt_bias)
    cw = conv_w.astype(f32)
    cb = conv_b.astype(f32).reshape(1, -1)
    dsk = jnp.repeat(d_skip.astype(f32), D_SSM // SSM_HEADS).reshape(1, -1)
    ng = norm_gain.astype(f32).reshape(1, -1)
    xcol, zcol, dcol = COL_XBC // D_SSM_CONV, COL_Z // D_SSM, COL_DT // 128
    hb = L // 8

    const = lambda shape: pl.BlockSpec(shape, lambda bi, j: (0,) * len(shape))
    state = pltpu.VMEM((SSM_HEADS // 2, SSM_STATE, 128), f32)
    chunk = lambda cidx, width, col: pl.BlockSpec((None, L, width), lambda bi, j: (bi, cidx(j), col))

    ridx = lambda j: jnp.where(j == nch, nch, nch - 1 - j)

    def pidx(j):
        ci = ridx(j)
        return jnp.where(ci == nch, 0, jnp.where(ci == 0, hb * nch + 1, hb * ci - 1))

    def nidx(j):
        ci = ridx(j)
        return jnp.where(ci >= nch - 1, 0, hb * (ci + 1))

    y_rev, xc = pl.pallas_call(
        functools.partial(_ssd_body, d=1, nchunks=nch),
        grid=(b, nch + 1),
        in_specs=[chunk(ridx, D_SSM_CONV, xcol),
                  pl.BlockSpec((None, 8, D_SSM_CONV), lambda bi, j: (bi, pidx(j), xcol)),
                  pl.BlockSpec((None, 8, D_SSM_CONV), lambda bi, j: (bi, nidx(j), xcol)),
                  chunk(ridx, 128, dcol),
                  const((3, D_SSM_CONV)), const((1, D_SSM_CONV)), const((1, 128)), const((1, 128))],
        out_specs=[chunk(ridx, D_SSM, 0), chunk(ridx, D_SSM_CONV, 0)],
        out_shape=[jax.ShapeDtypeStruct((b, t, D_SSM), f32), jax.ShapeDtypeStruct((b, t, D_SSM_CONV), f32)],
        scratch_shapes=[state],
        compiler_params=_params("parallel", "arbitrary"),
        name="ssd_rev",
    )(proj3, proj3, proj3, proj3, cw, cb, alog, dtb)

    fidx = lambda j: jnp.where(j == 0, nch, j - 1)
    return pl.pallas_call(
        functools.partial(_ssd_body, d=0, nchunks=nch),
        grid=(b, nch + 1),
        in_specs=[chunk(fidx, D_SSM_CONV, 0), chunk(fidx, 128, dcol), const((1, 128)), const((1, 128)),
                  chunk(fidx, D_SSM, zcol), chunk(fidx, D_SSM, 0), const((1, D_SSM)), const((1, D_SSM))],
        out_specs=chunk(fidx, D_SSM, 0),
        out_shape=jax.ShapeDtypeStruct((b, t, D_SSM), f32),
        scratch_shapes=[state, pltpu.VMEM((L, D_SSM), f32)],
        compiler_params=_params("parallel", "arbitrary"),
        name="ssd_fwd",
    )(xc, proj3, alog, dtb, proj3, y_rev, dsk, ng)


def _merge_body(ya, yf, ys, g0, g1, g2, s_ref, wa, wf, ws, wo, o_ref):
    def branch(y, g, w):
        return _sigmoid(g[...]) * jnp.dot(y[...].astype(bf16), w[...], preferred_element_type=f32)

    merged = branch(ya, g0, wa) + branch(yf, g1, wf) + branch(ys, g2, ws)
    o_ref[...] = s_ref[...] + jnp.dot(merged.astype(bf16), wo[...], preferred_element_type=f32)


def merge_branches(s2d, proj2d, ya, yf, ys, wa, wf, ws, wo, tm=256):
    m, d = s2d.shape
    row = lambda w: pl.BlockSpec((tm, w), lambda i: (i, 0))
    gate = lambda k: pl.BlockSpec((tm, d), lambda i: (i, COL_GATE // d + k))
    wspec = lambda r: pl.BlockSpec((r, d), lambda i: (0, 0), pipeline_mode=pl.Buffered(1))
    return pl.pallas_call(
        _merge_body,
        grid=(_cdiv(m, tm),),
        in_specs=[row(D_ATT), row(D_FNET), row(D_SSM), gate(0), gate(1), gate(2), row(d),
                  wspec(D_ATT), wspec(D_FNET), wspec(D_SSM), wspec(d)],
        out_specs=row(d),
        out_shape=jax.ShapeDtypeStruct((m, d), f32),
        compiler_params=_params("parallel"),
        name="merge_branches",
    )(ya, yf, ys, proj2d, proj2d, proj2d, s2d, wa, wf, ws, wo)


def _router_body(s_ref, g_ref, wr_ref, tok_ref, afft_ref, *, m_total, tm):
    i = pl.program_id(0)
    x = s_ref[...]
    ms = jnp.mean(x * x, axis=-1, keepdims=True)
    tok = x * lax.rsqrt(ms + EPS) * g_ref[...]
    row = i * tm + lax.broadcasted_iota(i32, (tm, 1), 0)
    tok = jnp.where(row < m_total, tok, 0.0)
    tok_ref[...] = tok.astype(bf16)
    lg = jnp.dot(tok, wr_ref[...], preferred_element_type=f32, precision=HI)
    ex = jnp.exp(lg - jnp.max(lg, axis=1, keepdims=True))
    afft_ref[...] = (ex / jnp.sum(ex, axis=1, keepdims=True)).T


def router(s2d, gain, w_router, tm=512):
    m, d = s2d.shape
    mp = _cdiv(m, ROUTE_PAD) * ROUTE_PAD
    last_blk = _cdiv(m, tm) - 1
    return pl.pallas_call(
        functools.partial(_router_body, m_total=m, tm=tm),
        grid=(mp // tm,),
        in_specs=[pl.BlockSpec((tm, d), lambda i: (jnp.minimum(i, last_blk), 0)),
                  pl.BlockSpec((1, d), lambda i: (0, 0)),
                  pl.BlockSpec((d, N_EXPERTS), lambda i: (0, 0))],
        out_specs=[pl.BlockSpec((tm, d), lambda i: (i, 0)),
                   pl.BlockSpec((N_EXPERTS, tm), lambda i: (0, i))],
        out_shape=[jax.ShapeDtypeStruct((mp, d), bf16),
                   jax.ShapeDtypeStruct((N_EXPERTS, mp), f32)],
        compiler_params=_params("parallel"),
        name="moe_router",
    )(s2d, gain.astype(f32).reshape(1, d), w_router.astype(f32))


def _exclusive_rank(x, nt):
    li = lax.broadcasted_iota(i32, (128, 128), 0)
    lj = lax.broadcasted_iota(i32, (128, 128), 1)
    lane_before = (li < lj).astype(bf16)
    ti = lax.broadcasted_iota(i32, (nt, nt), 0)
    tj = lax.broadcasted_iota(i32, (nt, nt), 1)
    row_before = (tj < ti).astype(bf16)
    xb = x.astype(bf16)
    within = jnp.dot(xb, lane_before, preferred_element_type=f32)
    before = jnp.sum(jnp.dot(row_before, xb, preferred_element_type=f32), axis=1, keepdims=True)
    return within + before


def _count(mask):
    return jnp.sum(jnp.sum(mask.astype(f32), axis=-1, keepdims=True), axis=-2, keepdims=True)


def _select_body(a_ref, sel_ref, *, cap, nt):
    bits = pltpu.bitcast(a_ref[...], i32)

    def step(i, pref):
        cand = pref | jnp.left_shift(jnp.int32(1), 30 - i)
        return jnp.where(_count(bits >= cand) >= cap, cand, pref)

    thr = lax.fori_loop(0, 31, step, jnp.zeros((N_EXPERTS, 1, 1), i32))
    for e in range(N_EXPERTS):
        be = bits[e]
        gt = be > thr[e]
        eq = be == thr[e]
        need = cap - _count(gt)
        take = jnp.logical_and(eq, _exclusive_rank(eq.astype(f32), nt) < need)
        sel_ref[e] = jnp.logical_or(gt, take).astype(f32)


def _rank_body(sel_ref, slot_ref, *, nt):
    for e in range(N_EXPERTS):
        sel = sel_ref[e]
        slot_ref[e] = jnp.where(sel > 0.5, _exclusive_rank(sel, nt), -1.0).astype(i32)


def select_tokens(aff3, cap):
    e, nt, _ = aff3.shape
    return pl.pallas_call(
        functools.partial(_select_body, cap=cap, nt=nt),
        out_shape=jax.ShapeDtypeStruct(aff3.shape, f32),
        compiler_params=pltpu.CompilerParams(vmem_limit_bytes=VMEM_LIMIT),
        name="moe_select",
    )(aff3)


def rank_tokens(sel3):
    e, nt, _ = sel3.shape
    return pl.pallas_call(
        functools.partial(_rank_body, nt=nt),
        out_shape=jax.ShapeDtypeStruct(sel3.shape, i32),
        compiler_params=pltpu.CompilerParams(vmem_limit_bytes=VMEM_LIMIT),
        name="moe_rank",
    )(sel3)


GATE_ROWS = 8


def _gather_body(i_ref, j_ref, f_ref, tok_ref, slot_ref, aff_ref, o_ref, g_ref, acc_ref, gacc_ref, *, ns):
    k = pl.program_id(0) * ns + pl.program_id(1)
    flags = f_ref[k]

    @pl.when((flags & FLAG_FIRST) != 0)
    def _():
        acc_ref[...] = jnp.zeros(acc_ref.shape, f32)
        gacc_ref[...] = jnp.zeros(gacc_ref.shape, f32)

    @pl.when((flags & FLAG_VALID) != 0)
    def _():
        want = lax.broadcasted_iota(i32, (SLOT_BLK, 1), 0) + j_ref[k] * SLOT_BLK
        onehot = jnp.where(slot_ref[...] == want, 1.0, 0.0).astype(bf16)
        acc_ref[...] += jnp.dot(onehot, tok_ref[...], preferred_element_type=f32)
        g = aff_ref[...]
        g0 = g.astype(bf16).astype(f32)
        g1 = (g - g0).astype(bf16).astype(f32)
        g2 = g - g0 - g1
        piece = lax.broadcasted_iota(i32, (GATE_ROWS, 1), 0)
        pieces = jnp.where(piece == 0, g0, jnp.where(piece == 1, g1, jnp.where(piece == 2, g2, 0.0)))
        gacc_ref[...] += lax.dot_general(pieces.astype(bf16), onehot, (((1,), (1,)), ((), ())),
                                         preferred_element_type=f32)

    @pl.when((flags & FLAG_LAST) != 0)
    def _():
        o_ref[...] = acc_ref[...].astype(bf16)
        g_ref[...] = gacc_ref[...]


def gather_tokens(tok, slot_row3, afft3, sched, cap_pad, ns):
    d = tok.shape[1]
    ii, jj, ff = sched
    row = pl.BlockSpec((None, 1, GATHER_TOK), lambda e, s, ii, jj, ff: (e, 0, ii[e * ns + s]))
    return pl.pallas_call(
        functools.partial(_gather_body, ns=ns),
        grid_spec=pltpu.PrefetchScalarGridSpec(
            num_scalar_prefetch=3,
            grid=(N_EXPERTS, ns),
            in_specs=[pl.BlockSpec((GATHER_TOK, d), lambda e, s, ii, jj, ff: (ii[e * ns + s], 0)), row, row],
            out_specs=[pl.BlockSpec((None, SLOT_BLK, d), lambda e, s, ii, jj, ff: (e, jj[e * ns + s], 0)),
                       pl.BlockSpec((None, GATE_ROWS, SLOT_BLK), lambda e, s, ii, jj, ff: (e, 0, jj[e * ns + s]))],
            scratch_shapes=[pltpu.VMEM((SLOT_BLK, d), f32), pltpu.VMEM((GATE_ROWS, SLOT_BLK), f32)]),
        out_shape=[jax.ShapeDtypeStruct((N_EXPERTS, cap_pad, d), bf16),
                   jax.ShapeDtypeStruct((N_EXPERTS, GATE_ROWS, cap_pad), f32)],
        compiler_params=_params("parallel", "arbitrary"),
        name="moe_gather",
    )(ii, jj, ff, tok, slot_row3, afft3)


def _ffn_body(x_ref, g_ref, wg_ref, wu_ref, wd_ref, o_ref, acc_ref):
    fi = pl.program_id(2)

    @pl.when(fi == 0)
    def _():
        acc_ref[...] = jnp.zeros(acc_ref.shape, f32)

    x = x_ref[...]
    gt = jnp.dot(x, wg_ref[...].astype(bf16), preferred_element_type=f32)
    up = jnp.dot(x, wu_ref[...].astype(bf16), preferred_element_type=f32)
    hid = (gt * _sigmoid(gt) * up).astype(bf16)
    acc_ref[...] += jnp.dot(hid, wd_ref[...].astype(bf16), preferred_element_type=f32)

    @pl.when(fi == pl.num_programs(2) - 1)
    def _():
        gate = g_ref[0:1, :] + g_ref[1:2, :] + g_ref[2:3, :]
        o_ref[...] = (acc_ref[...].T * gate).astype(bf16)


def expert_ffn(xe, gslot, w_gate, w_up, w_down, layer, tf=512, tm_max=768):
    e, cap_pad, d = xe.shape
    dff = w_gate.shape[-1]
    nm = 1
    while cap_pad // nm > tm_max or cap_pad % nm or (cap_pad // nm) % 128:
        nm += 1
    tm = cap_pad // nm
    return pl.pallas_call(
        _ffn_body,
        grid=(e, nm, dff // tf),
        in_specs=[pl.BlockSpec((None, tm, d), lambda ei, mi, fi: (ei, mi, 0)),
                  pl.BlockSpec((None, GATE_ROWS, tm), lambda ei, mi, fi: (ei, 0, mi)),
                  pl.BlockSpec((None, None, d, tf), lambda ei, mi, fi: (layer, ei, 0, fi)),
                  pl.BlockSpec((None, None, d, tf), lambda ei, mi, fi: (layer, ei, 0, fi)),
                  pl.BlockSpec((None, None, tf, d), lambda ei, mi, fi: (layer, ei, fi, 0))],
        out_specs=pl.BlockSpec((None, d, tm), lambda ei, mi, fi: (ei, 0, mi)),
        out_shape=jax.ShapeDtypeStruct((e, d, cap_pad), bf16),
        scratch_shapes=[pltpu.VMEM((tm, d), f32)],
        compiler_params=_params("parallel", "parallel", "arbitrary"),
        name="moe_expert_ffn",
    )(xe, gslot, w_gate, w_up, w_down)


def _combine_body(fetch_ref, match_ref, any_ref, s_ref, slot_ref, *rest):
    ye_refs, (o_ref, acc_ref) = rest[:N_EXPERTS], rest[N_EXPERTS:]
    r = pl.program_id(1)
    step = pl.program_id(0) * COMBINE_ROUNDS + r

    @pl.when(r == 0)
    def _():
        acc_ref[...] = jnp.zeros(acc_ref.shape, f32)

    @pl.when(any_ref[step] != 0)
    def _():
        row = lax.broadcasted_iota(i32, (SLOT_BLK, 1), 0)
        total = None
        for e in range(N_EXPERTS):
            want = row + match_ref[step * N_EXPERTS + e] * SLOT_BLK
            onehot = jnp.where(slot_ref[e] == want, 1.0, 0.0).astype(bf16)
            part = jnp.dot(ye_refs[e][...], onehot, preferred_element_type=f32)
            total = part if total is None else part + total
        acc_ref[...] += total

    @pl.when(r == COMBINE_ROUNDS - 1)
    def _():
        o_ref[...] = s_ref[...] + acc_ref[...].T


def combine_tokens(s2d, yet, slot_row3, sched):
    m, d = s2d.shape
    fetch, match, anyv = sched

    def ye_spec(e):
        return pl.BlockSpec((None, d, SLOT_BLK),
                            lambda i, r, fetch, match, anyv: (e, 0, fetch[(i * COMBINE_ROUNDS + r) * N_EXPERTS + e]))

    return pl.pallas_call(
        _combine_body,
        grid_spec=pltpu.PrefetchScalarGridSpec(
            num_scalar_prefetch=3,
            grid=(_cdiv(m, COMBINE_TOK), COMBINE_ROUNDS),
            in_specs=[pl.BlockSpec((COMBINE_TOK, d), lambda i, r, *_: (i, 0)),
                      pl.BlockSpec((N_EXPERTS, 1, COMBINE_TOK), lambda i, r, *_: (0, 0, i))]
                     + [ye_spec(e) for e in range(N_EXPERTS)],
            out_specs=pl.BlockSpec((COMBINE_TOK, d), lambda i, r, *_: (i, 0)),
            scratch_shapes=[pltpu.VMEM((d, COMBINE_TOK), f32)]),
        out_shape=jax.ShapeDtypeStruct((m, d), f32),
        compiler_params=_params("parallel", "arbitrary"),
        name="moe_combine",
    )(fetch, match, anyv, s2d, slot_row3, *([yet] * N_EXPERTS))


def _pair_lists(slot, m, tok_size, n_slot_blk):
    e = slot.shape[0]
    n_tok_blk = _cdiv(m, tok_size)
    cnt = jnp.sum((slot[:, :n_tok_blk * tok_size] >= 0).reshape(e, n_tok_blk, tok_size), axis=-1).astype(i32)
    cum_in = jnp.cumsum(cnt, axis=1)
    cum_ex = cum_in - cnt
    jlo = cum_ex // SLOT_BLK
    jhi = (cum_in - 1) // SLOT_BLK
    npairs = jnp.where(cnt > 0, jhi - jlo + 1, 0)
    off_in = jnp.cumsum(npairs, axis=1)
    off_ex = off_in - npairs
    total = off_in[:, -1:]
    ns = n_tok_blk + n_slot_blk
    step = jnp.arange(ns, dtype=i32)[None, :]
    valid = step < total
    sc = jnp.minimum(step, total - 1)
    tok_blk = jnp.sum(off_in[:, None, :] <= sc[:, :, None], axis=-1).astype(i32)
    tok_blk = jnp.minimum(tok_blk, n_tok_blk - 1)
    slot_blk = (jnp.take_along_axis(jlo, tok_blk, axis=1)
                + sc - jnp.take_along_axis(off_ex, tok_blk, axis=1)).astype(i32)
    return tok_blk, slot_blk, valid


def _gather_schedule(slot, m, n_slot_blk):
    tok_blk, slot_blk, valid = _pair_lists(slot, m, GATHER_TOK, n_slot_blk)
    e, ns = tok_blk.shape
    change = slot_blk[:, 1:] != slot_blk[:, :-1]
    first = jnp.concatenate([jnp.ones((e, 1), bool), change], axis=1)
    last = jnp.concatenate([change | ~valid[:, 1:], jnp.ones((e, 1), bool)], axis=1) & valid
    flags = first * FLAG_FIRST + valid * FLAG_VALID + last * FLAG_LAST
    return tuple(a.reshape(-1).astype(i32) for a in (tok_blk, slot_blk, flags)), ns


def _combine_schedule(slot, m):
    e = slot.shape[0]
    n_tok_blk = _cdiv(m, COMBINE_TOK)
    cnt = jnp.sum((slot[:, :n_tok_blk * COMBINE_TOK] >= 0).reshape(e, n_tok_blk, COMBINE_TOK), axis=-1).astype(i32)
    cum_in = jnp.cumsum(cnt, axis=1)
    jlo = ((cum_in - cnt) // SLOT_BLK)[:, :, None]
    jhi = (cum_in - 1) // SLOT_BLK
    has = (cnt > 0)[:, :, None]
    npairs = jnp.where(has, jhi[:, :, None] - jlo + 1, 0)
    held = lax.cummax(jnp.where(cnt > 0, jhi, 0), axis=1)[:, :, None]
    rnd = jnp.arange(COMBINE_ROUNDS, dtype=i32)[None, None, :]
    valid = rnd < npairs
    fetch = jnp.where(has, jlo + jnp.minimum(rnd, npairs - 1), held)
    match = jnp.where(valid, jlo + rnd, NO_SLOT_BLK)
    flat = lambda a: jnp.transpose(a, (1, 2, 0)).reshape(-1).astype(i32)
    return flat(fetch), flat(match), jnp.any(valid, axis=0).reshape(-1).astype(i32)


def expert_choice_ffn(s3, gain, w_router, w_gate, w_up, w_down, layer):
    b, t, d = s3.shape
    m = b * t
    cap = (EC_CAPACITY * m) // N_EXPERTS
    s2d = s3.reshape(m, d)
    tok, afft = router(s2d, gain, w_router)

    n_pad = _cdiv(m, ROUTE_PAD) * ROUTE_PAD
    nt = n_pad // 128
    a = jnp.roll(afft[:, :m].reshape(N_EXPERTS, b, t), N_META, axis=2).reshape(N_EXPERTS, m)
    a = jnp.pad(a, ((0, 0), (0, n_pad - m)), constant_values=-1.0)
    sel = select_tokens(a.reshape(N_EXPERTS, nt, 128), cap).reshape(N_EXPERTS, n_pad)
    sel = jnp.roll(sel[:, :m].reshape(N_EXPERTS, b, t), -N_META, axis=2).reshape(N_EXPERTS, m)
    sel = jnp.pad(sel, ((0, 0), (0, n_pad - m)))
    slot = rank_tokens(sel.reshape(N_EXPERTS, nt, 128)).reshape(N_EXPERTS, n_pad)

    n_slot_blk = _cdiv(cap, SLOT_BLK)
    cap_pad = n_slot_blk * SLOT_BLK
    g_sched, ns = _gather_schedule(slot, m, n_slot_blk)
    slot_row3 = slot.reshape(N_EXPERTS, 1, n_pad)
    xe, gslot = gather_tokens(tok, slot_row3, afft.reshape(N_EXPERTS, 1, n_pad), g_sched, cap_pad, ns)
    yet = expert_ffn(xe, gslot, w_gate, w_up, w_down, layer)
    out = combine_tokens(s2d, yet, slot_row3, _combine_schedule(slot, m))
    return out.reshape(b, t, d)


def _final_norm_body(x_ref, g_ref, o_ref):
    x = x_ref[...]
    ms = jnp.mean(x * x, axis=-1, keepdims=True)
    o_ref[...] = x * lax.rsqrt(ms + EPS) * g_ref[...]


def final_norm(s3, gain, g, tm=512):
    b, t, d = s3.shape
    assert g % tm == 0
    return pl.pallas_call(
        _final_norm_body,
        grid=(b, g // tm),
        in_specs=[pl.BlockSpec((None, tm, d), lambda bi, i: (bi, i, 0)),
                  pl.BlockSpec((1, d), lambda bi, i: (0, 0))],
        out_specs=pl.BlockSpec((None, tm, d), lambda bi, i: (bi, i, 0)),
        out_shape=jax.ShapeDtypeStruct((b, g, d), f32),
        compiler_params=_params("parallel", "parallel"),
        name="final_norm",
    )(s3, gain.astype(f32).reshape(1, d))


def _reorder_w_in(w):
    cuts = np.cumsum([3 * D_ATT, D_FNET, D_SSM_CONV, D_SSM, 2 * SSM_HEADS])
    qkv, u_f, xbc, z, dt, gate = jnp.split(w, [int(c) for c in cuts], axis=1)
    dt = jnp.pad(dt, ((0, 0), (0, N_PROJ - COL_DT - 2 * SSM_HEADS)))
    return jnp.concatenate([gate, qkv, u_f, xbc, z, dt], axis=1).astype(bf16)


def encoder_layer(s3, rows, layer, lw):
    b, t, d = s3.shape
    m = b * t
    proj = norm_matmul(s3.reshape(m, d), lw["norm1_gain"], lw["w_in"])
    proj3 = proj.reshape(b, t, N_PROJ)
    y_att = neighbourhood_attention(proj3, lw["bias_tab"], lw["meta_bias"], rows)
    y_f = fourier_mix(proj3, COL_F, N_META)
    y_s = ssd_mixer(proj3, rows, lw["conv_w"], lw["conv_b"], lw["a_log"], lw["dt_bias"], lw["d_skip"],
                    lw["ssd_norm_gain"])
    s2d = merge_branches(s3.reshape(m, d), proj, y_att.reshape(m, -1), y_f.reshape(m, -1), y_s.reshape(m, -1),
                         lw["w_branch_a"], lw["w_branch_f"], lw["w_branch_s"], lw["w_out"])
    return expert_choice_ffn(s2d.reshape(b, t, d), lw["norm2_gain"], lw["w_router"],
                             lw["w_exp_gate"], lw["w_exp_up"], lw["w_exp_down"], layer)


def encode(x, meta_tokens, final_gain, layers):
    b, g, d = x.shape
    rows = g // GRID_W
    meta = jnp.broadcast_to(meta_tokens.astype(x.dtype)[None], (b, N_META, d))
    s = jnp.concatenate([x, meta], axis=1)
    for layer, lw in enumerate(layers):
        s = encoder_layer(s, rows, layer, lw)
    return final_norm(s, final_gain, g)


def kernel(x_prompt, x_sample, meta_tokens, norm1_gain, w_in, rel_bias, meta_bias, conv_w, conv_b, a_log, dt_bias,
           d_skip, ssd_norm_gain, w_branch_a, w_branch_f, w_branch_s, w_out, norm2_gain, w_router, w_exp_gate,
           w_exp_up, w_exp_down, final_gain):
    depth = w_in.shape[0]
    rows_set = {x_prompt.shape[1] // GRID_W, x_sample.shape[1] // GRID_W}
    layers = []
    for l in range(depth):
        rb = rel_bias[l]
        layers.append({
            "norm1_gain": norm1_gain[l].astype(f32), "w_in": _reorder_w_in(w_in[l]),
            "rel_bias": rb, "meta_bias": meta_bias[l],
            "conv_w": conv_w[l], "conv_b": conv_b[l], "a_log": a_log[l], "dt_bias": dt_bias[l],
            "d_skip": d_skip[l], "ssd_norm_gain": ssd_norm_gain[l],
            "w_branch_a": w_branch_a[l].astype(bf16), "w_branch_f": w_branch_f[l].astype(bf16),
            "w_branch_s": w_branch_s[l].astype(bf16), "w_out": w_out[l].astype(bf16),
            "norm2_gain": norm2_gain[l], "w_router": w_router[l],
            "w_exp_gate": w_exp_gate, "w_exp_up": w_exp_up, "w_exp_down": w_exp_down,
            "bias_tab": attention_bias_tables(rb, max(rows_set)),
        })
    y_prompt = encode(x_prompt, meta_tokens, final_gain, layers)
    y_sample = encode(x_sample, meta_tokens, final_gain, layers)
    return (y_prompt, y_sample)
```

```python
import functools

import numpy as np
import jax
import jax.numpy as jnp
from jax import lax
from jax.experimental import pallas as pl
from jax.experimental.pallas import tpu as pltpu

f32 = jnp.float32
bf16 = jnp.bfloat16
i32 = jnp.int32
HI = lax.Precision.HIGHEST

D_MODEL = 2048
N_META = 16
GRID_W = 64
ATT_HEADS = 16
ATT_HEAD_DIM = 64
D_ATT = 1024
WIN_R = 8
WIN_C = 16
FNET_GROUPS = 4
D_FNET = 1024
FNET_GROUP_DIM = 256
SSM_HEADS = 16
D_SSM = 1024
SSM_GROUPS = 4
SSM_STATE = 128
SSM_CHUNK = 128
D_SSM_CONV = 2048
N_EXPERTS = 16
EC_CAPACITY = 2
EPS = 1e-6

COL_GATE = 0
COL_QKV = 6144
COL_F = 9216
COL_XBC = 10240
COL_Z = 12288
COL_DT = 13312
N_PROJ = 13824

NEG = -1e30
VMEM_LIMIT = 56 * 1024 * 1024

SLOT_BLK = 256
GATHER_TOK = 1024
COMBINE_TOK = 256
COMBINE_ROUNDS = COMBINE_TOK // SLOT_BLK + 1
NO_SLOT_BLK = -2
ROUTE_PAD = 1024
FLAG_FIRST, FLAG_VALID, FLAG_LAST = 1, 2, 4
ATT_RB = 8
ATT_KR = 16


def _cdiv(a, b):
    return -(-a // b)


def _params(*sem):
    return pltpu.CompilerParams(dimension_semantics=sem, vmem_limit_bytes=VMEM_LIMIT)


def _sigmoid(x):
    return 1.0 / (1.0 + jnp.exp(-x))


def _softplus(x):
    return jnp.maximum(x, 0.0) + jnp.log(1.0 + jnp.exp(-jnp.abs(x)))


def _norm_matmul_body(x_ref, g_ref, w_ref, o_ref, h_ref):
    @pl.when(pl.program_id(1) == 0)
    def _():
        x = x_ref[...]
        ms = jnp.mean(x * x, axis=-1, keepdims=True)
        h_ref[...] = (x * lax.rsqrt(ms + EPS) * g_ref[...]).astype(bf16)

    o_ref[...] = jnp.dot(h_ref[...], w_ref[...], preferred_element_type=f32)


def norm_matmul(x2d, gain, w, tm=1024, tn=1536):
    m, d = x2d.shape
    n = w.shape[1]
    assert n % tn == 0
    return pl.pallas_call(
        _norm_matmul_body,
        grid=(_cdiv(m, tm), n // tn),
        in_specs=[pl.BlockSpec((tm, d), lambda i, j: (i, 0)),
                  pl.BlockSpec((1, d), lambda i, j: (0, 0)),
                  pl.BlockSpec((d, tn), lambda i, j: (0, j))],
        out_specs=pl.BlockSpec((tm, tn), lambda i, j: (i, j)),
        out_shape=jax.ShapeDtypeStruct((m, n), f32),
        scratch_shapes=[pltpu.VMEM((tm, d), bf16)],
        compiler_params=_params("parallel", "arbitrary"),
        name="norm_matmul",
    )(x2d, gain.reshape(1, d), w)


def attention_bias_tables(rel_bias, rows):
    rel = rel_bias.astype(f32)
    nh = rel.shape[0]
    cols = []
    for c in range(GRID_W):
        cs = min(max(c - WIN_C // 2, 0), GRID_W - WIN_C)
        j0 = cs - c + (WIN_C - 1)
        cols.append(jnp.pad(rel[:, :, j0:j0 + WIN_C], ((0, 0), (0, 0), (cs, GRID_W - WIN_C - cs)),
                            constant_values=NEG))
    colbias = jnp.transpose(jnp.stack(cols, axis=2), (0, 2, 1, 3))
    lo = ATT_KR - WIN_R
    nd = 2 * WIN_R - 1 + 2 * lo
    flat = jnp.pad(colbias, ((0, 0), (0, 0), (lo, lo), (0, 0)), constant_values=NEG).reshape(nh, GRID_W, nd * GRID_W)
    lane_row = np.arange(ATT_KR * GRID_W) // GRID_W
    tabs = []
    for r0, k0 in ((0, 0), (ATT_RB, ATT_RB - WIN_R // 2), (rows - ATT_RB, rows - ATT_KR)):
        per_row = []
        for rq in range(ATT_RB):
            r = r0 + rq
            rs = min(max(r - WIN_R // 2, 0), rows - WIN_R)
            start = (k0 - r + (WIN_R - 1) + lo) * GRID_W
            seen = (lane_row >= rs - k0) & (lane_row < rs - k0 + WIN_R)
            per_row.append(jnp.where(seen, flat[:, :, start:start + ATT_KR * GRID_W], NEG))
        tabs.append(jnp.stack(per_row, axis=1).reshape(nh, ATT_RB * GRID_W, ATT_KR * GRID_W))
    return jnp.stack(tabs)


def _attn_body(q_ref, *refs, nblk, nk):
    k_refs, v_refs = refs[:nk], refs[nk:2 * nk]
    km_ref, vm_ref, bias_ref, mb_ref, o_ref = refs[2 * nk:]
    i = pl.program_id(2)
    lane = lax.broadcasted_iota(i32, (1, 128), 1)
    q = q_ref[...] * (ATT_HEAD_DIM ** -0.5)
    km = km_ref[...].astype(bf16)
    vm = vm_ref[...]

    @pl.when(i < nblk)
    def _():
        k = jnp.concatenate([r[...] for r in k_refs], axis=0).astype(bf16)
        v = jnp.concatenate([r[...] for r in v_refs], axis=0)
        acc = jnp.zeros(q.shape, f32)
        for h in range(2):
            hm = (lane // ATT_HEAD_DIM) == h
            qh = jnp.where(hm, q, 0.0).astype(bf16)
            s = lax.dot_general(qh, k, (((1,), (1,)), ((), ())), preferred_element_type=f32) + bias_ref[h]
            sm = lax.dot_general(qh, km, (((1,), (1,)), ((), ())), preferred_element_type=f32) + mb_ref[h:h + 1, :]
            mx = jnp.maximum(jnp.max(s, axis=1, keepdims=True), jnp.max(sm, axis=1, keepdims=True))
            p = jnp.exp(s - mx)
            pm = jnp.exp(sm - mx)
            den = jnp.sum(p, axis=1, keepdims=True) + jnp.sum(pm, axis=1, keepdims=True)
            vh = jnp.where(hm, v, 0.0).astype(bf16)
            vmh = jnp.where(hm, vm, 0.0).astype(bf16)
            o = (jnp.dot(p.astype(bf16), vh, preferred_element_type=f32)
                 + jnp.dot(pm.astype(bf16), vmh, preferred_element_type=f32))
            acc = acc + o / den
        o_ref[...] = acc

    @pl.when(i == nblk)
    def _():
        acc = jnp.zeros(q.shape, f32)
        for h in range(2):
            hm = (lane // ATT_HEAD_DIM) == h
            qh = jnp.where(hm, q, 0.0).astype(bf16)
            sm = lax.dot_general(qh, km, (((1,), (1,)), ((), ())), preferred_element_type=f32) + mb_ref[h:h + 1, :]
            mx = jnp.max(sm, axis=1, keepdims=True)
            pm = jnp.exp(sm - mx)
            den = jnp.sum(pm, axis=1, keepdims=True)
            vmh = jnp.where(hm, vm, 0.0).astype(bf16)
            acc = acc + jnp.dot(pm.astype(bf16), vmh, preferred_element_type=f32) / den
        o_ref[...] = acc


def neighbourhood_attention(proj3, bias_tab, meta_bias, rows):
    b, t, _ = proj3.shape
    g = rows * GRID_W
    assert rows % ATT_RB == 0 and rows >= ATT_KR + ATT_RB
    nblk = rows // ATT_RB
    tq = ATT_RB * GRID_W
    half = WIN_R // 2
    assert ATT_RB % half == 0 and ATT_KR % half == 0 and ATT_KR >= ATT_RB + WIN_R - 1
    tk = half * GRID_W
    nk = ATT_KR // half
    qc, kc, vc = COL_QKV // 128, (COL_QKV + D_ATT) // 128, (COL_QKV + 2 * D_ATT) // 128
    nkb = g // tk

    def kstart(i):
        return jnp.clip(i * (ATT_RB // half) - 1, 0, nkb - nk)

    def kspec(j, col):
        return pl.BlockSpec((None, tk, 128), lambda bi, p, i: (bi, kstart(i) + j, col + p))

    def variant(i):
        return jnp.where(i == 0, 0, jnp.where(i >= nblk - 1, 2, 1))

    in_specs = ([pl.BlockSpec((None, tq, 128), lambda bi, p, i: (bi, i, qc + p))]
                + [kspec(j, kc) for j in range(nk)]
                + [kspec(j, vc) for j in range(nk)]
                + [pl.BlockSpec((None, N_META, 128), lambda bi, p, i: (bi, g // N_META, kc + p)),
                   pl.BlockSpec((None, N_META, 128), lambda bi, p, i: (bi, g // N_META, vc + p)),
                   pl.BlockSpec((None, 2, tq, ATT_KR * GRID_W), lambda bi, p, i: (variant(i), p, 0, 0)),
                   pl.BlockSpec((None, 2, N_META), lambda bi, p, i: (p, 0, 0))])
    return pl.pallas_call(
        functools.partial(_attn_body, nblk=nblk, nk=nk),
        grid=(b, ATT_HEADS // 2, nblk + 1),
        in_specs=in_specs,
        out_specs=pl.BlockSpec((None, tq, 128), lambda bi, p, i: (bi, i, p)),
        out_shape=jax.ShapeDtypeStruct((b, t, D_ATT), f32),
        compiler_params=_params("parallel", "arbitrary", "arbitrary"),
        name="nbr_attention",
    )(proj3, *([proj3] * (2 * nk + 2)), bias_tab, meta_bias.astype(f32).reshape(ATT_HEADS // 2, 2, N_META))


def _fnet_factors(t):
    best = None
    for n1 in range(8, t, 8):
        if t % n1 == 0 and (best is None or n1 + t // n1 < best[0] + best[1]):
            best = (n1, t // n1)
    assert best is not None
    return best


def _cos_sin(num, den):
    ang = 2 * np.pi * (num % den).astype(np.float64) / den
    return np.cos(ang), np.sin(ang)


def _fnet_tables(t, shift):
    n1, n2 = _fnet_factors(t)
    n2p = _cdiv(n2, 8) * 8
    t1, t2 = np.arange(n1, dtype=np.int64), np.arange(n2, dtype=np.int64)
    k1, k2 = t1, t2
    c2, s2 = _cos_sin(np.outer(k2 + shift, t2), n2)
    fa = np.zeros((2 * n2p, n2), np.float64)
    fa[:n2], fa[n2p:n2p + n2] = c2, -s2
    tc, ts = _cos_sin(np.outer(t1 + shift, k2 + shift), t)
    c1, s1 = _cos_sin(np.outer(k1, t1 + shift), n1)
    fc = np.block([[c1, s1], [-s1, c1]])
    ch = np.arange(FNET_GROUP_DIM, dtype=np.int64)
    cc, sc = _cos_sin(np.outer(ch, ch), FNET_GROUP_DIM)
    scale = 1.0 / np.sqrt(t * FNET_GROUP_DIM)
    to = lambda a: jnp.asarray(a, f32)
    return (n1, n2, n2p, to(fa), to(tc[:, :, None]), to(ts[:, :, None]), to(fc), to(cc * scale), to(sc * scale))


def _dot_f32(a, b):
    return jnp.dot(a, b, preferred_element_type=f32, precision=HI)


def _fnet_a_body(u_ref, fa_ref, tc_ref, ts_ref, o_ref, *, n2, n2p):
    fa = fa_ref[...]
    for j in range(8):
        r = _dot_f32(fa, u_ref[:, j, :])
        re, im = r[:n2], r[n2p:n2p + n2]
        tc, ts = tc_ref[j], ts_ref[j]
        o_ref[0, j] = (re * tc + im * ts).astype(o_ref.dtype)
        o_ref[1, j] = (im * tc - re * ts).astype(o_ref.dtype)


def _dot_bf16(a, b):
    return jnp.dot(a.astype(bf16), b.astype(bf16), preferred_element_type=f32)


def _fnet_c_body(p_ref, fc_ref, o_ref):
    o_ref[...] = _dot_bf16(fc_ref[...], p_ref[...]).astype(o_ref.dtype)


def _fnet_d_body(q_ref, cc_ref, sc_ref, o_ref):
    cc, sc = cc_ref[...], sc_ref[...]
    for g in range(FNET_GROUPS):
        sl = slice(g * FNET_GROUP_DIM, (g + 1) * FNET_GROUP_DIM)
        o_ref[:, sl] = _dot_bf16(q_ref[0, :, sl], cc) + _dot_bf16(q_ref[1, :, sl], sc)


def fourier_mix(x3, col0, shift):
    b, t, ncols = x3.shape
    d = D_FNET
    n1, n2, n2p, fa, tc, ts, fc, cc, sc = _fnet_tables(t, shift)
    cb = 512
    assert col0 % cb == 0
    const2 = lambda shape: pl.BlockSpec(shape, lambda *_: (0, 0))
    p = pl.pallas_call(
        functools.partial(_fnet_a_body, n2=n2, n2p=n2p),
        grid=(b, n1 // 8, d // cb),
        in_specs=[pl.BlockSpec((None, n2, 8, cb), lambda bi, i, c: (bi, 0, i, col0 // cb + c)),
                  const2((2 * n2p, n2)),
                  pl.BlockSpec((8, n2, 1), lambda bi, i, c: (i, 0, 0)),
                  pl.BlockSpec((8, n2, 1), lambda bi, i, c: (i, 0, 0))],
        out_specs=pl.BlockSpec((None, 2, 8, n2, cb), lambda bi, i, c: (bi, 0, i, 0, c)),
        out_shape=jax.ShapeDtypeStruct((b, 2, n1, n2, d), bf16),
        compiler_params=_params("parallel", "parallel", "parallel"),
        name="fnet_stage_a",
    )(x3.reshape(b, n2, n1, ncols), fa, tc, ts)
    cw = 2048
    ncol = n2 * d
    q = pl.pallas_call(
        _fnet_c_body,
        grid=(b, _cdiv(ncol, cw)),
        in_specs=[pl.BlockSpec((None, 2 * n1, cw), lambda bi, c: (bi, 0, c)),
                  const2((2 * n1, 2 * n1))],
        out_specs=pl.BlockSpec((None, 2 * n1, cw), lambda bi, c: (bi, 0, c)),
        out_shape=jax.ShapeDtypeStruct((b, 2 * n1, ncol), bf16),
        compiler_params=_params("parallel", "parallel"),
        name="fnet_stage_c",
    )(p.reshape(b, 2 * n1, ncol), fc.astype(bf16))
    tm = 512
    gd = (FNET_GROUP_DIM, FNET_GROUP_DIM)
    return pl.pallas_call(
        _fnet_d_body,
        grid=(b, _cdiv(t, tm)),
        in_specs=[pl.BlockSpec((None, 2, tm, d), lambda bi, i: (bi, 0, i, 0)),
                  const2(gd), const2(gd)],
        out_specs=pl.BlockSpec((None, tm, d), lambda bi, i: (bi, i, 0)),
        out_shape=jax.ShapeDtypeStruct((b, t, d), f32),
        compiler_params=_params("parallel", "parallel"),
        name="fnet_stage_d",
    )(q.reshape(b, 2, t, d), cc.astype(bf16), sc.astype(bf16))


def _ssd_body(*refs, d, nchunks):
    if d == 0:
        (xc_ref, dt_ref, alog_ref, dtb_ref, z_ref, yr_ref, dsk_ref, ng_ref, o_ref, st_ref, y_ref) = refs
    else:
        (xbc_ref, prev_ref, next_ref, dt_ref, cw_ref, cb_ref, alog_ref, dtb_ref, o_ref, xc_out_ref, st_ref) = refs
    L = SSM_CHUNK
    j = pl.program_id(1)
    if d == 0:
        ci = jnp.where(j == 0, nchunks, j - 1)
    else:
        ci = jnp.where(j == nchunks, nchunks, nchunks - 1 - j)
    is_meta = ci == nchunks

    @pl.when(j == 0)
    def _():
        st_ref[...] = jnp.zeros(st_ref.shape, f32)

    row = lax.broadcasted_iota(i32, (L, 1), 0)
    nvalid = jnp.where(is_meta, N_META, L)
    valid = row < nvalid

    if d == 0:
        xc = jnp.where(valid, xc_ref[...], 0.0)
    else:
        x = jnp.where(valid, xbc_ref[...], 0.0)
        prev = jnp.where(is_meta, 0.0, prev_ref[7:8, :])
        nxt = jnp.where(ci == nchunks - 1, 0.0, next_ref[0:1, :])
        xp = jnp.where(row == 0, prev, pltpu.roll(x, 1, axis=0))
        xn = jnp.where(row == nvalid - 1, nxt, pltpu.roll(x, L - 1, axis=0))
        w = cw_ref[...]
        pre = w[0:1] * xp + w[1:2] * x + w[2:3] * xn + cb_ref[...]
        xc = jnp.where(valid, pre * _sigmoid(pre), 0.0)
        xc_out_ref[...] = xc
    xs = xc[:, :D_SSM]
    bm = xc[:, D_SSM:D_SSM + SSM_GROUPS * SSM_STATE]
    cm = xc[:, D_SSM + SSM_GROUPS * SSM_STATE:]

    dtf = jnp.where(valid, _softplus(dt_ref[...] + dtb_ref[...]), 0.0)
    da = dtf * (-jnp.exp(alog_ref[...]))
    li = lax.broadcasted_iota(i32, (L, L), 0)
    si = lax.broadcasted_iota(i32, (L, L), 1)
    causal = (si <= li) if d == 0 else (si >= li)
    ac = jnp.dot(causal.astype(f32), da, preferred_element_type=f32, precision=HI)
    act = ac.T
    dtt = dtf.T
    tot = jnp.sum(da, axis=0, keepdims=True)

    lane = lax.broadcasted_iota(i32, (1, 128), 1)
    cbs = {}
    for p in range(SSM_HEADS // 2):
        g = (2 * p) // (SSM_HEADS // SSM_GROUPS)
        bg = bm[:, g * SSM_STATE:(g + 1) * SSM_STATE]
        cg = cm[:, g * SSM_STATE:(g + 1) * SSM_STATE]
        if g not in cbs:
            cbs[g] = lax.dot_general(cg.astype(bf16), bg.astype(bf16), (((1,), (1,)), ((), ())),
                                     preferred_element_type=f32)
        cbg = cbs[g]
        xs_p = xs[:, p * 128:(p + 1) * 128]
        st = st_ref[p]
        y_p = jnp.zeros((L, 128), f32)
        new_st = jnp.zeros((SSM_STATE, 128), f32)
        dec_row = jnp.zeros((1, 128), f32)
        for hh in range(2):
            col = d * SSM_HEADS + 2 * p + hh
            ac_c, ac_r = ac[:, col:col + 1], act[col:col + 1, :]
            dt_c, dt_r = dtf[:, col:col + 1], dtt[col:col + 1, :]
            tot_h = tot[:, col:col + 1]
            hm = (lane // 64) == hh
            m = cbg * jnp.exp(jnp.where(causal, ac_c - ac_r, NEG)) * dt_r
            xm = jnp.where(hm, xs_p, 0.0).astype(bf16)
            stm = jnp.where(hm, st, 0.0).astype(bf16)
            cwt = cg * jnp.exp(ac_c)
            lhs = jnp.concatenate([m, cwt], axis=1).astype(bf16)
            rhs = jnp.concatenate([xm, stm], axis=0)
            y_p = y_p + jnp.dot(lhs, rhs, preferred_element_type=f32)
            bw = (bg * (jnp.exp(tot_h - ac_c) * dt_c)).astype(bf16)
            new_st = new_st + lax.dot_general(bw, xm, (((0,), (0,)), ((), ())), preferred_element_type=f32)
            dec_row = jnp.where(hm, jnp.exp(tot_h), dec_row)
        st_ref[p] = st * dec_row + new_st
        if d == 0:
            y_ref[:, p * 128:(p + 1) * 128] = y_p
        else:
            o_ref[:, p * 128:(p + 1) * 128] = y_p

    if d == 0:
        y = y_ref[...] + yr_ref[...] + xs * dsk_ref[...]
        z = z_ref[...]
        y = y * (z * _sigmoid(z))
        ms = jnp.mean(y * y, axis=-1, keepdims=True)
        o_ref[...] = y * lax.rsqrt(ms + EPS) * ng_ref[...]


def ssd_mixer(proj3, rows, conv_w, conv_b, a_log, dt_bias, d_skip, norm_gain):
    b, t, _ = proj3.shape
    g = rows * GRID_W
    L = SSM_CHUNK
    assert g % L == 0
    nch = g // L
    pad128 = lambda v: jnp.pad(v.astype(f32).reshape(1, -1), ((0, 0), (0, 128 - 2 * SSM_HEADS)))
    alog, dtb = pad128(a_log), pad128(dt_bias)
    cw = conv_w.astype(f32)
    cb = conv_b.astype(f32).reshape(1, -1)
    dsk = jnp.repeat(d_skip.astype(f32), D_SSM // SSM_HEADS).reshape(1, -1)
    ng = norm_gain.astype(f32).reshape(1, -1)
    xcol, zcol, dcol = COL_XBC // D_SSM_CONV, COL_Z // D_SSM, COL_DT // 128
    hb = L // 8

    const = lambda shape: pl.BlockSpec(shape, lambda bi, j: (0,) * len(shape))
    state = pltpu.VMEM((SSM_HEADS // 2, SSM_STATE, 128), f32)
    chunk = lambda cidx, width, col: pl.BlockSpec((None, L, width), lambda bi, j: (bi, cidx(j), col))

    ridx = lambda j: jnp.where(j == nch, nch, nch - 1 - j)

    def pidx(j):
        ci = ridx(j)
        return jnp.where(ci == nch, 0, jnp.where(ci == 0, hb * nch + 1, hb * ci - 1))

    def nidx(j):
        ci = ridx(j)
        return jnp.where(ci >= nch - 1, 0, hb * (ci + 1))

    y_rev, xc = pl.pallas_call(
        functools.partial(_ssd_body, d=1, nchunks=nch),
        grid=(b, nch + 1),
        in_specs=[chunk(ridx, D_SSM_CONV, xcol),
                  pl.BlockSpec((None, 8, D_SSM_CONV), lambda bi, j: (bi, pidx(j), xcol)),
                  pl.BlockSpec((None, 8, D_SSM_CONV), lambda bi, j: (bi, nidx(j), xcol)),
                  chunk(ridx, 128, dcol),
                  const((3, D_SSM_CONV)), const((1, D_SSM_CONV)), const((1, 128)), const((1, 128))],
        out_specs=[chunk(ridx, D_SSM, 0), chunk(ridx, D_SSM_CONV, 0)],
        out_shape=[jax.ShapeDtypeStruct((b, t, D_SSM), f32), jax.ShapeDtypeStruct((b, t, D_SSM_CONV), f32)],
        scratch_shapes=[state],
        compiler_params=_params("parallel", "arbitrary"),
        name="ssd_rev",
    )(proj3, proj3, proj3, proj3, cw, cb, alog, dtb)

    fidx = lambda j: jnp.where(j == 0, nch, j - 1)
    return pl.pallas_call(
        functools.partial(_ssd_body, d=0, nchunks=nch),
        grid=(b, nch + 1),
        in_specs=[chunk(fidx, D_SSM_CONV, 0), chunk(fidx, 128, dcol), const((1, 128)), const((1, 128)),
                  chunk(fidx, D_SSM, zcol), chunk(fidx, D_SSM, 0), const((1, D_SSM)), const((1, D_SSM))],
        out_specs=chunk(fidx, D_SSM, 0),
        out_shape=jax.ShapeDtypeStruct((b, t, D_SSM), f32),
        scratch_shapes=[state, pltpu.VMEM((L, D_SSM), f32)],
        compiler_params=_params("parallel", "arbitrary"),
        name="ssd_fwd",
    )(xc, proj3, alog, dtb, proj3, y_rev, dsk, ng)


def _merge_body(ya, yf, ys, g0, g1, g2, s_ref, wa, wf, ws, wo, o_ref):
    def branch(y, g, w):
        return _sigmoid(g[...]) * jnp.dot(y[...].astype(bf16), w[...], preferred_element_type=f32)

    merged = branch(ya, g0, wa) + branch(yf, g1, wf) + branch(ys, g2, ws)
    o_ref[...] = s_ref[...] + jnp.dot(merged.astype(bf16), wo[...], preferred_element_type=f32)


def merge_branches(s2d, proj2d, ya, yf, ys, wa, wf, ws, wo, tm=256):
    m, d = s2d.shape
    row = lambda w: pl.BlockSpec((tm, w), lambda i: (i, 0))
    gate = lambda k: pl.BlockSpec((tm, d), lambda i: (i, COL_GATE // d + k))
    wspec = lambda r: pl.BlockSpec((r, d), lambda i: (0, 0), pipeline_mode=pl.Buffered(1))
    return pl.pallas_call(
        _merge_body,
        grid=(_cdiv(m, tm),),
        in_specs=[row(D_ATT), row(D_FNET), row(D_SSM), gate(0), gate(1), gate(2), row(d),
                  wspec(D_ATT), wspec(D_FNET), wspec(D_SSM), wspec(d)],
        out_specs=row(d),
        out_shape=jax.ShapeDtypeStruct((m, d), f32),
        compiler_params=_params("parallel"),
        name="merge_branches",
    )(ya, yf, ys, proj2d, proj2d, proj2d, s2d, wa, wf, ws, wo)


def _router_body(s_ref, g_ref, wr_ref, tok_ref, afft_ref, *, m_total, tm):
    i = pl.program_id(0)
    x = s_ref[...]
    ms = jnp.mean(x * x, axis=-1, keepdims=True)
    tok = x * lax.rsqrt(ms + EPS) * g_ref[...]
    row = i * tm + lax.broadcasted_iota(i32, (tm, 1), 0)
    tok = jnp.where(row < m_total, tok, 0.0)
    tok_ref[...] = tok.astype(bf16)
    lg = jnp.dot(tok, wr_ref[...], preferred_element_type=f32, precision=HI)
    ex = jnp.exp(lg - jnp.max(lg, axis=1, keepdims=True))
    afft_ref[...] = (ex / jnp.sum(ex, axis=1, keepdims=True)).T


def router(s2d, gain, w_router, tm=512):
    m, d = s2d.shape
    mp = _cdiv(m, ROUTE_PAD) * ROUTE_PAD
    last_blk = _cdiv(m, tm) - 1
    return pl.pallas_call(
        functools.partial(_router_body, m_total=m, tm=tm),
        grid=(mp // tm,),
        in_specs=[pl.BlockSpec((tm, d), lambda i: (jnp.minimum(i, last_blk), 0)),
                  pl.BlockSpec((1, d), lambda i: (0, 0)),
                  pl.BlockSpec((d, N_EXPERTS), lambda i: (0, 0))],
        out_specs=[pl.BlockSpec((tm, d), lambda i: (i, 0)),
                   pl.BlockSpec((N_EXPERTS, tm), lambda i: (0, i))],
        out_shape=[jax.ShapeDtypeStruct((mp, d), bf16),
                   jax.ShapeDtypeStruct((N_EXPERTS, mp), f32)],
        compiler_params=_params("parallel"),
        name="moe_router",
    )(s2d, gain.astype(f32).reshape(1, d), w_router.astype(f32))


def _exclusive_rank(x, nt):
    li = lax.broadcasted_iota(i32, (128, 128), 0)
    lj = lax.broadcasted_iota(i32, (128, 128), 1)
    lane_before = (li < lj).astype(bf16)
    ti = lax.broadcasted_iota(i32, (nt, nt), 0)
    tj = lax.broadcasted_iota(i32, (nt, nt), 1)
    row_before = (tj < ti).astype(bf16)
    xb = x.astype(bf16)
    within = jnp.dot(xb, lane_before, preferred_element_type=f32)
    before = jnp.sum(jnp.dot(row_before, xb, preferred_element_type=f32), axis=1, keepdims=True)
    return within + before


def _count(mask):
    return jnp.sum(jnp.sum(mask.astype(f32), axis=-1, keepdims=True), axis=-2, keepdims=True)


def _select_body(a_ref, sel_ref, *, cap, nt):
    bits = pltpu.bitcast(a_ref[...], i32)

    def step(i, pref):
        cand = pref | jnp.left_shift(jnp.int32(1), 30 - i)
        return jnp.where(_count(bits >= cand) >= cap, cand, pref)

    thr = lax.fori_loop(0, 31, step, jnp.zeros((N_EXPERTS, 1, 1), i32))
    for e in range(N_EXPERTS):
        be = bits[e]
        gt = be > thr[e]
        eq = be == thr[e]
        need = cap - _count(gt)
        take = jnp.logical_and(eq, _exclusive_rank(eq.astype(f32), nt) < need)
        sel_ref[e] = jnp.logical_or(gt, take).astype(f32)


def _rank_body(sel_ref, slot_ref, *, nt):
    for e in range(N_EXPERTS):
        sel = sel_ref[e]
        slot_ref[e] = jnp.where(sel > 0.5, _exclusive_rank(sel, nt), -1.0).astype(i32)


def select_tokens(aff3, cap):
    e, nt, _ = aff3.shape
    return pl.pallas_call(
        functools.partial(_select_body, cap=cap, nt=nt),
        out_shape=jax.ShapeDtypeStruct(aff3.shape, f32),
        compiler_params=pltpu.CompilerParams(vmem_limit_bytes=VMEM_LIMIT),
        name="moe_select",
    )(aff3)


def rank_tokens(sel3):
    e, nt, _ = sel3.shape
    return pl.pallas_call(
        functools.partial(_rank_body, nt=nt),
        out_shape=jax.ShapeDtypeStruct(sel3.shape, i32),
        compiler_params=pltpu.CompilerParams(vmem_limit_bytes=VMEM_LIMIT),
        name="moe_rank",
    )(sel3)


GATE_ROWS = 8


def _gather_body(i_ref, j_ref, f_ref, tok_ref, slot_ref, aff_ref, o_ref, g_ref, acc_ref, gacc_ref, *, ns):
    k = pl.program_id(0) * ns + pl.program_id(1)
    flags = f_ref[k]

    @pl.when((flags & FLAG_FIRST) != 0)
    def _():
        acc_ref[...] = jnp.zeros(acc_ref.shape, f32)
        gacc_ref[...] = jnp.zeros(gacc_ref.shape, f32)

    @pl.when((flags & FLAG_VALID) != 0)
    def _():
        want = lax.broadcasted_iota(i32, (SLOT_BLK, 1), 0) + j_ref[k] * SLOT_BLK
        onehot = jnp.where(slot_ref[...] == want, 1.0, 0.0).astype(bf16)
        acc_ref[...] += jnp.dot(onehot, tok_ref[...], preferred_element_type=f32)
        g = aff_ref[...]
        g0 = g.astype(bf16).astype(f32)
        g1 = (g - g0).astype(bf16).astype(f32)
        g2 = g - g0 - g1
        piece = lax.broadcasted_iota(i32, (GATE_ROWS, 1), 0)
        pieces = jnp.where(piece == 0, g0, jnp.where(piece == 1, g1, jnp.where(piece == 2, g2, 0.0)))
        gacc_ref[...] += lax.dot_general(pieces.astype(bf16), onehot, (((1,), (1,)), ((), ())),
                                         preferred_element_type=f32)

    @pl.when((flags & FLAG_LAST) != 0)
    def _():
        o_ref[...] = acc_ref[...].astype(bf16)
        g_ref[...] = gacc_ref[...]


def gather_tokens(tok, slot_row3, afft3, sched, cap_pad, ns):
    d = tok.shape[1]
    ii, jj, ff = sched
    row = pl.BlockSpec((None, 1, GATHER_TOK), lambda e, s, ii, jj, ff: (e, 0, ii[e * ns + s]))
    return pl.pallas_call(
        functools.partial(_gather_body, ns=ns),
        grid_spec=pltpu.PrefetchScalarGridSpec(
            num_scalar_prefetch=3,
            grid=(N_EXPERTS, ns),
            in_specs=[pl.BlockSpec((GATHER_TOK, d), lambda e, s, ii, jj, ff: (ii[e * ns + s], 0)), row, row],
            out_specs=[pl.BlockSpec((None, SLOT_BLK, d), lambda e, s, ii, jj, ff: (e, jj[e * ns + s], 0)),
                       pl.BlockSpec((None, GATE_ROWS, SLOT_BLK), lambda e, s, ii, jj, ff: (e, 0, jj[e * ns + s]))],
            scratch_shapes=[pltpu.VMEM((SLOT_BLK, d), f32), pltpu.VMEM((GATE_ROWS, SLOT_BLK), f32)]),
        out_shape=[jax.ShapeDtypeStruct((N_EXPERTS, cap_pad, d), bf16),
                   jax.ShapeDtypeStruct((N_EXPERTS, GATE_ROWS, cap_pad), f32)],
        compiler_params=_params("parallel", "arbitrary"),
        name="moe_gather",
    )(ii, jj, ff, tok, slot_row3, afft3)


def _ffn_body(x_ref, g_ref, wg_ref, wu_ref, wd_ref, o_ref, acc_ref):
    fi = pl.program_id(2)

    @pl.when(fi == 0)
    def _():
        acc_ref[...] = jnp.zeros(acc_ref.shape, f32)

    x = x_ref[...]
    gt = jnp.dot(x, wg_ref[...].astype(bf16), preferred_element_type=f32)
    up = jnp.dot(x, wu_ref[...].astype(bf16), preferred_element_type=f32)
    hid = (gt * _sigmoid(gt) * up).astype(bf16)
    acc_ref[...] += jnp.dot(hid, wd_ref[...].astype(bf16), preferred_element_type=f32)

    @pl.when(fi == pl.num_programs(2) - 1)
    def _():
        gate = g_ref[0:1, :] + g_ref[1:2, :] + g_ref[2:3, :]
        o_ref[...] = (acc_ref[...].T * gate).astype(bf16)


def expert_ffn(xe, gslot, w_gate, w_up, w_down, layer, tf=512, tm_max=768):
    e, cap_pad, d = xe.shape
    dff = w_gate.shape[-1]
    nm = 1
    while cap_pad // nm > tm_max or cap_pad % nm or (cap_pad // nm) % 128:
        nm += 1
    tm = cap_pad // nm
    return pl.pallas_call(
        _ffn_body,
        grid=(e, nm, dff // tf),
        in_specs=[pl.BlockSpec((None, tm, d), lambda ei, mi, fi: (ei, mi, 0)),
                  pl.BlockSpec((None, GATE_ROWS, tm), lambda ei, mi, fi: (ei, 0, mi)),
                  pl.BlockSpec((None, None, d, tf), lambda ei, mi, fi: (layer, ei, 0, fi)),
                  pl.BlockSpec((None, None, d, tf), lambda ei, mi, fi: (layer, ei, 0, fi)),
                  pl.BlockSpec((None, None, tf, d), lambda ei, mi, fi: (layer, ei, fi, 0))],
        out_specs=pl.BlockSpec((None, d, tm), lambda ei, mi, fi: (ei, 0, mi)),
        out_shape=jax.ShapeDtypeStruct((e, d, cap_pad), bf16),
        scratch_shapes=[pltpu.VMEM((tm, d), f32)],
        compiler_params=_params("parallel", "parallel", "arbitrary"),
        name="moe_expert_ffn",
    )(xe, gslot, w_gate, w_up, w_down)


def _combine_body(fetch_ref, match_ref, any_ref, s_ref, slot_ref, *rest):
    ye_refs, (o_ref, acc_ref) = rest[:N_EXPERTS], rest[N_EXPERTS:]
    r = pl.program_id(1)
    step = pl.program_id(0) * COMBINE_ROUNDS + r

    @pl.when(r == 0)
    def _():
        acc_ref[...] = jnp.zeros(acc_ref.shape, f32)

    @pl.when(any_ref[step] != 0)
    def _():
        row = lax.broadcasted_iota(i32, (SLOT_BLK, 1), 0)
        total = None
        for e in range(N_EXPERTS):
            want = row + match_ref[step * N_EXPERTS + e] * SLOT_BLK
            onehot = jnp.where(slot_ref[e] == want, 1.0, 0.0).astype(bf16)
            part = jnp.dot(ye_refs[e][...], onehot, preferred_element_type=f32)
            total = part if total is None else part + total
        acc_ref[...] += total

    @pl.when(r == COMBINE_ROUNDS - 1)
    def _():
        o_ref[...] = s_ref[...] + acc_ref[...].T


def combine_tokens(s2d, yet, slot_row3, sched):
    m, d = s2d.shape
    fetch, match, anyv = sched

    def ye_spec(e):
        return pl.BlockSpec((None, d, SLOT_BLK),
                            lambda i, r, fetch, match, anyv: (e, 0, fetch[(i * COMBINE_ROUNDS + r) * N_EXPERTS + e]))

    return pl.pallas_call(
        _combine_body,
        grid_spec=pltpu.PrefetchScalarGridSpec(
            num_scalar_prefetch=3,
            grid=(_cdiv(m, COMBINE_TOK), COMBINE_ROUNDS),
            in_specs=[pl.BlockSpec((COMBINE_TOK, d), lambda i, r, *_: (i, 0)),
                      pl.BlockSpec((N_EXPERTS, 1, COMBINE_TOK), lambda i, r, *_: (0, 0, i))]
                     + [ye_spec(e) for e in range(N_EXPERTS)],
            out_specs=pl.BlockSpec((COMBINE_TOK, d), lambda i, r, *_: (i, 0)),
            scratch_shapes=[pltpu.VMEM((d, COMBINE_TOK), f32)]),
        out_shape=jax.ShapeDtypeStruct((m, d), f32),
        compiler_params=_params("parallel", "arbitrary"),
        name="moe_combine",
    )(fetch, match, anyv, s2d, slot_row3, *([yet] * N_EXPERTS))


def _pair_lists(slot, m, tok_size, n_slot_blk):
    e = slot.shape[0]
    n_tok_blk = _cdiv(m, tok_size)
    cnt = jnp.sum((slot[:, :n_tok_blk * tok_size] >= 0).reshape(e, n_tok_blk, tok_size), axis=-1).astype(i32)
    cum_in = jnp.cumsum(cnt, axis=1)
    cum_ex = cum_in - cnt
    jlo = cum_ex // SLOT_BLK
    jhi = (cum_in - 1) // SLOT_BLK
    npairs = jnp.where(cnt > 0, jhi - jlo + 1, 0)
    off_in = jnp.cumsum(npairs, axis=1)
    off_ex = off_in - npairs
    total = off_in[:, -1:]
    ns = n_tok_blk + n_slot_blk
    step = jnp.arange(ns, dtype=i32)[None, :]
    valid = step < total
    sc = jnp.minimum(step, total - 1)
    tok_blk = jnp.sum(off_in[:, None, :] <= sc[:, :, None], axis=-1).astype(i32)
    tok_blk = jnp.minimum(tok_blk, n_tok_blk - 1)
    slot_blk = (jnp.take_along_axis(jlo, tok_blk, axis=1)
                + sc - jnp.take_along_axis(off_ex, tok_blk, axis=1)).astype(i32)
    return tok_blk, slot_blk, valid


def _gather_schedule(slot, m, n_slot_blk):
    tok_blk, slot_blk, valid = _pair_lists(slot, m, GATHER_TOK, n_slot_blk)
    e, ns = tok_blk.shape
    change = slot_blk[:, 1:] != slot_blk[:, :-1]
    first = jnp.concatenate([jnp.ones((e, 1), bool), change], axis=1)
    last = jnp.concatenate([change | ~valid[:, 1:], jnp.ones((e, 1), bool)], axis=1) & valid
    flags = first * FLAG_FIRST + valid * FLAG_VALID + last * FLAG_LAST
    return tuple(a.reshape(-1).astype(i32) for a in (tok_blk, slot_blk, flags)), ns


def _combine_schedule(slot, m):
    e = slot.shape[0]
    n_tok_blk = _cdiv(m, COMBINE_TOK)
    cnt = jnp.sum((slot[:, :n_tok_blk * COMBINE_TOK] >= 0).reshape(e, n_tok_blk, COMBINE_TOK), axis=-1).astype(i32)
    cum_in = jnp.cumsum(cnt, axis=1)
    jlo = ((cum_in - cnt) // SLOT_BLK)[:, :, None]
    jhi = (cum_in - 1) // SLOT_BLK
    has = (cnt > 0)[:, :, None]
    npairs = jnp.where(has, jhi[:, :, None] - jlo + 1, 0)
    held = lax.cummax(jnp.where(cnt > 0, jhi, 0), axis=1)[:, :, None]
    rnd = jnp.arange(COMBINE_ROUNDS, dtype=i32)[None, None, :]
    valid = rnd < npairs
    fetch = jnp.where(has, jlo + jnp.minimum(rnd, npairs - 1), held)
    match = jnp.where(valid, jlo + rnd, NO_SLOT_BLK)
    flat = lambda a: jnp.transpose(a, (1, 2, 0)).reshape(-1).astype(i32)
    return flat(fetch), flat(match), jnp.any(valid, axis=0).reshape(-1).astype(i32)


def expert_choice_ffn(s3, gain, w_router, w_gate, w_up, w_down, layer):
    b, t, d = s3.shape
    m = b * t
    cap = (EC_CAPACITY * m) // N_EXPERTS
    s2d = s3.reshape(m, d)
    tok, afft = router(s2d, gain, w_router)

    n_pad = _cdiv(m, ROUTE_PAD) * ROUTE_PAD
    nt = n_pad // 128
    a = jnp.roll(afft[:, :m].reshape(N_EXPERTS, b, t), N_META, axis=2).reshape(N_EXPERTS, m)
    a = jnp.pad(a, ((0, 0), (0, n_pad - m)), constant_values=-1.0)
    sel = select_tokens(a.reshape(N_EXPERTS, nt, 128), cap).reshape(N_EXPERTS, n_pad)
    sel = jnp.roll(sel[:, :m].reshape(N_EXPERTS, b, t), -N_META, axis=2).reshape(N_EXPERTS, m)
    sel = jnp.pad(sel, ((0, 0), (0, n_pad - m)))
    slot = rank_tokens(sel.reshape(N_EXPERTS, nt, 128)).reshape(N_EXPERTS, n_pad)

    n_slot_blk = _cdiv(cap, SLOT_BLK)
    cap_pad = n_slot_blk * SLOT_BLK
    g_sched, ns = _gather_schedule(slot, m, n_slot_blk)
    slot_row3 = slot.reshape(N_EXPERTS, 1, n_pad)
    xe, gslot = gather_tokens(tok, slot_row3, afft.reshape(N_EXPERTS, 1, n_pad), g_sched, cap_pad, ns)
    yet = expert_ffn(xe, gslot, w_gate, w_up, w_down, layer)
    out = combine_tokens(s2d, yet, slot_row3, _combine_schedule(slot, m))
    return out.reshape(b, t, d)


def _final_norm_body(x_ref, g_ref, o_ref):
    x = x_ref[...]
    ms = jnp.mean(x * x, axis=-1, keepdims=True)
    o_ref[...] = x * lax.rsqrt(ms + EPS) * g_ref[...]


def final_norm(s3, gain, g, tm=512):
    b, t, d = s3.shape
    assert g % tm == 0
    return pl.pallas_call(
        _final_norm_body,
        grid=(b, g // tm),
        in_specs=[pl.BlockSpec((None, tm, d), lambda bi, i: (bi, i, 0)),
                  pl.BlockSpec((1, d), lambda bi, i: (0, 0))],
        out_specs=pl.BlockSpec((None, tm, d), lambda bi, i: (bi, i, 0)),
        out_shape=jax.ShapeDtypeStruct((b, g, d), f32),
        compiler_params=_params("parallel", "parallel"),
        name="final_norm",
    )(s3, gain.astype(f32).reshape(1, d))


def _reorder_w_in(w):
    cuts = np.cumsum([3 * D_ATT, D_FNET, D_SSM_CONV, D_SSM, 2 * SSM_HEADS])
    qkv, u_f, xbc, z, dt, gate = jnp.split(w, [int(c) for c in cuts], axis=1)
    dt = jnp.pad(dt, ((0, 0), (0, N_PROJ - COL_DT - 2 * SSM_HEADS)))
    return jnp.concatenate([gate, qkv, u_f, xbc, z, dt], axis=1).astype(bf16)


def encoder_layer(s3, rows, layer, lw):
    b, t, d = s3.shape
    m = b * t
    proj = norm_matmul(s3.reshape(m, d), lw["norm1_gain"], lw["w_in"])
    proj3 = proj.reshape(b, t, N_PROJ)
    y_att = neighbourhood_attention(proj3, lw["bias_tab"], lw["meta_bias"], rows)
    y_f = fourier_mix(proj3, COL_F, N_META)
    y_s = ssd_mixer(proj3, rows, lw["conv_w"], lw["conv_b"], lw["a_log"], lw["dt_bias"], lw["d_skip"],
                    lw["ssd_norm_gain"])
    s2d = merge_branches(s3.reshape(m, d), proj, y_att.reshape(m, -1), y_f.reshape(m, -1), y_s.reshape(m, -1),
                         lw["w_branch_a"], lw["w_branch_f"], lw["w_branch_s"], lw["w_out"])
    return expert_choice_ffn(s2d.reshape(b, t, d), lw["norm2_gain"], lw["w_router"],
                             lw["w_exp_gate"], lw["w_exp_up"], lw["w_exp_down"], layer)


def encode(x, meta_tokens, final_gain, layers):
    b, g, d = x.shape
    rows = g // GRID_W
    meta = jnp.broadcast_to(meta_tokens.astype(x.dtype)[None], (b, N_META, d))
    s = jnp.concatenate([x, meta], axis=1)
    for layer, lw in enumerate(layers):
        s = encoder_layer(s, rows, layer, lw)
    return final_norm(s, final_gain, g)


def kernel(x_prompt, x_sample, meta_tokens, norm1_gain, w_in, rel_bias, meta_bias, conv_w, conv_b, a_log, dt_bias,
           d_skip, ssd_norm_gain, w_branch_a, w_branch_f, w_branch_s, w_out, norm2_gain, w_router, w_exp_gate,
           w_exp_up, w_exp_down, final_gain):
    depth = w_in.shape[0]
    rows_set = {x_prompt.shape[1] // GRID_W, x_sample.shape[1] // GRID_W}
    layers = []
    for l in range(depth):
        rb = rel_bias[l]
        layers.append({
            "norm1_gain": norm1_gain[l].astype(f32), "w_in": _reorder_w_in(w_in[l]),
            "rel_bias": rb, "meta_bias": meta_bias[l],
            "conv_w": conv_w[l], "conv_b": conv_b[l], "a_log": a_log[l], "dt_bias": dt_bias[l],
            "d_skip": d_skip[l], "ssd_norm_gain": ssd_norm_gain[l],
            "w_branch_a": w_branch_a[l].astype(bf16), "w_branch_f": w_branch_f[l].astype(bf16),
            "w_branch_s": w_branch_s[l].astype(bf16), "w_out": w_out[l].astype(bf16),
            "norm2_gain": norm2_gain[l], "w_router": w_router[l],
            "w_exp_gate": w_exp_gate, "w_exp_up": w_exp_up, "w_exp_down": w_exp_down,
            "bias_tab": attention_bias_tables(rb, max(rows_set)),
        })
    y_prompt = encode(x_prompt, meta_tokens, final_gain, layers)
    y_sample = encode(x_sample, meta_tokens, final_gain, layers)
    return (y_prompt, y_sample)
```
